```python
import jax
import jax.numpy as jnp
from jax import lax
import numpy as np

D_MODEL = 1024
BATCH = 8
SEQ = 2048
DEPTH = 2
DEC_BATCH = 32
DEC_SEQ = 4
PAST_LEN = 16384
PAGE_SIZE = 128

HEAD_DIM = 64
N_GROUP_HEADS = 4
GROUP_WIDTH = N_GROUP_HEADS * HEAD_DIM
D_MIX = 4 * GROUP_WIDTH
D_FF = 2816
N_FFN = 2
N_NORMS = 6
EPS = 1e-6

GLA_DK = HEAD_DIM // 2
GLA_DV = HEAD_DIM
GLA_RANK = 16
GLA_TAU = 16.0
GLA_CHUNK = 64

GMLP_CHUNK = 128
LN_EPS = 1e-5

RWKV_N = HEAD_DIM
RWKV_W_RANK = 64
RWKV_A_RANK = 64
RWKV_G_RANK = 128
RWKV_LNX_EPS = 64e-5

DIL_PAIRS = ((128, 1), (512, 4), (2048, 16))
DIL_N_GROUPS = len(DIL_PAIRS)
N_DIL_HEADS = DIL_N_GROUPS * N_GROUP_HEADS
DIL_QBLOCK = 128
ALIBI_MAX_EXP = 8.0

GLA_COLS = 2 * N_GROUP_HEADS * GLA_DK + GROUP_WIDTH + GLA_RANK + GROUP_WIDTH
GMLP_COLS = 2 * GROUP_WIDTH
RWKV_COLS = 3 * GROUP_WIDTH + RWKV_W_RANK + RWKV_A_RANK + RWKV_G_RANK
DIL_COLS = DIL_N_GROUPS * 3 * GROUP_WIDTH
N_IN_COLS = GLA_COLS + GMLP_COLS + RWKV_COLS + DIL_COLS

kernel_name = 'hybrid_parallel_groups_decode_step'


def rms_norm(x, g):
    xf = x.astype(jnp.float32)
    y = xf * lax.rsqrt(jnp.mean(xf * xf, axis=-1, keepdims=True) + EPS)
    return (y * g.astype(jnp.float32)).astype(x.dtype)


def swiglu(x, wg, wu, wd):
    return (jax.nn.silu(x @ wg) * (x @ wu)) @ wd


def split_at(x, sizes):
    return jnp.split(x, np.cumsum(sizes)[:-1].tolist(), axis=-1)


def gla_mix(cols, w_alpha2, b_alpha, norm_g, h0):
    B, T, _ = cols.shape
    H = N_GROUP_HEADS
    f32 = jnp.float32
    q, k, v, a_lr, g = split_at(cols.astype(f32), [H * GLA_DK, H * GLA_DK, GROUP_WIDTH, GLA_RANK, GROUP_WIDTH])
    q = q.reshape(B, T, H, GLA_DK) * GLA_DK ** -0.5
    k = k.reshape(B, T, H, GLA_DK)
    v = v.reshape(B, T, H, GLA_DV)
    log_a = (jax.nn.log_sigmoid(a_lr @ w_alpha2.astype(f32) + b_alpha.astype(f32)) / GLA_TAU).reshape(B, T, H, GLA_DK)
    C = min(GLA_CHUNK, T)
    pad = (-T) % C
    n = (T + pad) // C

    def to_chunks(z):
        return jnp.pad(z, ((0, 0), (0, pad), (0, 0), (0, 0))).reshape(B, n, C, H, z.shape[-1])

    q, k, v, log_a = (to_chunks(z) for z in (q, k, v, log_a))
    b = jnp.cumsum(log_a, axis=2)
    b_last = b[:, :, -1:]
    qe = q * jnp.exp(b)
    ke = k * jnp.exp(-b)
    kd = k * jnp.exp(b_last - b)
    causal = jnp.tril(jnp.ones((C, C), dtype=bool))
    att = jnp.where(causal, jnp.einsum('bnthk,bnshk->bnhts', qe, ke), 0.0)
    o_intra = jnp.einsum('bnhts,bnshv->bnthv', att, v)
    incr = jnp.einsum('bnshk,bnshv->bnhkv', kd, v)
    decay = jnp.exp(b_last[:, :, 0])

    def step(S, xs):
        dec, inc = xs
        return dec[..., None] * S + inc, S

    s_final, s_in = lax.scan(step, h0.astype(f32), (jnp.moveaxis(decay, 1, 0), jnp.moveaxis(incr, 1, 0)))
    s_in = jnp.moveaxis(s_in, 0, 1)
    o = o_intra + jnp.einsum('bnthk,bnhkv->bnthv', qe, s_in)
    o = o.reshape(B, n * C, H, GLA_DV)[:, :T]
    o = o * lax.rsqrt(jnp.mean(o * o, axis=-1, keepdims=True) + EPS)
    o = o.reshape(B, T, GROUP_WIDTH) * norm_g.astype(f32) * jax.nn.silu(g)
    return o.astype(cols.dtype), s_final


def gmlp_mix(cols, ln_g, ln_b, w_s, b_s):
    B, T, _ = cols.shape
    f32 = jnp.float32
    u, v = jnp.split(jax.nn.gelu(cols.astype(f32), approximate=False), 2, axis=-1)
    mean = jnp.mean(v, axis=-1, keepdims=True)
    var = jnp.mean(jnp.square(v - mean), axis=-1, keepdims=True)
    v = (v - mean) * lax.rsqrt(var + LN_EPS) * ln_g.astype(f32) + ln_b.astype(f32)
    pad = (-T) % GMLP_CHUNK
    n = (T + pad) // GMLP_CHUNK
    vc = jnp.pad(v, ((0, 0), (0, pad), (0, 0))).reshape(B, n, GMLP_CHUNK, N_GROUP_HEADS, HEAD_DIM)
    causal = jnp.tril(jnp.ones((GMLP_CHUNK, GMLP_CHUNK), dtype=bool))
    w = jnp.where(causal, w_s.astype(f32), 0.0)
    s = jnp.einsum('gts,bnsgc->bntgc', w, vc) + b_s.astype(f32).T[None, None, :, :, None]
    s = s.reshape(B, n * GMLP_CHUNK, GROUP_WIDTH)[:, :T]
    return (u * s).astype(cols.dtype), v.astype(cols.dtype)


def rwkv_mix(cols, shift_prev, s0, mu, w0, w2, a0, a2, g2, k_k, k_a, r_k, lnx_g, lnx_b):
    B, T, _ = cols.shape
    H, N = N_GROUP_HEADS, RWKV_N
    f32 = jnp.float32
    c = cols.astype(f32)
    prev = jnp.concatenate([shift_prev.astype(f32)[:, None], c[:, :-1]], axis=1)
    xm = c + (prev - c) * mu.astype(f32)
    r, k, v, w_lr, a_lr, g_lr = split_at(xm, [GROUP_WIDTH] * 3 + [RWKV_W_RANK, RWKV_A_RANK, RWKV_G_RANK])
    w = -jax.nn.softplus(-(w0.astype(f32) + jnp.tanh(w_lr) @ w2.astype(f32))) - 0.5
    decay = jnp.exp(-jnp.exp(w))
    a = jax.nn.sigmoid(a0.astype(f32) + a_lr @ a2.astype(f32))
    g = jax.nn.sigmoid(g_lr) @ g2.astype(f32)

    def heads(z):
        return z.reshape(B, T, H, N)

    kk = heads(k * k_k.astype(f32))
    kk = kk / jnp.maximum(jnp.linalg.norm(kk, axis=-1, keepdims=True), 1e-12)
    k = k * (1.0 + (a - 1.0) * k_a.astype(f32))
    r, k, v, decay, a = heads(r), heads(k), heads(v), heads(decay), heads(a)

    def step(S, xs):
        r_t, w_t, k_t, v_t, a_t, b_t = xs
        sa = jnp.einsum('bhvk,bhk->bhv', S, a_t)
        S = S * w_t[:, :, None, :] + sa[..., None] * b_t[:, :, None, :] + v_t[..., None] * k_t[:, :, None, :]
        return S, jnp.einsum('bhvk,bhk->bhv', S, r_t)

    def tm(z):
        return jnp.moveaxis(z, 1, 0)

    s_final, y = lax.scan(step, s0.astype(f32), (tm(r), tm(decay), tm(k), tm(v), tm(-kk), tm(kk * a)))
    y = jnp.moveaxis(y, 0, 1)
    mean = jnp.mean(y, axis=-1, keepdims=True)
    var = jnp.mean(jnp.square(y - mean), axis=-1, keepdims=True)
    y = ((y - mean) * lax.rsqrt(var + RWKV_LNX_EPS)).reshape(B, T, GROUP_WIDTH) * lnx_g.astype(f32) + lnx_b.astype(f32)
    bonus = jnp.sum(r * k * r_k.astype(f32).reshape(H, N), axis=-1, keepdims=True) * v
    o = (y + bonus.reshape(B, T, GROUP_WIDTH)) * g
    return o.astype(cols.dtype), s_final, cols[:, -1]


def dilated_group(q, kv_new, past, window, dilation, slopes):
    B, T, H, dh = q.shape
    L = past.shape[1]
    f32 = jnp.float32
    J = window // dilation + 1
    ext = jnp.concatenate([jnp.zeros((B, window - L, 2, H, dh), kv_new.dtype), past.astype(kv_new.dtype), kv_new], axis=1)
    qb = T if T <= DIL_QBLOCK else DIL_QBLOCK
    nb = T // qb
    offs = jnp.arange(J) * dilation
    bias = -slopes[:, None] * offs[None, :].astype(f32)

    def block(args):
        nblk, qblk = args
        i = nblk * qb + jnp.arange(qb)
        idx = window + i[:, None] - offs[None, :]
        kvg = jnp.take(ext, idx, axis=1)
        valid = (i[:, None] - offs[None, :]) >= -L
        s = jnp.einsum('bqhd,bqjhd->bhqj', qblk, kvg[:, :, :, 0]).astype(f32) + bias[None, :, None, :]
        s = jnp.where(valid[None, None], s, -jnp.inf)
        m = jnp.max(s, axis=-1, keepdims=True)
        p = jnp.exp(s - m)
        den = jnp.sum(p, axis=-1)
        o = jnp.einsum('bhqj,bqjhd->bqhd', p, kvg[:, :, :, 1].astype(f32)) / jnp.transpose(den, (0, 2, 1))[..., None]
        lse = jnp.transpose(m[..., 0] + jnp.log(den), (0, 2, 1))
        return o, lse

    qs = jnp.swapaxes(q.reshape(B, nb, qb, H, dh), 0, 1)
    o, lse = lax.map(block, (jnp.arange(nb), qs))
    o = jnp.swapaxes(o, 0, 1).reshape(B, T, H, dh)
    lse = jnp.swapaxes(lse, 0, 1).reshape(B, T, H)
    return o, lse


def dilated_mix(cols, pasts):
    B, T, _ = cols.shape
    qkv = cols.reshape(B, T, DIL_N_GROUPS, 3, N_GROUP_HEADS, HEAD_DIM)
    slopes = jnp.exp2(-ALIBI_MAX_EXP * (jnp.arange(N_DIL_HEADS, dtype=jnp.float32) + 1.0) / N_DIL_HEADS)
    slopes = slopes.reshape(DIL_N_GROUPS, N_GROUP_HEADS)
    outs, lses, new_kv = [], [], []
    for gi, (window, dilation) in enumerate(DIL_PAIRS):
        q = qkv[:, :, gi, 0] * HEAD_DIM ** -0.5
        kv = qkv[:, :, gi, 1:]
        o, lse = dilated_group(q, kv, pasts[gi], window, dilation, slopes[gi])
        outs.append(o)
        lses.append(lse)
        new_kv.append(kv[:, -min(window, T):])
    wts = jax.nn.softmax(jnp.stack(lses, axis=0), axis=0)
    o = jnp.sum(wts[..., None] * jnp.stack(outs, axis=0), axis=0)
    return o.reshape(B, T, GROUP_WIDTH).astype(cols.dtype), new_kv


def trunk_layer(x, p, gla_h0, rwkv_s0, shift0, pasts):
    ng = p['norms']
    h = x + 0.5 * rms_norm(swiglu(rms_norm(x, ng[0]), p['ff_gate'][0], p['ff_up'][0], p['ff_down'][0]), ng[1])
    cols = rms_norm(h, ng[2]) @ p['w_in']
    c_gla, c_gmlp, c_rwkv, c_dil = split_at(cols, [GLA_COLS, GMLP_COLS, RWKV_COLS, DIL_COLS])
    o_gla, gla_s = gla_mix(c_gla, p['gla_w_alpha2'], p['gla_b_alpha'], p['gla_norm'], gla_h0)
    o_gmlp, gmlp_v = gmlp_mix(c_gmlp, p['gmlp_ln_g'], p['gmlp_ln_b'], p['gmlp_ws'], p['gmlp_bs'])
    o_rwkv, rwkv_s, shift = rwkv_mix(c_rwkv, shift0, rwkv_s0, p['rwkv_mu'], p['rwkv_w0'], p['rwkv_w2'],
                                     p['rwkv_a0'], p['rwkv_a2'], p['rwkv_g2'], p['rwkv_kk'], p['rwkv_ka'],
                                     p['rwkv_rk'], p['rwkv_lnx_g'], p['rwkv_lnx_b'])
    o_dil, kv_new = dilated_mix(c_dil, pasts)
    mix = jnp.concatenate([o_gla, o_gmlp, o_rwkv, o_dil], axis=-1) @ p['w_out']
    h = h + rms_norm(mix, ng[3])
    h = h + 0.5 * rms_norm(swiglu(rms_norm(h, ng[4]), p['ff_gate'][1], p['ff_up'][1], p['ff_down'][1]), ng[5])
    return h, gla_s, rwkv_s, shift, gmlp_v, kv_new


def setup_inputs(seed: int = 0) -> dict:
    key = jax.random.key(seed)
    ks = iter(jax.random.split(key, 48))
    f32 = jnp.float32

    def nrm(shape, scale):
        return jax.random.normal(next(ks), shape, f32) * scale

    def unif(shape, lo, hi):
        return jax.random.uniform(next(ks), shape, f32, minval=lo, maxval=hi)

    inp = {}
    inp['x_prompt'] = nrm((BATCH, SEQ, D_MODEL), 1.0)
    inp['x_sample'] = nrm((DEC_BATCH, DEC_SEQ, D_MODEL), 1.0)
    for window, _ in DIL_PAIRS:
        inp['cache_win%d' % window] = nrm((DEPTH, DEC_BATCH, min(window, PAST_LEN), 2, N_GROUP_HEADS, HEAD_DIM), 1.0)
    inp['state_gla'] = nrm((DEPTH, DEC_BATCH, N_GROUP_HEADS, GLA_DK, GLA_DV), 1.0)
    inp['state_rwkv'] = nrm((DEPTH, DEC_BATCH, N_GROUP_HEADS, RWKV_N, RWKV_N), 0.5)
    inp['state_shift'] = nrm((DEPTH, DEC_BATCH, RWKV_COLS), 1.0)
    inp['norm_gains'] = 1.0 + nrm((DEPTH, N_NORMS, D_MODEL), 0.05)
    inp['w_ff_gate'] = nrm((DEPTH, N_FFN, D_MODEL, D_FF), D_MODEL ** -0.5)
    inp['w_ff_up'] = nrm((DEPTH, N_FFN, D_MODEL, D_FF), D_MODEL ** -0.5)
    inp['w_ff_down'] = nrm((DEPTH, N_FFN, D_FF, D_MODEL), D_FF ** -0.5)
    inp['w_in'] = nrm((DEPTH, D_MODEL, N_IN_COLS), D_MODEL ** -0.5)
    inp['w_out'] = nrm((DEPTH, D_MIX, D_MODEL), D_MIX ** -0.5)
    inp['gla_w_alpha2'] = nrm((DEPTH, GLA_RANK, N_GROUP_HEADS * GLA_DK), GLA_RANK ** -0.5)
    inp['gla_b_alpha'] = nrm((DEPTH, N_GROUP_HEADS * GLA_DK), 0.1)
    inp['gla_norm'] = 1.0 + nrm((DEPTH, GROUP_WIDTH), 0.05)
    inp['gmlp_ln_g'] = 1.0 + nrm((DEPTH, GROUP_WIDTH), 0.05)
    inp['gmlp_ln_b'] = nrm((DEPTH, GROUP_WIDTH), 0.05)
    inp['gmlp_ws'] = nrm((DEPTH, N_GROUP_HEADS, GMLP_CHUNK, GMLP_CHUNK), GMLP_CHUNK ** -0.5)
    inp['gmlp_bs'] = 1.0 + nrm((DEPTH, N_GROUP_HEADS, GMLP_CHUNK), 0.05)
    inp['rwkv_mu'] = unif((DEPTH, RWKV_COLS), 0.0, 1.0)
    inp['rwkv_w0'] = unif((DEPTH, GROUP_WIDTH), -6.5, -1.5)
    inp['rwkv_w2'] = nrm((DEPTH, RWKV_W_RANK, GROUP_WIDTH), 0.1)
    inp['rwkv_a0'] = nrm((DEPTH, GROUP_WIDTH), 0.1)
    inp['rwkv_a2'] = nrm((DEPTH, RWKV_A_RANK, GROUP_WIDTH), RWKV_A_RANK ** -0.5)
    inp['rwkv_g2'] = nrm((DEPTH, RWKV_G_RANK, GROUP_WIDTH), RWKV_G_RANK ** -0.5)
    inp['rwkv_kk'] = 0.85 + nrm((DEPTH, GROUP_WIDTH), 0.05)
    inp['rwkv_ka'] = 1.0 + nrm((DEPTH, GROUP_WIDTH), 0.05)
    inp['rwkv_rk'] = nrm((DEPTH, GROUP_WIDTH), 0.1)
    inp['rwkv_lnx_g'] = 1.0 + nrm((DEPTH, GROUP_WIDTH), 0.05)
    inp['rwkv_lnx_b'] = nrm((DEPTH, GROUP_WIDTH), 0.05)
    return inp


def reference(x_prompt, x_sample, cache_win128, cache_win512, cache_win2048, state_gla, state_rwkv, state_shift,
              norm_gains, w_ff_gate, w_ff_up, w_ff_down, w_in, w_out, gla_w_alpha2, gla_b_alpha, gla_norm,
              gmlp_ln_g, gmlp_ln_b, gmlp_ws, gmlp_bs, rwkv_mu, rwkv_w0, rwkv_w2, rwkv_a0, rwkv_a2, rwkv_g2,
              rwkv_kk, rwkv_ka, rwkv_rk, rwkv_lnx_g, rwkv_lnx_b):
    f32 = jnp.float32
    Bp = x_prompt.shape[0]
    caches = (cache_win128, cache_win512, cache_win2048)
    gla0 = jnp.zeros((Bp, N_GROUP_HEADS, GLA_DK, GLA_DV), f32)
    rwkv0 = jnp.zeros((Bp, N_GROUP_HEADS, RWKV_N, RWKV_N), f32)
    shift0 = jnp.zeros((Bp, RWKV_COLS), x_prompt.dtype)
    past0 = jnp.zeros((Bp, 0, 2, N_GROUP_HEADS, HEAD_DIM), x_prompt.dtype)
    yp, ys = x_prompt, x_sample
    pg, pr, psh, pw = [], [], [], [[], [], []]
    sg, sr, ssh, sw, sv = [], [], [], [[], [], []], []
    for l in range(DEPTH):
        p = dict(norms=norm_gains[l], ff_gate=w_ff_gate[l], ff_up=w_ff_up[l], ff_down=w_ff_down[l],
                 w_in=w_in[l], w_out=w_out[l], gla_w_alpha2=gla_w_alpha2[l], gla_b_alpha=gla_b_alpha[l],
                 gla_norm=gla_norm[l], gmlp_ln_g=gmlp_ln_g[l], gmlp_ln_b=gmlp_ln_b[l], gmlp_ws=gmlp_ws[l],
                 gmlp_bs=gmlp_bs[l], rwkv_mu=rwkv_mu[l], rwkv_w0=rwkv_w0[l], rwkv_w2=rwkv_w2[l],
                 rwkv_a0=rwkv_a0[l], rwkv_a2=rwkv_a2[l], rwkv_g2=rwkv_g2[l], rwkv_kk=rwkv_kk[l],
                 rwkv_ka=rwkv_ka[l], rwkv_rk=rwkv_rk[l], rwkv_lnx_g=rwkv_lnx_g[l], rwkv_lnx_b=rwkv_lnx_b[l])
        yp, g_p, r_p, sh_p, _, kv_p = trunk_layer(yp, p, gla0, rwkv0, shift0, [past0, past0, past0])
        ys, g_s, r_s, sh_s, v_s, kv_s = trunk_layer(ys, p, state_gla[l], state_rwkv[l], state_shift[l],
                                                    [c[l] for c in caches])
        pg.append(g_p)
        pr.append(r_p)
        psh.append(sh_p)
        sg.append(g_s)
        sr.append(r_s)
        ssh.append(sh_s)
        sv.append(v_s)
        for gi in range(DIL_N_GROUPS):
            pw[gi].append(kv_p[gi])
            sw[gi].append(kv_s[gi])
    return (yp, ys,
            jnp.stack(pg), jnp.stack(pr), jnp.stack(psh),
            jnp.stack(pw[0]), jnp.stack(pw[1]), jnp.stack(pw[2]),
            jnp.stack(sg), jnp.stack(sr), jnp.stack(ssh),
            jnp.stack(sw[0]), jnp.stack(sw[1]), jnp.stack(sw[2]),
            jnp.stack(sv))
```

```python
import functools

import jax
import jax.numpy as jnp
import numpy as np
from jax import lax
from jax.experimental import pallas as pl
from jax.experimental.pallas import tpu as pltpu

f32 = jnp.float32
bf16 = jnp.bfloat16
HIGHEST = lax.Precision.HIGHEST

D_MODEL = 1024
HEAD_DIM = 64
N_HEADS = 4
GROUP_WIDTH = N_HEADS * HEAD_DIM
D_FF = 2816
EPS = 1e-6
GLA_DK = 32
GLA_RANK = 16
GLA_TAU = 16.0
GLA_CHUNK = 64
GMLP_CHUNK = 128
LN_EPS = 1e-5
RWKV_W_RANK = 64
RWKV_A_RANK = 64
RWKV_G_RANK = 128
RWKV_LNX_EPS = 64e-5
DIL_PAIRS = ((128, 1), (512, 4), (2048, 16))
N_DIL_HEADS = len(DIL_PAIRS) * N_HEADS
ALIBI_MAX_EXP = 8.0
GLA_COLS = 2 * N_HEADS * GLA_DK + GROUP_WIDTH + GLA_RANK + GROUP_WIDTH
GMLP_COLS = 2 * GROUP_WIDTH
RWKV_COLS = 3 * GROUP_WIDTH + RWKV_W_RANK + RWKV_A_RANK + RWKV_G_RANK
DIL_COLS = len(DIL_PAIRS) * 3 * GROUP_WIDTH

LANES = 128
SUBLANES = 8
VMEM_LIMIT_BYTES = 56 * 1024 * 1024

GLA_COLS_PAD = 896
DIL_SLABS = DIL_COLS // LANES // len(DIL_PAIRS)
FFN_CHUNK = 256
ROW_TILE = 512
RWKV_BATCH = 8
RWKV_TBLOCK = 64


def _alibi_slope(head):
    return float(2.0 ** (-ALIBI_MAX_EXP * (head + 1.0) / N_DIL_HEADS))


def _iota(shape, dim):
    return lax.broadcasted_iota(jnp.int32, shape, dim)


def _dot(a, b, precision=None):
    return jnp.dot(a, b, preferred_element_type=f32, precision=precision)


def _dot_tr(a, b, precision=None):
    return lax.dot_general(a, b, (((1,), (1,)), ((), ())), preferred_element_type=f32, precision=precision)


def _dot_tl(a, b, precision=None):
    return lax.dot_general(a, b, (((0,), (0,)), ((), ())), preferred_element_type=f32, precision=precision)


def _rms_rows(x, g):
    ms = jnp.mean(x * x, axis=-1, keepdims=True)
    return x * lax.rsqrt(ms + EPS) * g


def _softplus(x):
    return jnp.maximum(x, 0.0) + jnp.log1p(jnp.exp(-jnp.abs(x)))


def _params(*semantics):
    return pltpu.CompilerParams(dimension_semantics=semantics, vmem_limit_bytes=VMEM_LIMIT_BYTES)


def _resident(shape):
    zeros = (0,) * len(shape)
    return pl.BlockSpec(shape, lambda *_: zeros, pipeline_mode=pl.Buffered(1))


def _ffn_body(x_ref, gpre_ref, gpost_ref, wg_ref, wu_ref, wd_ref, o_ref):
    x = x_ref[...]
    xn = _rms_rows(x, gpre_ref[...]).astype(bf16)
    acc = None
    for c in range(D_FF // FFN_CHUNK):
        lo, hi = c * FFN_CHUNK, (c + 1) * FFN_CHUNK
        g = _dot(xn, wg_ref[:, lo:hi])
        u = _dot(xn, wu_ref[:, lo:hi])
        act = (g * jax.nn.sigmoid(g) * u).astype(bf16)
        part = _dot(act, wd_ref[lo:hi, :])
        acc = part if acc is None else acc + part
    o_ref[...] = x + 0.5 * _rms_rows(acc, gpost_ref[...])


def _ffn(x, g_pre, g_post, wg, wu, wd):
    n = x.shape[0]
    tm = min(ROW_TILE, n)
    row = pl.BlockSpec((tm, D_MODEL), lambda i: (i, 0))
    return pl.pallas_call(
        _ffn_body,
        out_shape=jax.ShapeDtypeStruct((n, D_MODEL), f32),
        grid=(n // tm,),
        in_specs=[row, _resident((1, D_MODEL)), _resident((1, D_MODEL)),
                  _resident((D_MODEL, D_FF)), _resident((D_MODEL, D_FF)), _resident((D_FF, D_MODEL))],
        out_specs=row,
        compiler_params=_params("parallel"),
        name="ffn",
    )(x, g_pre, g_post, wg, wu, wd)


def _proj_body(h_ref, g_ref, wgla_ref, wgmlp_ref, wrwkv_ref, wdil_ref, ogla_ref, ogmlp_ref, orwkv_ref, odil_ref):
    xn = _rms_rows(h_ref[...], g_ref[...]).astype(bf16)
    ogla_ref[...] = _dot(xn, wgla_ref[...])
    ogmlp_ref[...] = _dot(xn, wgmlp_ref[...])
    orwkv_ref[...] = _dot(xn, wrwkv_ref[...])
    res = _dot(xn, wdil_ref[...])
    for s in range(DIL_COLS // LANES):
        odil_ref[s // DIL_SLABS, s % DIL_SLABS] = res[:, s * LANES:(s + 1) * LANES]


def _proj(h, g, w_gla, w_gmlp, w_rwkv, w_dil):
    n = h.shape[0]
    tm = min(ROW_TILE, n)
    n_grp = len(DIL_PAIRS)
    return pl.pallas_call(
        _proj_body,
        out_shape=(jax.ShapeDtypeStruct((n, GLA_COLS_PAD), f32),
                   jax.ShapeDtypeStruct((n, GMLP_COLS), f32),
                   jax.ShapeDtypeStruct((n, RWKV_COLS), f32),
                   jax.ShapeDtypeStruct((n_grp, DIL_SLABS, n, LANES), f32)),
        grid=(n // tm,),
        in_specs=[pl.BlockSpec((tm, D_MODEL), lambda i: (i, 0)), _resident((1, D_MODEL)),
                  _resident((D_MODEL, GLA_COLS_PAD)), _resident((D_MODEL, GMLP_COLS)),
                  _resident((D_MODEL, RWKV_COLS)), _resident((D_MODEL, DIL_COLS))],
        out_specs=(pl.BlockSpec((tm, GLA_COLS_PAD), lambda i: (i, 0)),
                   pl.BlockSpec((tm, GMLP_COLS), lambda i: (i, 0)),
                   pl.BlockSpec((tm, RWKV_COLS), lambda i: (i, 0)),
                   pl.BlockSpec((n_grp, DIL_SLABS, tm, LANES), lambda i: (0, 0, i, 0))),
        compiler_params=_params("parallel"),
        name="proj",
    )(h, g, w_gla, w_gmlp, w_rwkv, w_dil)


def _outproj_body(h_ref, ogla_ref, ogmlp_ref, orwkv_ref, odil_ref, w_ref, g_ref, o_ref):
    gw = GROUP_WIDTH
    mix = _dot(ogla_ref[...].astype(bf16), w_ref[0:gw, :])
    mix += _dot(ogmlp_ref[...].astype(bf16), w_ref[gw:2 * gw, :])
    mix += _dot(orwkv_ref[...].astype(bf16), w_ref[2 * gw:3 * gw, :])
    for p in range(gw // LANES):
        mix += _dot(odil_ref[p].astype(bf16), w_ref[3 * gw + p * LANES:3 * gw + (p + 1) * LANES, :])
    o_ref[...] = h_ref[...] + _rms_rows(mix, g_ref[...])


def _outproj(h, o_gla, o_gmlp, o_rwkv, o_dil, w_out, g):
    n = h.shape[0]
    tm = min(ROW_TILE, n)
    row = pl.BlockSpec((tm, D_MODEL), lambda i: (i, 0))
    grp = pl.BlockSpec((tm, GROUP_WIDTH), lambda i: (i, 0))
    return pl.pallas_call(
        _outproj_body,
        out_shape=jax.ShapeDtypeStruct((n, D_MODEL), f32),
        grid=(n // tm,),
        in_specs=[row, grp, grp, grp, pl.BlockSpec((GROUP_WIDTH // LANES, tm, LANES), lambda i: (0, i, 0)),
                  _resident((D_MODEL, D_MODEL)), _resident((1, D_MODEL))],
        out_specs=row,
        compiler_params=_params("parallel"),
        name="outproj",
    )(h, o_gla, o_gmlp, o_rwkv, o_dil, w_out, g)


def _gla_body(c_ref, h0_ref, wa_ref, ba_ref, ng_ref, o_ref, sfin_ref, s_scr, *, n_chunks, valid_rows):
    C = GLA_CHUNK
    hk, hv = N_HEADS * GLA_DK, GROUP_WIDTH
    tb = pl.program_id(1)

    @pl.when(tb == 0)
    def _init():
        s_scr[...] = jnp.zeros((hk, hv), f32)
        for h in range(N_HEADS):
            s_scr[GLA_DK * h:GLA_DK * (h + 1), HEAD_DIM * h:HEAD_DIM * (h + 1)] = h0_ref[0, h]

    rr, cc = _iota((C, C), 0), _iota((C, C), 1)
    causal = rr >= cc
    tril = causal.astype(f32)
    k_head = _iota((1, hk), 1) >> 5
    v_head = _iota((1, hv), 1) >> 6
    diag_blocks = (_iota((hk, hv), 0) >> 5) == (_iota((hk, hv), 1) >> 6)
    seg_mean = ((_iota((hv, hv), 0) >> 6) == (_iota((hv, hv), 1) >> 6)).astype(f32) * (1.0 / HEAD_DIM)
    ones_cv = jnp.ones((C, hv), f32)

    for ci in range(n_chunks):
        x = c_ref[0, ci * C:(ci + 1) * C, :]
        q = x[:, 0:hk] * (GLA_DK ** -0.5)
        k = x[:, hk:2 * hk]
        v = x[:, 2 * hk:2 * hk + hv]
        g = x[:, 2 * hk + hv:2 * hk + 2 * hv]
        a_lr = x[:, 2 * hk + 2 * hv:]
        z = _dot(a_lr, wa_ref[...], HIGHEST) + ba_ref[...]
        log_a = -_softplus(-z) * (1.0 / GLA_TAU)
        if valid_rows < C:
            log_a = jnp.where(_iota((C, hk), 0) < valid_rows, log_a, 0.0)
        b = _dot(tril, log_a, HIGHEST)
        b_last = b[C - 1:C, :]
        qe = q * jnp.exp(b)
        ke = k * jnp.exp(-b)
        kd = k * jnp.exp(b_last - b)
        state = s_scr[...]
        o = _dot(qe.astype(bf16), state.astype(bf16))
        keb = ke.astype(bf16)
        for h in range(N_HEADS):
            att = _dot_tr(jnp.where(k_head == h, qe, 0.0).astype(bf16), keb)
            att = jnp.where(causal, att, 0.0)
            o += _dot(att.astype(bf16), jnp.where(v_head == h, v, 0.0).astype(bf16))
        incr = jnp.where(diag_blocks, _dot_tl(kd.astype(bf16), v.astype(bf16)), 0.0)
        chunk_log_decay = _dot_tl(log_a, ones_cv, HIGHEST)
        s_scr[...] = jnp.exp(chunk_log_decay) * state + incr
        ms = _dot(o * o, seg_mean, HIGHEST)
        o_ref[0, ci * C:(ci + 1) * C, :] = o * lax.rsqrt(ms + EPS) * ng_ref[...] * (g * jax.nn.sigmoid(g))

    @pl.when(tb == pl.num_programs(1) - 1)
    def _fin():
        for h in range(N_HEADS):
            sfin_ref[0, h] = s_scr[GLA_DK * h:GLA_DK * (h + 1), HEAD_DIM * h:HEAD_DIM * (h + 1)]


def _gla(c, h0, wa_pad, b_alpha, norm_g, valid_rows):
    bsz, t, _ = c.shape
    tb = min(t, 4 * GLA_CHUNK)
    return pl.pallas_call(
        functools.partial(_gla_body, n_chunks=tb // GLA_CHUNK, valid_rows=valid_rows),
        out_shape=(jax.ShapeDtypeStruct((bsz, t, GROUP_WIDTH), f32),
                   jax.ShapeDtypeStruct((bsz, N_HEADS, GLA_DK, HEAD_DIM), f32)),
        grid=(bsz, t // tb),
        in_specs=[pl.BlockSpec((1, tb, GLA_COLS_PAD), lambda b, i: (b, i, 0)),
                  pl.BlockSpec((1, N_HEADS, GLA_DK, HEAD_DIM), lambda b, i: (b, 0, 0, 0)),
                  pl.BlockSpec((LANES, N_HEADS * GLA_DK), lambda b, i: (0, 0)),
                  pl.BlockSpec((1, N_HEADS * GLA_DK), lambda b, i: (0, 0)),
                  pl.BlockSpec((1, GROUP_WIDTH), lambda b, i: (0, 0))],
        out_specs=(pl.BlockSpec((1, tb, GROUP_WIDTH), lambda b, i: (b, i, 0)),
                   pl.BlockSpec((1, N_HEADS, GLA_DK, HEAD_DIM), lambda b, i: (b, 0, 0, 0))),
        scratch_shapes=[pltpu.VMEM((N_HEADS * GLA_DK, GROUP_WIDTH), f32)],
        compiler_params=_params("parallel", "arbitrary"),
        name="gla",
    )(c, h0, wa_pad, b_alpha, norm_g)


def _gmlp_body(c_ref, lng_ref, lnb_ref, ws_ref, bias_ref, o_ref, *v_refs):
    gw = GROUP_WIDTH
    x = c_ref[...]
    ge = 0.5 * x * (1.0 + lax.erf(x * float(np.sqrt(0.5))))
    u, v = ge[:, :gw], ge[:, gw:]
    mean = jnp.mean(v, axis=-1, keepdims=True)
    d = v - mean
    var = jnp.mean(d * d, axis=-1, keepdims=True)
    vn = d * lax.rsqrt(var + LN_EPS) * lng_ref[...] + lnb_ref[...]
    n = GMLP_CHUNK
    causal = _iota((n, n), 0) >= _iota((n, n), 1)
    v_head = _iota((1, gw), 1) >> 6
    vb = vn.astype(bf16)
    s = jnp.zeros((n, gw), f32)
    for grp in range(N_HEADS):
        w = jnp.where(causal, ws_ref[grp], 0.0).astype(bf16)
        s = jnp.where(v_head == grp, _dot(w, vb), s)
    o_ref[...] = u * (s + bias_ref[...])
    if v_refs:
        v_refs[0][...] = vn


def _gmlp(c, ln_g, ln_b, ws, bias_rows, emit_v):
    n = c.shape[0]
    blk = pl.BlockSpec((GMLP_CHUNK, GROUP_WIDTH), lambda i: (i, 0))
    out_shape = [jax.ShapeDtypeStruct((n, GROUP_WIDTH), f32)]
    out_specs = [blk]
    if emit_v:
        out_shape.append(jax.ShapeDtypeStruct((n, GROUP_WIDTH), f32))
        out_specs.append(blk)
    return pl.pallas_call(
        _gmlp_body,
        out_shape=tuple(out_shape),
        grid=(n // GMLP_CHUNK,),
        in_specs=[pl.BlockSpec((GMLP_CHUNK, GMLP_COLS), lambda i: (i, 0)),
                  pl.BlockSpec((1, GROUP_WIDTH), lambda i: (0, 0)),
                  pl.BlockSpec((1, GROUP_WIDTH), lambda i: (0, 0)),
                  pl.BlockSpec((N_HEADS, GMLP_CHUNK, GMLP_CHUNK), lambda i: (0, 0, 0)),
                  pl.BlockSpec((GMLP_CHUNK, GROUP_WIDTH), lambda i: (0, 0))],
        out_specs=tuple(out_specs),
        compiler_params=_params("parallel"),
        name="gmlp",
    )(c, ln_g, ln_b, ws, bias_rows)


def _split_dot(x, seg_b):
    hi = x.astype(bf16)
    lo = (x - hi.astype(f32)).astype(bf16)
    return _dot(hi, seg_b) + _dot(lo, seg_b)


def _rwkv_body(c_ref, sp_ref, s0_ref, mu_ref, w0_ref, w2_ref, a0_ref, a2_ref, g2_ref, kk_ref, ka_ref, rk_ref,
               lg_ref, lb_ref, seg_ref, o_ref, sfin_ref,
               s_scr, carry_scr, r_scr, w_scr, k_scr, v_scr, a_scr, b_scr, y_scr, g_scr, bonus_scr,
               *, nb, n_steps):
    gw = GROUP_WIDTH
    tbr = c_ref.shape[1]
    tb = pl.program_id(1)
    seg_b = seg_ref[...]
    seg_f = seg_b.astype(f32)
    seg_mean = seg_f * (1.0 / HEAD_DIM)

    @pl.when(tb == 0)
    def _init():
        for b in range(nb):
            carry_scr[b] = sp_ref[b]
            for h in range(N_HEADS):
                s_scr[b, :, HEAD_DIM * h:HEAD_DIM * (h + 1)] = s0_ref[b, h]

    first_row = _iota((tbr, RWKV_COLS), 0) == 0
    for b in range(nb):
        c = c_ref[b]
        prev = jnp.where(first_row, carry_scr[b], pltpu.roll(c, 1, 0))
        carry_scr[b] = c[tbr - 1:tbr, :]
        xm = c + (prev - c) * mu_ref[...]
        r = xm[:, 0:gw]
        k = xm[:, gw:2 * gw]
        v = xm[:, 2 * gw:3 * gw]
        wa_lr = xm[:, 3 * gw:3 * gw + LANES]
        g_lr = xm[:, 3 * gw + LANES:]
        w = -_softplus(-(w0_ref[...] + _dot(jnp.tanh(wa_lr), w2_ref[...], HIGHEST))) - 0.5
        a = jax.nn.sigmoid(a0_ref[...] + _dot(wa_lr, a2_ref[...], HIGHEST))
        g = _dot(jax.nn.sigmoid(g_lr), g2_ref[...], HIGHEST)
        kk = k * kk_ref[...]
        kk = kk / jnp.maximum(jnp.sqrt(_dot(kk * kk, seg_f, HIGHEST)), 1e-12)
        k2 = k * (1.0 + (a - 1.0) * ka_ref[...])
        r_scr[b] = r
        w_scr[b] = jnp.exp(-jnp.exp(w))
        k_scr[b] = k2
        v_scr[b] = v
        a_scr[b] = -kk
        b_scr[b] = kk * a
        g_scr[b] = g
        bonus_scr[b] = _dot(r * k2 * rk_ref[...], seg_f, HIGHEST) * v
        y_scr[b] = jnp.zeros((tbr, gw), f32)

    eye = (_iota((HEAD_DIM, gw), 1) & (HEAD_DIM - 1)) == _iota((HEAD_DIM, gw), 0)

    def step(t, carry):
        for b in range(nb):
            row = pl.ds(t, 1)
            state = s_scr[b]
            sa = _split_dot(state * a_scr[b, row, :], seg_b)
            v_col = _split_dot(jnp.where(eye, v_scr[b, row, :], 0.0), seg_b)
            state = state * w_scr[b, row, :] + sa * b_scr[b, row, :] + v_col * k_scr[b, row, :]
            s_scr[b] = state
            y = _dot((state * r_scr[b, row, :]).astype(bf16), seg_b)
            y_scr[b, row, :] = jnp.sum(jnp.where(eye, y, 0.0), axis=0, keepdims=True)
        return carry

    lax.fori_loop(0, n_steps, step, 0)

    for b in range(nb):
        y = y_scr[b]
        d = y - _dot(y, seg_mean, HIGHEST)
        var = _dot(d * d, seg_mean, HIGHEST)
        yn = d * lax.rsqrt(var + RWKV_LNX_EPS) * lg_ref[...] + lb_ref[...]
        o_ref[b] = (yn + bonus_scr[b]) * g_scr[b]

    @pl.when(tb == pl.num_programs(1) - 1)
    def _fin():
        for b in range(nb):
            for h in range(N_HEADS):
                sfin_ref[b, h] = s_scr[b, :, HEAD_DIM * h:HEAD_DIM * (h + 1)]


def _rwkv(c, shift_prev, s0, p, seg_b, n_valid):
    bsz, t, _ = c.shape
    nb = RWKV_BATCH
    tbr = min(t, RWKV_TBLOCK)
    n_steps = min(n_valid, tbr)
    vec = lambda width: pl.BlockSpec((1, width), lambda b, i: (0, 0))
    mat = lambda rows: pl.BlockSpec((rows, GROUP_WIDTH), lambda b, i: (0, 0))
    rows_scr = pltpu.VMEM((nb, tbr, GROUP_WIDTH), f32)
    return pl.pallas_call(
        functools.partial(_rwkv_body, nb=nb, n_steps=n_steps),
        out_shape=(jax.ShapeDtypeStruct((bsz, t, GROUP_WIDTH), f32),
                   jax.ShapeDtypeStruct((bsz, N_HEADS, HEAD_DIM, HEAD_DIM), f32)),
        grid=(bsz // nb, t // tbr),
        in_specs=[pl.BlockSpec((nb, tbr, RWKV_COLS), lambda b, i: (b, i, 0)),
                  pl.BlockSpec((nb, 1, RWKV_COLS), lambda b, i: (b, 0, 0)),
                  pl.BlockSpec((nb, N_HEADS, HEAD_DIM, HEAD_DIM), lambda b, i: (b, 0, 0, 0)),
                  vec(RWKV_COLS), vec(GROUP_WIDTH), mat(LANES), vec(GROUP_WIDTH), mat(LANES), mat(RWKV_G_RANK),
                  vec(GROUP_WIDTH), vec(GROUP_WIDTH), vec(GROUP_WIDTH), vec(GROUP_WIDTH), vec(GROUP_WIDTH),
                  mat(GROUP_WIDTH)],
        out_specs=(pl.BlockSpec((nb, tbr, GROUP_WIDTH), lambda b, i: (b, i, 0)),
                   pl.BlockSpec((nb, N_HEADS, HEAD_DIM, HEAD_DIM), lambda b, i: (b, 0, 0, 0))),
        scratch_shapes=[pltpu.VMEM((nb, HEAD_DIM, GROUP_WIDTH), f32), pltpu.VMEM((nb, 1, RWKV_COLS), f32)]
        + [rows_scr] * 9,
        compiler_params=_params("parallel", "arbitrary"),
        name="rwkv",
    )(c, shift_prev, s0, p["mu"], p["w0"], p["w2"], p["a0"], p["a2"], p["g2"], p["kk"], p["ka"], p["rk"],
      p["lnx_g"], p["lnx_b"], seg_b)


def _softmax_tiles(scores, values):
    mx = functools.reduce(jnp.maximum, [jnp.max(s, axis=-1, keepdims=True) for s in scores])
    out, den = None, None
    for s, v in zip(scores, values):
        p = jnp.exp(s - mx)
        ps = jnp.sum(p, axis=-1, keepdims=True)
        pv = _dot(p.astype(bf16), v)
        out = pv if out is None else out + pv
        den = ps if den is None else den + ps
    return out, mx, den


def _merge_rows(acc, mrun, den, o_new, m_new, d_new):
    m = jnp.maximum(mrun, m_new)
    a_old, a_new = jnp.exp(mrun - m), jnp.exp(m_new - m)
    return acc * a_old + o_new * a_new, m, den * a_old + d_new * a_new


def _dil_prompt_body(x_ref, o_ref, acc_scr, m_scr, d_scr, *, t):
    grp = pl.program_id(1)
    blk = LANES
    rr, cc = _iota((blk, blk), 0), _iota((blk, blk), 1)
    first_head = _iota((1, LANES), 1) < HEAD_DIM
    neg_inf = float("-inf")

    def group(gi, dilation):
        n_sub = t // dilation // blk
        dist_cur = (rr - cc).astype(f32) * float(dilation)
        dist_prev = (rr + blk - cc).astype(f32) * float(dilation)

        def combo(idx, carry):
            res, sub = idx // n_sub, idx % n_sub
            start = res + dilation * blk * sub
            pstart = jnp.maximum(start - dilation * blk, 0)
            rows = pl.ds(start, blk, stride=dilation) if dilation > 1 else pl.ds(start, blk)
            prows = pl.ds(pstart, blk, stride=dilation) if dilation > 1 else pl.ds(pstart, blk)
            for pair in range(GROUP_WIDTH // LANES):
                q = x_ref[0, pair, rows, :] * (HEAD_DIM ** -0.5)
                kc = x_ref[0, 2 + pair, rows, :].astype(bf16)
                vc = x_ref[0, 4 + pair, rows, :].astype(bf16)
                if n_sub > 1:
                    kp = x_ref[0, 2 + pair, prows, :].astype(bf16)
                    vp = x_ref[0, 4 + pair, prows, :].astype(bf16)
                o_pair = m_pair = d_pair = None
                for hh in range(2):
                    slope = _alibi_slope(gi * N_HEADS + 2 * pair + hh)
                    lanes = first_head if hh == 0 else jnp.logical_not(first_head)
                    qm = jnp.where(lanes, q, 0.0).astype(bf16)
                    scores = [jnp.where(rr >= cc, _dot_tr(qm, kc) - slope * dist_cur, neg_inf)]
                    values = [vc]
                    if n_sub > 1:
                        s_prev = jnp.where(cc >= rr, _dot_tr(qm, kp) - slope * dist_prev, neg_inf)
                        scores.append(s_prev + jnp.where(sub > 0, 0.0, neg_inf))
                        values.append(vp)
                    o_h, m_h, d_h = _softmax_tiles(scores, values)
                    m_h = jnp.broadcast_to(m_h, (blk, LANES))
                    d_h = jnp.broadcast_to(d_h, (blk, LANES))
                    if hh == 0:
                        o_pair, m_pair, d_pair = o_h, m_h, d_h
                    else:
                        o_pair = jnp.where(lanes, o_h, o_pair)
                        m_pair = jnp.where(lanes, m_h, m_pair)
                        d_pair = jnp.where(lanes, d_h, d_pair)
                if gi > 0:
                    o_pair, m_pair, d_pair = _merge_rows(acc_scr[pair, rows, :], m_scr[pair, rows, :],
                                                         d_scr[pair, rows, :], o_pair, m_pair, d_pair)
                acc_scr[pair, rows, :] = o_pair
                m_scr[pair, rows, :] = m_pair
                d_scr[pair, rows, :] = d_pair
            return carry

        lax.fori_loop(0, t // blk, combo, 0)

    for gi, (_, dilation) in enumerate(DIL_PAIRS):
        pl.when(grp == gi)(functools.partial(group, gi, dilation))

    @pl.when(grp == len(DIL_PAIRS) - 1)
    def _fin():
        o_ref[...] = acc_scr[...] / d_scr[...]


def _dil_prompt(x, bsz, t):
    n_pairs = GROUP_WIDTH // LANES
    scr = pltpu.VMEM((n_pairs, t, LANES), f32)
    return pl.pallas_call(
        functools.partial(_dil_prompt_body, t=t),
        out_shape=jax.ShapeDtypeStruct((n_pairs, bsz * t, LANES), f32),
        grid=(bsz, len(DIL_PAIRS)),
        in_specs=[pl.BlockSpec((1, DIL_SLABS, t, LANES), lambda b, g: (g, 0, b, 0))],
        out_specs=pl.BlockSpec((n_pairs, t, LANES), lambda b, g: (0, b, 0)),
        scratch_shapes=[scr, scr, scr],
        compiler_params=_params("parallel", "arbitrary"),
        name="dil_prompt",
    )(x)


def _dil_sample_body(x_ref, c0_ref, c1_ref, c2_ref, o_ref, *, t_new):
    caches = (c0_ref, c1_ref, c2_ref)
    nq = SUBLANES
    per = nq // t_new
    first_head = _iota((1, LANES), 1) < HEAD_DIM
    neg_inf = float("-inf")
    q_idx = _iota((nq, 1), 0) & (t_new - 1)
    q_bat = _iota((nq, 1), 0) >> 2
    rn, cn = _iota((nq, nq), 0), _iota((nq, nq), 1)
    dn = (rn & (t_new - 1)) - (cn & (t_new - 1))
    same = (rn >> 2) == (cn >> 2)
    for pair in range(GROUP_WIDTH // LANES):
        acc = mrun = den = None
        for gi, (window, dilation) in enumerate(DIL_PAIRS):
            q = x_ref[gi, pair] * (HEAD_DIM ** -0.5)
            kn = x_ref[gi, 2 + pair].astype(bf16)
            vn = x_ref[gi, 4 + pair].astype(bf16)
            dist_c = window + q_idx - _iota((nq, window), 1)
            ok_c = jnp.logical_and((dist_c & (dilation - 1)) == 0, dist_c <= window)
            ok_n = jnp.logical_and(jnp.logical_and(same, dn >= 0), (dn & (dilation - 1)) == 0)
            o_pair = m_pair = d_pair = None
            for hh in range(2):
                slope = _alibi_slope(gi * N_HEADS + 2 * pair + hh)
                lanes = first_head if hh == 0 else jnp.logical_not(first_head)
                qm = jnp.where(lanes, q, 0.0).astype(bf16)
                s_new = jnp.where(ok_n, _dot_tr(qm, kn) - slope * dn.astype(f32), neg_inf)
                o_h = m_h = d_h = None
                for s in range(per):
                    kc = caches[gi][s, :, pair * LANES:(pair + 1) * LANES].astype(bf16)
                    vc = caches[gi][s, :, GROUP_WIDTH + pair * LANES:GROUP_WIDTH + (pair + 1) * LANES].astype(bf16)
                    s_c = jnp.where(ok_c, _dot_tr(qm, kc) - slope * dist_c.astype(f32), neg_inf)
                    o_s, m_s, d_s = _softmax_tiles([s_c, s_new], [vc, vn])
                    if s == 0:
                        o_h, m_h, d_h = o_s, m_s, d_s
                    else:
                        mine = q_bat == s
                        o_h = jnp.where(mine, o_s, o_h)
                        m_h = jnp.where(mine, m_s, m_h)
                        d_h = jnp.where(mine, d_s, d_h)
                m_h = jnp.broadcast_to(m_h, (nq, LANES))
                d_h = jnp.broadcast_to(d_h, (nq, LANES))
                if hh == 0:
                    o_pair, m_pair, d_pair = o_h, m_h, d_h
                else:
                    o_pair = jnp.where(lanes, o_h, o_pair)
                    m_pair = jnp.where(lanes, m_h, m_pair)
                    d_pair = jnp.where(lanes, d_h, d_pair)
            if gi == 0:
                acc, mrun, den = o_pair, m_pair, d_pair
            else:
                acc, mrun, den = _merge_rows(acc, mrun, den, o_pair, m_pair, d_pair)
        o_ref[pair] = acc / den


def _dil_sample(x, caches, bsz, t_new):
    n_pairs = GROUP_WIDTH // LANES
    per = SUBLANES // t_new
    return pl.pallas_call(
        functools.partial(_dil_sample_body, t_new=t_new),
        out_shape=jax.ShapeDtypeStruct((n_pairs, bsz * t_new, LANES), f32),
        grid=(bsz // per,),
        in_specs=[pl.BlockSpec((len(DIL_PAIRS), DIL_SLABS, SUBLANES, LANES), lambda i: (0, 0, i, 0))]
        + [pl.BlockSpec((per, w, 2 * GROUP_WIDTH), lambda i: (i, 0, 0)) for w, _ in DIL_PAIRS],
        out_specs=pl.BlockSpec((n_pairs, SUBLANES, LANES), lambda i: (0, i, 0)),
        compiler_params=_params("parallel"),
        name="dil_sample",
    )(x, *caches)


def _row(v):
    return v.reshape(1, -1)


def _prep_layer(l, norm_gains, w_ff_gate, w_ff_up, w_ff_down, w_in, w_out, gla_w_alpha2, gla_b_alpha, gla_norm,
                gmlp_ln_g, gmlp_ln_b, gmlp_ws, gmlp_bs, rwkv_mu, rwkv_w0, rwkv_w2, rwkv_a0, rwkv_a2, rwkv_g2,
                rwkv_kk, rwkv_ka, rwkv_rk, rwkv_lnx_g, rwkv_lnx_b):
    win = w_in[l]
    o1, o2, o3 = GLA_COLS, GLA_COLS + GMLP_COLS, GLA_COLS + GMLP_COLS + RWKV_COLS
    hk = N_HEADS * GLA_DK
    a_lo, a_hi = 2 * hk + GROUP_WIDTH, 2 * hk + GROUP_WIDTH + GLA_RANK
    w_gla = win[:, :o1]
    w_gla = jnp.concatenate([w_gla[:, :a_lo], w_gla[:, a_hi:], w_gla[:, a_lo:a_hi],
                             jnp.zeros((D_MODEL, GLA_COLS_PAD - GLA_COLS), f32)], axis=1)
    zeros_r = jnp.zeros((RWKV_W_RANK, GROUP_WIDTH), f32)
    return dict(
        norms=[_row(norm_gains[l, i]) for i in range(norm_gains.shape[1])],
        ff_gate=[w_ff_gate[l, i].astype(bf16) for i in range(2)],
        ff_up=[w_ff_up[l, i].astype(bf16) for i in range(2)],
        ff_down=[w_ff_down[l, i].astype(bf16) for i in range(2)],
        w_gla=w_gla.astype(bf16), w_gmlp=win[:, o1:o2].astype(bf16), w_rwkv=win[:, o2:o3].astype(bf16),
        w_dil=win[:, o3:].astype(bf16), w_out=w_out[l].astype(bf16),
        gla_wa=jnp.concatenate([gla_w_alpha2[l], jnp.zeros((LANES - GLA_RANK, hk), f32)], axis=0),
        gla_ba=_row(gla_b_alpha[l]), gla_norm=_row(gla_norm[l]),
        gmlp_ln_g=_row(gmlp_ln_g[l]), gmlp_ln_b=_row(gmlp_ln_b[l]), gmlp_ws=gmlp_ws[l], gmlp_bs=gmlp_bs[l],
        rwkv=dict(mu=_row(rwkv_mu[l]), w0=_row(rwkv_w0[l]),
                  w2=jnp.concatenate([rwkv_w2[l], zeros_r], axis=0), a0=_row(rwkv_a0[l]),
                  a2=jnp.concatenate([zeros_r, rwkv_a2[l]], axis=0), g2=rwkv_g2[l],
                  kk=_row(rwkv_kk[l]), ka=_row(rwkv_ka[l]), rk=_row(rwkv_rk[l]),
                  lnx_g=_row(rwkv_lnx_g[l]), lnx_b=_row(rwkv_lnx_b[l])),
    )


def _kv_rows(c_dil, bsz, t, rows):
    out = []
    for gi in range(len(DIL_PAIRS)):
        kv = c_dil[gi, 2:].reshape(2, GROUP_WIDTH // LANES, bsz, t, LANES)[:, :, :, t - rows[gi]:]
        kv = jnp.transpose(kv, (2, 3, 0, 1, 4))
        out.append(kv.reshape(bsz, rows[gi], 2, N_HEADS, HEAD_DIM))
    return out


def _trunk_layer(x, p, seg_b, bsz, t, gla_h0, rwkv_s0, shift0, caches):
    ng = p["norms"]
    n = bsz * t
    h = _ffn(x, ng[0], ng[1], p["ff_gate"][0], p["ff_up"][0], p["ff_down"][0])
    c_gla, c_gmlp, c_rwkv, c_dil = _proj(h, ng[2], p["w_gla"], p["w_gmlp"], p["w_rwkv"], p["w_dil"])
    is_prompt = caches is None

    t_gla = max(t, GLA_CHUNK)
    cg = c_gla.reshape(bsz, t, GLA_COLS_PAD)
    if t_gla != t:
        cg = jnp.pad(cg, ((0, 0), (0, t_gla - t), (0, 0)))
    o_gla, gla_s = _gla(cg, gla_h0, p["gla_wa"], p["gla_ba"], p["gla_norm"], min(t, GLA_CHUNK))
    o_gla = o_gla[:, :t].reshape(n, GROUP_WIDTH)

    if t >= GMLP_CHUNK:
        ws = p["gmlp_ws"]
        bias_rows = jnp.repeat(p["gmlp_bs"].T, HEAD_DIM, axis=1)
    else:
        per = GMLP_CHUNK // t
        eye = jnp.eye(per, dtype=f32)
        ws = jnp.einsum("ab,gij->gaibj", eye, p["gmlp_ws"][:, :t, :t]).reshape(N_HEADS, GMLP_CHUNK, GMLP_CHUNK)
        bias_rows = jnp.tile(jnp.repeat(p["gmlp_bs"][:, :t].T, HEAD_DIM, axis=1), (per, 1))
    gm = _gmlp(c_gmlp, p["gmlp_ln_g"], p["gmlp_ln_b"], ws, bias_rows, emit_v=not is_prompt)
    o_gmlp = gm[0]
    gmlp_v = None if is_prompt else gm[1]

    t_rw = max(t, SUBLANES)
    cr = c_rwkv.reshape(bsz, t, RWKV_COLS)
    shift = cr[:, -1]
    if t_rw != t:
        cr = jnp.pad(cr, ((0, 0), (0, t_rw - t), (0, 0)))
    o_rwkv, rwkv_s = _rwkv(cr, shift0.reshape(bsz, 1, RWKV_COLS), rwkv_s0, p["rwkv"], seg_b, t)
    o_rwkv = o_rwkv[:, :t].reshape(n, GROUP_WIDTH)

    if is_prompt:
        o_dil = _dil_prompt(c_dil, bsz, t)
        kv_new = _kv_rows(c_dil, bsz, t, [min(w, t) for w, _ in DIL_PAIRS])
    else:
        o_dil = _dil_sample(c_dil, caches, bsz, t)
        kv_new = _kv_rows(c_dil, bsz, t, [min(w, t) for w, _ in DIL_PAIRS])

    h = _outproj(h, o_gla, o_gmlp, o_rwkv, o_dil, p["w_out"], ng[3])
    y = _ffn(h, ng[4], ng[5], p["ff_gate"][1], p["ff_up"][1], p["ff_down"][1])
    return y, gla_s, rwkv_s, shift, gmlp_v, kv_new


def kernel(x_prompt, x_sample, cache_win128, cache_win512, cache_win2048, state_gla, state_rwkv, state_shift,
           norm_gains, w_ff_gate, w_ff_up, w_ff_down, w_in, w_out, gla_w_alpha2, gla_b_alpha, gla_norm,
           gmlp_ln_g, gmlp_ln_b, gmlp_ws, gmlp_bs, rwkv_mu, rwkv_w0, rwkv_w2, rwkv_a0, rwkv_a2, rwkv_g2,
           rwkv_kk, rwkv_ka, rwkv_rk, rwkv_lnx_g, rwkv_lnx_b):
    bp, tp, _ = x_prompt.shape
    bs, ts, _ = x_sample.shape
    depth = norm_gains.shape[0]
    assert tp % GMLP_CHUNK == 0 and tp % (DIL_PAIRS[-1][1] * LANES) == 0 and bp % RWKV_BATCH == 0
    assert ts == 4 and bs % RWKV_BATCH == 0 and (bs * ts) % GMLP_CHUNK == 0
    caches = (cache_win128, cache_win512, cache_win2048)
    lanes = np.arange(GROUP_WIDTH)
    seg_b = jnp.asarray((lanes[:, None] // HEAD_DIM) == (lanes[None, :] // HEAD_DIM), dtype=bf16)
    gla0 = jnp.zeros((bp, N_HEADS, GLA_DK, HEAD_DIM), f32)
    rwkv0 = jnp.zeros((bp, N_HEADS, HEAD_DIM, HEAD_DIM), f32)
    shift0 = jnp.zeros((bp, RWKV_COLS), f32)
    yp = x_prompt.reshape(bp * tp, D_MODEL)
    ys = x_sample.reshape(bs * ts, D_MODEL)
    pg, pr, psh, pw = [], [], [], [[], [], []]
    sg, sr, ssh, sw, sv = [], [], [], [[], [], []], []
    for l in range(depth):
        p = _prep_layer(l, norm_gains, w_ff_gate, w_ff_up, w_ff_down, w_in, w_out, gla_w_alpha2, gla_b_alpha,
                        gla_norm, gmlp_ln_g, gmlp_ln_b, gmlp_ws, gmlp_bs, rwkv_mu, rwkv_w0, rwkv_w2, rwkv_a0,
                        rwkv_a2, rwkv_g2, rwkv_kk, rwkv_ka, rwkv_rk, rwkv_lnx_g, rwkv_lnx_b)
        yp, g_p, r_p, sh_p, _, kv_p = _trunk_layer(yp, p, seg_b, bp, tp, gla0, rwkv0, shift0, None)
        layer_caches = [c[l].reshape(bs, c.shape[2], 2 * GROUP_WIDTH) for c in caches]
        ys, g_s, r_s, sh_s, v_s, kv_s = _trunk_layer(ys, p, seg_b, bs, ts, state_gla[l], state_rwkv[l],
                                                     state_shift[l], layer_caches)
        pg.append(g_p)
        pr.append(r_p)
        psh.append(sh_p)
        sg.append(g_s)
        sr.append(r_s)
        ssh.append(sh_s)
        sv.append(v_s.reshape(bs, ts, GROUP_WIDTH))
        for gi in range(len(DIL_PAIRS)):
            pw[gi].append(kv_p[gi])
            sw[gi].append(kv_s[gi])
    return (yp.reshape(bp, tp, D_MODEL), ys.reshape(bs, ts, D_MODEL),
            jnp.stack(pg), jnp.stack(pr), jnp.stack(psh),
            jnp.stack(pw[0]), jnp.stack(pw[1]), jnp.stack(pw[2]),
            jnp.stack(sg), jnp.stack(sr), jnp.stack(ssh),
            jnp.stack(sw[0]), jnp.stack(sw[1]), jnp.stack(sw[2]),
            jnp.stack(sv))
```

```python
import functools

import jax
import jax.numpy as jnp
import numpy as np
from jax import lax
from jax.experimental import pallas as pl
from jax.experimental.pallas import tpu as pltpu

f32 = jnp.float32
bf16 = jnp.bfloat16

D_MODEL = 1024
HEAD_DIM = 64
N_HEADS = 4
GROUP_WIDTH = N_HEADS * HEAD_DIM
D_FF = 2816
EPS = 1e-6
GLA_DK = 32
GLA_RANK = 16
GLA_TAU = 16.0
GLA_CHUNK = 64
GMLP_CHUNK = 128
LN_EPS = 1e-5
RWKV_W_RANK = 64
RWKV_A_RANK = 64
RWKV_G_RANK = 128
RWKV_LNX_EPS = 64e-5
DIL_PAIRS = ((128, 1), (512, 4), (2048, 16))
N_DIL_HEADS = len(DIL_PAIRS) * N_HEADS
ALIBI_MAX_EXP = 8.0
GLA_COLS = 2 * N_HEADS * GLA_DK + GROUP_WIDTH + GLA_RANK + GROUP_WIDTH
GMLP_COLS = 2 * GROUP_WIDTH
RWKV_COLS = 3 * GROUP_WIDTH + RWKV_W_RANK + RWKV_A_RANK + RWKV_G_RANK
DIL_COLS = len(DIL_PAIRS) * 3 * GROUP_WIDTH

LANES = 128
SUBLANES = 8
VMEM_LIMIT_BYTES = 56 * 1024 * 1024

GLA_COLS_PAD = 896
DIL_SLABS = DIL_COLS // LANES // len(DIL_PAIRS)
FFN_CHUNK = 256
ROW_TILE = 512
RWKV_BATCH = 8
RWKV_TBLOCK = 64
RWKV_GROUP = 2
RWKV_UNROLL = 4


def _alibi_slope(head):
    return float(2.0 ** (-ALIBI_MAX_EXP * (head + 1.0) / N_DIL_HEADS))


def _iota(shape, dim):
    return lax.broadcasted_iota(jnp.int32, shape, dim)


def _dot(a, b, precision=None):
    return jnp.dot(a, b, preferred_element_type=f32, precision=precision)


def _dot_tr(a, b, precision=None):
    return lax.dot_general(a, b, (((1,), (1,)), ((), ())), preferred_element_type=f32, precision=precision)


def _dot_tl(a, b, precision=None):
    return lax.dot_general(a, b, (((0,), (0,)), ((), ())), preferred_element_type=f32, precision=precision)


def _rms_rows(x, g):
    ms = jnp.mean(x * x, axis=-1, keepdims=True)
    return x * lax.rsqrt(ms + EPS) * g


def _softplus(x):
    return jnp.maximum(x, 0.0) + jnp.log1p(jnp.exp(-jnp.abs(x)))


def _split(x, n):
    parts = []
    for _ in range(n - 1):
        p = x.astype(bf16)
        parts.append(p)
        x = x - p.astype(f32)
    parts.append(x.astype(bf16))
    return parts


def _split_param(w):
    hi = w.astype(bf16)
    return hi, (w - hi.astype(f32)).astype(bf16)


def _dot_exact(x, m_b, n=2):
    return functools.reduce(jnp.add, [_dot(p, m_b) for p in _split(x, n)])


def _dot_param(x, w_hi, w_lo):
    x_hi, x_lo = _split(x, 2)
    return _dot(x_hi, w_hi) + _dot(x_lo, w_hi) + _dot(x_hi, w_lo)


def _params(*semantics):
    return pltpu.CompilerParams(dimension_semantics=semantics, vmem_limit_bytes=VMEM_LIMIT_BYTES)


def _resident(shape):
    zeros = (0,) * len(shape)
    return pl.BlockSpec(shape, lambda *_: zeros, pipeline_mode=pl.Buffered(1))


def _ffn_body(x_ref, gpre_ref, gpost_ref, wg_ref, wu_ref, wd_ref, o_ref):
    x = x_ref[...]
    xn = _rms_rows(x, gpre_ref[...]).astype(bf16)
    acc = None
    for c in range(D_FF // FFN_CHUNK):
        lo, hi = c * FFN_CHUNK, (c + 1) * FFN_CHUNK
        g = _dot(xn, wg_ref[:, lo:hi])
        u = _dot(xn, wu_ref[:, lo:hi])
        act = (g * jax.nn.sigmoid(g) * u).astype(bf16)
        part = _dot(act, wd_ref[lo:hi, :])
        acc = part if acc is None else acc + part
    o_ref[...] = x + 0.5 * _rms_rows(acc, gpost_ref[...])


def _ffn(x, g_pre, g_post, wg, wu, wd):
    n = x.shape[0]
    tm = min(ROW_TILE, n)
    row = pl.BlockSpec((tm, D_MODEL), lambda i: (i, 0))
    return pl.pallas_call(
        _ffn_body,
        out_shape=jax.ShapeDtypeStruct((n, D_MODEL), f32),
        grid=(n // tm,),
        in_specs=[row, _resident((1, D_MODEL)), _resident((1, D_MODEL)),
                  _resident((D_MODEL, D_FF)), _resident((D_MODEL, D_FF)), _resident((D_FF, D_MODEL))],
        out_specs=row,
        compiler_params=_params("parallel"),
        name="ffn",
    )(x, g_pre, g_post, wg, wu, wd)


def _proj_body(h_ref, g_ref, wgla_ref, wgmlp_ref, wrwkv_ref, wdil_ref, ogla_ref, ogmlp_ref, orwkv_ref, odil_ref):
    xn = _rms_rows(h_ref[...], g_ref[...]).astype(bf16)
    ogla_ref[...] = _dot(xn, wgla_ref[...])
    ogmlp_ref[...] = _dot(xn, wgmlp_ref[...])
    orwkv_ref[...] = _dot(xn, wrwkv_ref[...])
    res = _dot(xn, wdil_ref[...])
    for s in range(DIL_COLS // LANES):
        odil_ref[s // DIL_SLABS, s % DIL_SLABS] = res[:, s * LANES:(s + 1) * LANES]


def _proj(h, g, w_gla, w_gmlp, w_rwkv, w_dil):
    n = h.shape[0]
    tm = min(ROW_TILE, n)
    n_grp = len(DIL_PAIRS)
    return pl.pallas_call(
        _proj_body,
        out_shape=(jax.ShapeDtypeStruct((n, GLA_COLS_PAD), f32),
                   jax.ShapeDtypeStruct((n, GMLP_COLS), f32),
                   jax.ShapeDtypeStruct((n, RWKV_COLS), f32),
                   jax.ShapeDtypeStruct((n_grp, DIL_SLABS, n, LANES), f32)),
        grid=(n // tm,),
        in_specs=[pl.BlockSpec((tm, D_MODEL), lambda i: (i, 0)), _resident((1, D_MODEL)),
                  _resident((D_MODEL, GLA_COLS_PAD)), _resident((D_MODEL, GMLP_COLS)),
                  _resident((D_MODEL, RWKV_COLS)), _resident((D_MODEL, DIL_COLS))],
        out_specs=(pl.BlockSpec((tm, GLA_COLS_PAD), lambda i: (i, 0)),
                   pl.BlockSpec((tm, GMLP_COLS), lambda i: (i, 0)),
                   pl.BlockSpec((tm, RWKV_COLS), lambda i: (i, 0)),
                   pl.BlockSpec((n_grp, DIL_SLABS, tm, LANES), lambda i: (0, 0, i, 0))),
        compiler_params=_params("parallel"),
        name="proj",
    )(h, g, w_gla, w_gmlp, w_rwkv, w_dil)


def _outproj_body(h_ref, ogla_ref, ogmlp_ref, orwkv_ref, odil_ref, w_ref, g_ref, o_ref):
    gw = GROUP_WIDTH
    mix = _dot(ogla_ref[...].astype(bf16), w_ref[0:gw, :])
    mix += _dot(ogmlp_ref[...].astype(bf16), w_ref[gw:2 * gw, :])
    mix += _dot(orwkv_ref[...].astype(bf16), w_ref[2 * gw:3 * gw, :])
    for p in range(gw // LANES):
        mix += _dot(odil_ref[p].astype(bf16), w_ref[3 * gw + p * LANES:3 * gw + (p + 1) * LANES, :])
    o_ref[...] = h_ref[...] + _rms_rows(mix, g_ref[...])


def _outproj(h, o_gla, o_gmlp, o_rwkv, o_dil, w_out, g):
    n = h.shape[0]
    tm = min(ROW_TILE, n)
    row = pl.BlockSpec((tm, D_MODEL), lambda i: (i, 0))
    grp = pl.BlockSpec((tm, GROUP_WIDTH), lambda i: (i, 0))
    return pl.pallas_call(
        _outproj_body,
        out_shape=jax.ShapeDtypeStruct((n, D_MODEL), f32),
        grid=(n // tm,),
        in_specs=[row, grp, grp, grp, pl.BlockSpec((GROUP_WIDTH // LANES, tm, LANES), lambda i: (0, i, 0)),
                  _resident((D_MODEL, D_MODEL)), _resident((1, D_MODEL))],
        out_specs=row,
        compiler_params=_params("parallel"),
        name="outproj",
    )(h, o_gla, o_gmlp, o_rwkv, o_dil, w_out, g)


def _gla_body(c_ref, h0_ref, wa_hi_ref, wa_lo_ref, ba_ref, ng_ref, o_ref, sfin_ref, s_scr, *, n_chunks, valid_rows):
    C = GLA_CHUNK
    hk, hv = N_HEADS * GLA_DK, GROUP_WIDTH
    tb = pl.program_id(1)

    @pl.when(tb == 0)
    def _init():
        s_scr[...] = jnp.zeros((hk, hv), f32)
        for h in range(N_HEADS):
            s_scr[GLA_DK * h:GLA_DK * (h + 1), HEAD_DIM * h:HEAD_DIM * (h + 1)] = h0_ref[0, h]

    rr, cc = _iota((C, C), 0), _iota((C, C), 1)
    causal = rr >= cc
    tril = causal.astype(bf16)
    k_head = _iota((1, hk), 1) >> 5
    v_head = _iota((1, hv), 1) >> 6
    diag_blocks = (_iota((hk, hv), 0) >> 5) == (_iota((hk, hv), 1) >> 6)
    seg_b = ((_iota((hv, hv), 0) >> 6) == (_iota((hv, hv), 1) >> 6)).astype(bf16)
    ones_cv = jnp.ones((C, hv), bf16)

    for ci in range(n_chunks):
        x = c_ref[0, ci * C:(ci + 1) * C, :]
        q = x[:, 0:hk] * (GLA_DK ** -0.5)
        k = x[:, hk:2 * hk]
        v = x[:, 2 * hk:2 * hk + hv]
        g = x[:, 2 * hk + hv:2 * hk + 2 * hv]
        a_lr = x[:, 2 * hk + 2 * hv:]
        z = _dot_param(a_lr, wa_hi_ref[...], wa_lo_ref[...]) + ba_ref[...]
        log_a = -_softplus(-z) * (1.0 / GLA_TAU)
        if valid_rows < C:
            log_a = jnp.where(_iota((C, hk), 0) < valid_rows, log_a, 0.0)
        la_parts = _split(log_a, 3)
        b = functools.reduce(jnp.add, [_dot(tril, p) for p in la_parts])
        b_last = b[C - 1:C, :]
        qe = q * jnp.exp(b)
        ke = k * jnp.exp(-b)
        kd = k * jnp.exp(b_last - b)
        state = s_scr[...]
        o = _dot(qe.astype(bf16), state.astype(bf16))
        keb = ke.astype(bf16)
        for h in range(N_HEADS):
            att = _dot_tr(jnp.where(k_head == h, qe, 0.0).astype(bf16), keb)
            att = jnp.where(causal, att, 0.0)
            o += _dot(att.astype(bf16), jnp.where(v_head == h, v, 0.0).astype(bf16))
        incr = jnp.where(diag_blocks, _dot_tl(kd.astype(bf16), v.astype(bf16)), 0.0)
        chunk_log_decay = functools.reduce(jnp.add, [_dot_tl(p, ones_cv) for p in la_parts])
        s_scr[...] = jnp.exp(chunk_log_decay) * state + incr
        ms = _dot_exact(o * o, seg_b) * (1.0 / HEAD_DIM)
        o_ref[0, ci * C:(ci + 1) * C, :] = o * lax.rsqrt(ms + EPS) * ng_ref[...] * (g * jax.nn.sigmoid(g))

    @pl.when(tb == pl.num_programs(1) - 1)
    def _fin():
        for h in range(N_HEADS):
            sfin_ref[0, h] = s_scr[GLA_DK * h:GLA_DK * (h + 1), HEAD_DIM * h:HEAD_DIM * (h + 1)]


def _gla(c, h0, wa_pad, b_alpha, norm_g, valid_rows):
    bsz, t, _ = c.shape
    tb = min(t, 4 * GLA_CHUNK)
    wa_spec = pl.BlockSpec((LANES, N_HEADS * GLA_DK), lambda b, i: (0, 0))
    return pl.pallas_call(
        functools.partial(_gla_body, n_chunks=tb // GLA_CHUNK, valid_rows=valid_rows),
        out_shape=(jax.ShapeDtypeStruct((bsz, t, GROUP_WIDTH), f32),
                   jax.ShapeDtypeStruct((bsz, N_HEADS, GLA_DK, HEAD_DIM), f32)),
        grid=(bsz, t // tb),
        in_specs=[pl.BlockSpec((1, tb, GLA_COLS_PAD), lambda b, i: (b, i, 0)),
                  pl.BlockSpec((1, N_HEADS, GLA_DK, HEAD_DIM), lambda b, i: (b, 0, 0, 0)),
                  wa_spec, wa_spec,
                  pl.BlockSpec((1, N_HEADS * GLA_DK), lambda b, i: (0, 0)),
                  pl.BlockSpec((1, GROUP_WIDTH), lambda b, i: (0, 0))],
        out_specs=(pl.BlockSpec((1, tb, GROUP_WIDTH), lambda b, i: (b, i, 0)),
                   pl.BlockSpec((1, N_HEADS, GLA_DK, HEAD_DIM), lambda b, i: (b, 0, 0, 0))),
        scratch_shapes=[pltpu.VMEM((N_HEADS * GLA_DK, GROUP_WIDTH), f32)],
        compiler_params=_params("parallel", "arbitrary"),
        name="gla",
    )(c, h0, *wa_pad, b_alpha, norm_g)


def _gmlp_body(c_ref, lng_ref, lnb_ref, ws_ref, bias_ref, o_ref, *v_refs):
    gw = GROUP_WIDTH
    x = c_ref[...]
    ge = 0.5 * x * (1.0 + lax.erf(x * float(np.sqrt(0.5))))
    u, v = ge[:, :gw], ge[:, gw:]
    mean = jnp.mean(v, axis=-1, keepdims=True)
    d = v - mean
    var = jnp.mean(d * d, axis=-1, keepdims=True)
    vn = d * lax.rsqrt(var + LN_EPS) * lng_ref[...] + lnb_ref[...]
    n = GMLP_CHUNK
    causal = _iota((n, n), 0) >= _iota((n, n), 1)
    v_head = _iota((1, gw), 1) >> 6
    vb = vn.astype(bf16)
    s = jnp.zeros((n, gw), f32)
    for grp in range(N_HEADS):
        w = jnp.where(causal, ws_ref[grp], 0.0).astype(bf16)
        s = jnp.where(v_head == grp, _dot(w, vb), s)
    o_ref[...] = u * (s + bias_ref[...])
    if v_refs:
        v_refs[0][...] = vn


def _gmlp(c, ln_g, ln_b, ws, bias_rows, emit_v):
    n = c.shape[0]
    blk = pl.BlockSpec((GMLP_CHUNK, GROUP_WIDTH), lambda i: (i, 0))
    out_shape = [jax.ShapeDtypeStruct((n, GROUP_WIDTH), f32)]
    out_specs = [blk]
    if emit_v:
        out_shape.append(jax.ShapeDtypeStruct((n, GROUP_WIDTH), f32))
        out_specs.append(blk)
    return pl.pallas_call(
        _gmlp_body,
        out_shape=tuple(out_shape),
        grid=(n // GMLP_CHUNK,),
        in_specs=[pl.BlockSpec((GMLP_CHUNK, GMLP_COLS), lambda i: (i, 0)),
                  pl.BlockSpec((1, GROUP_WIDTH), lambda i: (0, 0)),
                  pl.BlockSpec((1, GROUP_WIDTH), lambda i: (0, 0)),
                  pl.BlockSpec((N_HEADS, GMLP_CHUNK, GMLP_CHUNK), lambda i: (0, 0, 0)),
                  pl.BlockSpec((GMLP_CHUNK, GROUP_WIDTH), lambda i: (0, 0))],
        out_specs=tuple(out_specs),
        compiler_params=_params("parallel"),
        name="gmlp",
    )(c, ln_g, ln_b, ws, bias_rows)


def _rwkv_body(c_ref, sp_ref, s0_ref, mu_ref, w0_ref, w2h_ref, w2l_ref, a0_ref, a2h_ref, a2l_ref, g2h_ref, g2l_ref,
               kk_ref, ka_ref, rk_ref, lg_ref, lb_ref, seg_ref, o_ref, sfin_ref,
               s_scr, carry_scr, r_scr, w_scr, k_scr, v_scr, a_scr, b_scr, y_scr, g_scr, bonus_scr, sr_scr,
               *, nb, n_steps):
    gw = GROUP_WIDTH
    tbr = c_ref.shape[1]
    tb = pl.program_id(1)
    seg_b = seg_ref[...]

    @pl.when(tb == 0)
    def _init():
        for b in range(nb):
            carry_scr[b] = sp_ref[b]
            for h in range(N_HEADS):
                s_scr[b, :, HEAD_DIM * h:HEAD_DIM * (h + 1)] = s0_ref[b, h]

    first_row = _iota((tbr, RWKV_COLS), 0) == 0
    for b in range(nb):
        c = c_ref[b]
        prev = jnp.where(first_row, carry_scr[b], pltpu.roll(c, 1, 0))
        carry_scr[b] = c[tbr - 1:tbr, :]
        xm = c + (prev - c) * mu_ref[...]
        r = xm[:, 0:gw]
        k = xm[:, gw:2 * gw]
        v = xm[:, 2 * gw:3 * gw]
        wa_lr = xm[:, 3 * gw:3 * gw + LANES]
        g_lr = xm[:, 3 * gw + LANES:]
        w = -_softplus(-(w0_ref[...] + _dot_param(jnp.tanh(wa_lr), w2h_ref[...], w2l_ref[...]))) - 0.5
        a = jax.nn.sigmoid(a0_ref[...] + _dot_param(wa_lr, a2h_ref[...], a2l_ref[...]))
        g = _dot_param(jax.nn.sigmoid(g_lr), g2h_ref[...], g2l_ref[...])
        kk = k * kk_ref[...]
        kk = kk / jnp.maximum(jnp.sqrt(_dot_exact(kk * kk, seg_b)), 1e-12)
        k2 = k * (1.0 + (a - 1.0) * ka_ref[...])
        r_scr[b] = r
        w_scr[b] = jnp.exp(-jnp.exp(w))
        k_scr[b] = k2
        v_scr[b] = v
        a_scr[b] = -kk
        b_scr[b] = kk * a
        g_scr[b] = g
        bonus_scr[b] = _dot_exact(r * k2 * rk_ref[...], seg_b) * v
        y_scr[b] = jnp.zeros((tbr, gw), f32)

    eye = (_iota((HEAD_DIM, gw), 1) & (HEAD_DIM - 1)) == _iota((HEAD_DIM, gw), 0)

    hd = HEAD_DIM

    groups = [list(range(g0, g0 + RWKV_GROUP)) for g0 in range(0, nb, RWKV_GROUP)]

    def feedback_dot(row, batches):
        states, parts = [], []
        for b in batches:
            state = s_scr[b]
            sa_in = state * a_scr[b, row, :]
            hi = sa_in.astype(bf16)
            parts += [hi, (sa_in - hi.astype(f32)).astype(bf16),
                      jnp.where(eye, v_scr[b, row, :], 0.0).astype(bf16)]
            states.append(state)
        return states, _dot(jnp.concatenate(parts, axis=0), seg_b)

    def update(row, batches, states, res):
        for i, b in enumerate(batches):
            base = 3 * hd * i
            sa = res[base:base + hd] + res[base + hd:base + 2 * hd]
            v_col = res[base + 2 * hd:base + 3 * hd]
            state = states[i] * w_scr[b, row, :] + sa * b_scr[b, row, :] + v_col * k_scr[b, row, :]
            s_scr[b] = state
            sr_scr[b] = state * r_scr[b, row, :]

    def readout(row, batches):
        res = _dot(jnp.concatenate([sr_scr[b].astype(bf16) for b in batches], axis=0), seg_b)
        for i, b in enumerate(batches):
            y = res[hd * i:hd * (i + 1)]
            y_scr[b, row, :] = jnp.sum(jnp.where(eye, y, 0.0), axis=0, keepdims=True)

    def step(t, carry):
        row = pl.ds(t, 1)
        prev_row = pl.ds(jnp.maximum(t - 1, 0), 1)
        fed = [feedback_dot(row, g) for g in groups]
        for g in groups:
            readout(prev_row, g)
        for g, f in zip(groups, fed):
            update(row, g, *f)
        return carry

    for b in range(nb):
        sr_scr[b] = jnp.zeros((hd, gw), f32)
    lax.fori_loop(0, n_steps, step, 0, unroll=RWKV_UNROLL)
    for g in groups:
        readout(pl.ds(n_steps - 1, 1), g)

    for b in range(nb):
        y = y_scr[b]
        d = y - _dot_exact(y, seg_b) * (1.0 / HEAD_DIM)
        var = _dot_exact(d * d, seg_b) * (1.0 / HEAD_DIM)
        yn = d * lax.rsqrt(var + RWKV_LNX_EPS) * lg_ref[...] + lb_ref[...]
        o_ref[b] = (yn + bonus_scr[b]) * g_scr[b]

    @pl.when(tb == pl.num_programs(1) - 1)
    def _fin():
        for b in range(nb):
            for h in range(N_HEADS):
                sfin_ref[b, h] = s_scr[b, :, HEAD_DIM * h:HEAD_DIM * (h + 1)]


def _rwkv(c, shift_prev, s0, p, seg_b, n_valid):
    bsz, t, _ = c.shape
    nb = RWKV_BATCH
    tbr = min(t, RWKV_TBLOCK)
    n_steps = min(n_valid, tbr)
    vec = lambda width: pl.BlockSpec((1, width), lambda b, i: (0, 0))
    mat = lambda rows: pl.BlockSpec((rows, GROUP_WIDTH), lambda b, i: (0, 0))
    rows_scr = pltpu.VMEM((nb, tbr, GROUP_WIDTH), f32)
    return pl.pallas_call(
        functools.partial(_rwkv_body, nb=nb, n_steps=n_steps),
        out_shape=(jax.ShapeDtypeStruct((bsz, t, GROUP_WIDTH), f32),
                   jax.ShapeDtypeStruct((bsz, N_HEADS, HEAD_DIM, HEAD_DIM), f32)),
        grid=(bsz // nb, t // tbr),
        in_specs=[pl.BlockSpec((nb, tbr, RWKV_COLS), lambda b, i: (b, i, 0)),
                  pl.BlockSpec((nb, 1, RWKV_COLS), lambda b, i: (b, 0, 0)),
                  pl.BlockSpec((nb, N_HEADS, HEAD_DIM, HEAD_DIM), lambda b, i: (b, 0, 0, 0)),
                  vec(RWKV_COLS), vec(GROUP_WIDTH), mat(LANES), mat(LANES), vec(GROUP_WIDTH), mat(LANES), mat(LANES),
                  mat(RWKV_G_RANK), mat(RWKV_G_RANK),
                  vec(GROUP_WIDTH), vec(GROUP_WIDTH), vec(GROUP_WIDTH), vec(GROUP_WIDTH), vec(GROUP_WIDTH),
                  mat(GROUP_WIDTH)],
        out_specs=(pl.BlockSpec((nb, tbr, GROUP_WIDTH), lambda b, i: (b, i, 0)),
                   pl.BlockSpec((nb, N_HEADS, HEAD_DIM, HEAD_DIM), lambda b, i: (b, 0, 0, 0))),
        scratch_shapes=[pltpu.VMEM((nb, HEAD_DIM, GROUP_WIDTH), f32), pltpu.VMEM((nb, 1, RWKV_COLS), f32)]
        + [rows_scr] * 9 + [pltpu.VMEM((nb, HEAD_DIM, GROUP_WIDTH), f32)],
        compiler_params=_params("parallel", "arbitrary"),
        name="rwkv",
    )(c, shift_prev, s0, p["mu"], p["w0"], *p["w2"], p["a0"], *p["a2"], *p["g2"], p["kk"], p["ka"], p["rk"],
      p["lnx_g"], p["lnx_b"], seg_b)


def _softmax_tiles(scores, values):
    mx = functools.reduce(jnp.maximum, [jnp.max(s, axis=-1, keepdims=True) for s in scores])
    out, den = None, None
    for s, v in zip(scores, values):
        p = jnp.exp(s - mx)
        ps = jnp.sum(p, axis=-1, keepdims=True)
        pv = _dot(p.astype(bf16), v)
        out = pv if out is None else out + pv
        den = ps if den is None else den + ps
    return out, mx, den


def _merge_rows(acc, mrun, den, o_new, m_new, d_new):
    m = jnp.maximum(mrun, m_new)
    a_old, a_new = jnp.exp(mrun - m), jnp.exp(m_new - m)
    return acc * a_old + o_new * a_new, m, den * a_old + d_new * a_new


def _dil_prompt_body(x_ref, o_ref, acc_scr, m_scr, d_scr, *, t):
    grp = pl.program_id(1)
    blk = LANES
    rr, cc = _iota((blk, blk), 0), _iota((blk, blk), 1)
    first_head = _iota((1, LANES), 1) < HEAD_DIM
    neg_inf = float("-inf")

    def group(gi, dilation):
        n_sub = t // dilation // blk
        dist_cur = (rr - cc).astype(f32) * float(dilation)
        dist_prev = (rr + blk - cc).astype(f32) * float(dilation)

        def combo(idx, carry):
            res, sub = idx // n_sub, idx % n_sub
            start = res + dilation * blk * sub
            pstart = jnp.maximum(start - dilation * blk, 0)
            rows = pl.ds(start, blk, stride=dilation) if dilation > 1 else pl.ds(start, blk)
            prows = pl.ds(pstart, blk, stride=dilation) if dilation > 1 else pl.ds(pstart, blk)
            for pair in range(GROUP_WIDTH // LANES):
                q = x_ref[0, pair, rows, :] * (HEAD_DIM ** -0.5)
                kc = x_ref[0, 2 + pair, rows, :].astype(bf16)
                vc = x_ref[0, 4 + pair, rows, :].astype(bf16)
                if n_sub > 1:
                    kp = x_ref[0, 2 + pair, prows, :].astype(bf16)
                    vp = x_ref[0, 4 + pair, prows, :].astype(bf16)
                o_pair = m_pair = d_pair = None
                for hh in range(2):
                    slope = _alibi_slope(gi * N_HEADS + 2 * pair + hh)
                    lanes = first_head if hh == 0 else jnp.logical_not(first_head)
                    qm = jnp.where(lanes, q, 0.0).astype(bf16)
                    scores = [jnp.where(rr >= cc, _dot_tr(qm, kc) - slope * dist_cur, neg_inf)]
                    values = [vc]
                    if n_sub > 1:
                        s_prev = jnp.where(cc >= rr, _dot_tr(qm, kp) - slope * dist_prev, neg_inf)
                        scores.append(s_prev + jnp.where(sub > 0, 0.0, neg_inf))
                        values.append(vp)
                    o_h, m_h, d_h = _softmax_tiles(scores, values)
                    m_h = jnp.broadcast_to(m_h, (blk, LANES))
                    d_h = jnp.broadcast_to(d_h, (blk, LANES))
                    if hh == 0:
                        o_pair, m_pair, d_pair = o_h, m_h, d_h
                    else:
                        o_pair = jnp.where(lanes, o_h, o_pair)
                        m_pair = jnp.where(lanes, m_h, m_pair)
                        d_pair = jnp.where(lanes, d_h, d_pair)
                if gi > 0:
                    o_pair, m_pair, d_pair = _merge_rows(acc_scr[pair, rows, :], m_scr[pair, rows, :],
                                                         d_scr[pair, rows, :], o_pair, m_pair, d_pair)
                acc_scr[pair, rows, :] = o_pair
                m_scr[pair, rows, :] = m_pair
                d_scr[pair, rows, :] = d_pair
            return carry

        lax.fori_loop(0, t // blk, combo, 0)

    for gi, (_, dilation) in enumerate(DIL_PAIRS):
        pl.when(grp == gi)(functools.partial(group, gi, dilation))

    @pl.when(grp == len(DIL_PAIRS) - 1)
    def _fin():
        o_ref[...] = acc_scr[...] / d_scr[...]


def _dil_prompt(x, bsz, t):
    n_pairs = GROUP_WIDTH // LANES
    scr = pltpu.VMEM((n_pairs, t, LANES), f32)
    return pl.pallas_call(
        functools.partial(_dil_prompt_body, t=t),
        out_shape=jax.ShapeDtypeStruct((n_pairs, bsz * t, LANES), f32),
        grid=(bsz, len(DIL_PAIRS)),
        in_specs=[pl.BlockSpec((1, DIL_SLABS, t, LANES), lambda b, g: (g, 0, b, 0))],
        out_specs=pl.BlockSpec((n_pairs, t, LANES), lambda b, g: (0, b, 0)),
        scratch_shapes=[scr, scr, scr],
        compiler_params=_params("parallel", "arbitrary"),
        name="dil_prompt",
    )(x)


def _dil_sample_body(x_ref, c0_ref, c1_ref, c2_ref, o_ref, *, t_new):
    caches = (c0_ref, c1_ref, c2_ref)
    nq = SUBLANES
    per = nq // t_new
    first_head = _iota((1, LANES), 1) < HEAD_DIM
    neg_inf = float("-inf")
    q_idx = _iota((nq, 1), 0) & (t_new - 1)
    q_bat = _iota((nq, 1), 0) >> 2
    rn, cn = _iota((nq, nq), 0), _iota((nq, nq), 1)
    dn = (rn & (t_new - 1)) - (cn & (t_new - 1))
    same = (rn >> 2) == (cn >> 2)
    for pair in range(GROUP_WIDTH // LANES):
        acc = mrun = den = None
        for gi, (window, dilation) in enumerate(DIL_PAIRS):
            q = x_ref[gi, pair] * (HEAD_DIM ** -0.5)
            kn = x_ref[gi, 2 + pair].astype(bf16)
            vn = x_ref[gi, 4 + pair].astype(bf16)
            dist_c = window + q_idx - _iota((nq, window), 1)
            ok_c = jnp.logical_and((dist_c & (dilation - 1)) == 0, dist_c <= window)
            ok_n = jnp.logical_and(jnp.logical_and(same, dn >= 0), (dn & (dilation - 1)) == 0)
            o_pair = m_pair = d_pair = None
            for hh in range(2):
                slope = _alibi_slope(gi * N_HEADS + 2 * pair + hh)
                lanes = first_head if hh == 0 else jnp.logical_not(first_head)
                qm = jnp.where(lanes, q, 0.0).astype(bf16)
                s_new = jnp.where(ok_n, _dot_tr(qm, kn) - slope * dn.astype(f32), neg_inf)
                o_h = m_h = d_h = None
                for s in range(per):
                    kc = caches[gi][s, :, pair * LANES:(pair + 1) * LANES].astype(bf16)
                    vc = caches[gi][s, :, GROUP_WIDTH + pair * LANES:GROUP_WIDTH + (pair + 1) * LANES].astype(bf16)
                    s_c = jnp.where(ok_c, _dot_tr(qm, kc) - slope * dist_c.astype(f32), neg_inf)
                    o_s, m_s, d_s = _softmax_tiles([s_c, s_new], [vc, vn])
                    if s == 0:
                        o_h, m_h, d_h = o_s, m_s, d_s
                    else:
                        mine = q_bat == s
                        o_h = jnp.where(mine, o_s, o_h)
                        m_h = jnp.where(mine, m_s, m_h)
                        d_h = jnp.where(mine, d_s, d_h)
                m_h = jnp.broadcast_to(m_h, (nq, LANES))
                d_h = jnp.broadcast_to(d_h, (nq, LANES))
                if hh == 0:
                    o_pair, m_pair, d_pair = o_h, m_h, d_h
                else:
                    o_pair = jnp.where(lanes, o_h, o_pair)
                    m_pair = jnp.where(lanes, m_h, m_pair)
                    d_pair = jnp.where(lanes, d_h, d_pair)
            if gi == 0:
                acc, mrun, den = o_pair, m_pair, d_pair
            else:
                acc, mrun, den = _merge_rows(acc, mrun, den, o_pair, m_pair, d_pair)
        o_ref[pair] = acc / den


def _dil_sample(x, caches, bsz, t_new):
    n_pairs = GROUP_WIDTH // LANES
    per = SUBLANES // t_new
    return pl.pallas_call(
        functools.partial(_dil_sample_body, t_new=t_new),
        out_shape=jax.ShapeDtypeStruct((n_pairs, bsz * t_new, LANES), f32),
        grid=(bsz // per,),
        in_specs=[pl.BlockSpec((len(DIL_PAIRS), DIL_SLABS, SUBLANES, LANES), lambda i: (0, 0, i, 0))]
        + [pl.BlockSpec((per, w, 2 * GROUP_WIDTH), lambda i: (i, 0, 0)) for w, _ in DIL_PAIRS],
        out_specs=pl.BlockSpec((n_pairs, SUBLANES, LANES), lambda i: (0, i, 0)),
        compiler_params=_params("parallel"),
        name="dil_sample",
    )(x, *caches)


def _row(v):
    return v.reshape(1, -1)


def _prep_layer(l, norm_gains, w_ff_gate, w_ff_up, w_ff_down, w_in, w_out, gla_w_alpha2, gla_b_alpha, gla_norm,
                gmlp_ln_g, gmlp_ln_b, gmlp_ws, gmlp_bs, rwkv_mu, rwkv_w0, rwkv_w2, rwkv_a0, rwkv_a2, rwkv_g2,
                rwkv_kk, rwkv_ka, rwkv_rk, rwkv_lnx_g, rwkv_lnx_b):
    win = w_in[l]
    o1, o2, o3 = GLA_COLS, GLA_COLS + GMLP_COLS, GLA_COLS + GMLP_COLS + RWKV_COLS
    hk = N_HEADS * GLA_DK
    a_lo, a_hi = 2 * hk + GROUP_WIDTH, 2 * hk + GROUP_WIDTH + GLA_RANK
    w_gla = win[:, :o1]
    w_gla = jnp.concatenate([w_gla[:, :a_lo], w_gla[:, a_hi:], w_gla[:, a_lo:a_hi],
                             jnp.zeros((D_MODEL, GLA_COLS_PAD - GLA_COLS), f32)], axis=1)
    zeros_r = jnp.zeros((RWKV_W_RANK, GROUP_WIDTH), f32)
    return dict(
        norms=[_row(norm_gains[l, i]) for i in range(norm_gains.shape[1])],
        ff_gate=[w_ff_gate[l, i].astype(bf16) for i in range(2)],
        ff_up=[w_ff_up[l, i].astype(bf16) for i in range(2)],
        ff_down=[w_ff_down[l, i].astype(bf16) for i in range(2)],
        w_gla=w_gla.astype(bf16), w_gmlp=win[:, o1:o2].astype(bf16), w_rwkv=win[:, o2:o3].astype(bf16),
        w_dil=win[:, o3:].astype(bf16), w_out=w_out[l].astype(bf16),
        gla_wa=_split_param(jnp.concatenate([gla_w_alpha2[l], jnp.zeros((LANES - GLA_RANK, hk), f32)], axis=0)),
        gla_ba=_row(gla_b_alpha[l]), gla_norm=_row(gla_norm[l]),
        gmlp_ln_g=_row(gmlp_ln_g[l]), gmlp_ln_b=_row(gmlp_ln_b[l]), gmlp_ws=gmlp_ws[l], gmlp_bs=gmlp_bs[l],
        rwkv=dict(mu=_row(rwkv_mu[l]), w0=_row(rwkv_w0[l]),
                  w2=_split_param(jnp.concatenate([rwkv_w2[l], zeros_r], axis=0)), a0=_row(rwkv_a0[l]),
                  a2=_split_param(jnp.concatenate([zeros_r, rwkv_a2[l]], axis=0)), g2=_split_param(rwkv_g2[l]),
                  kk=_row(rwkv_kk[l]), ka=_row(rwkv_ka[l]), rk=_row(rwkv_rk[l]),
                  lnx_g=_row(rwkv_lnx_g[l]), lnx_b=_row(rwkv_lnx_b[l])),
    )


def _kv_rows(c_dil, bsz, t, rows):
    out = []
    for gi in range(len(DIL_PAIRS)):
        kv = c_dil[gi, 2:].reshape(2, GROUP_WIDTH // LANES, bsz, t, LANES)[:, :, :, t - rows[gi]:]
        kv = jnp.transpose(kv, (2, 3, 0, 1, 4))
        out.append(kv.reshape(bsz, rows[gi], 2, N_HEADS, HEAD_DIM))
    return out


def _trunk_layer(x, p, seg_b, bsz, t, gla_h0, rwkv_s0, shift0, caches):
    ng = p["norms"]
    n = bsz * t
    h = _ffn(x, ng[0], ng[1], p["ff_gate"][0], p["ff_up"][0], p["ff_down"][0])
    c_gla, c_gmlp, c_rwkv, c_dil = _proj(h, ng[2], p["w_gla"], p["w_gmlp"], p["w_rwkv"], p["w_dil"])
    is_prompt = caches is None

    t_gla = max(t, GLA_CHUNK)
    cg = c_gla.reshape(bsz, t, GLA_COLS_PAD)
    if t_gla != t:
        cg = jnp.pad(cg, ((0, 0), (0, t_gla - t), (0, 0)))
    o_gla, gla_s = _gla(cg, gla_h0, p["gla_wa"], p["gla_ba"], p["gla_norm"], min(t, GLA_CHUNK))
    o_gla = o_gla[:, :t].reshape(n, GROUP_WIDTH)

    if t >= GMLP_CHUNK:
        ws = p["gmlp_ws"]
        bias_rows = jnp.repeat(p["gmlp_bs"].T, HEAD_DIM, axis=1)
    else:
        per = GMLP_CHUNK // t
        eye = jnp.eye(per, dtype=f32)
        ws = jnp.einsum("ab,gij->gaibj", eye, p["gmlp_ws"][:, :t, :t]).reshape(N_HEADS, GMLP_CHUNK, GMLP_CHUNK)
        bias_rows = jnp.tile(jnp.repeat(p["gmlp_bs"][:, :t].T, HEAD_DIM, axis=1), (per, 1))
    gm = _gmlp(c_gmlp, p["gmlp_ln_g"], p["gmlp_ln_b"], ws, bias_rows, emit_v=not is_prompt)
    o_gmlp = gm[0]
    gmlp_v = None if is_prompt else gm[1]

    t_rw = max(t, SUBLANES)
    cr = c_rwkv.reshape(bsz, t, RWKV_COLS)
    shift = cr[:, -1]
    if t_rw != t:
        cr = jnp.pad(cr, ((0, 0), (0, t_rw - t), (0, 0)))
    o_rwkv, rwkv_s = _rwkv(cr, shift0.reshape(bsz, 1, RWKV_COLS), rwkv_s0, p["rwkv"], seg_b, t)
    o_rwkv = o_rwkv[:, :t].reshape(n, GROUP_WIDTH)

    if is_prompt:
        o_dil = _dil_prompt(c_dil, bsz, t)
        kv_new = _kv_rows(c_dil, bsz, t, [min(w, t) for w, _ in DIL_PAIRS])
    else:
        o_dil = _dil_sample(c_dil, caches, bsz, t)
        kv_new = _kv_rows(c_dil, bsz, t, [min(w, t) for w, _ in DIL_PAIRS])

    h = _outproj(h, o_gla, o_gmlp, o_rwkv, o_dil, p["w_out"], ng[3])
    y = _ffn(h, ng[4], ng[5], p["ff_gate"][1], p["ff_up"][1], p["ff_down"][1])
    return y, gla_s, rwkv_s, shift, gmlp_v, kv_new


def kernel(x_prompt, x_sample, cache_win128, cache_win512, cache_win2048, state_gla, state_rwkv, state_shift,
           norm_gains, w_ff_gate, w_ff_up, w_ff_down, w_in, w_out, gla_w_alpha2, gla_b_alpha, gla_norm,
           gmlp_ln_g, gmlp_ln_b, gmlp_ws, gmlp_bs, rwkv_mu, rwkv_w0, rwkv_w2, rwkv_a0, rwkv_a2, rwkv_g2,
           rwkv_kk, rwkv_ka, rwkv_rk, rwkv_lnx_g, rwkv_lnx_b):
    bp, tp, _ = x_prompt.shape
    bs, ts, _ = x_sample.shape
    depth = norm_gains.shape[0]
    assert tp % GMLP_CHUNK == 0 and tp % (DIL_PAIRS[-1][1] * LANES) == 0 and bp % RWKV_BATCH == 0
    assert ts == 4 and bs % RWKV_BATCH == 0 and (bs * ts) % GMLP_CHUNK == 0
    caches = (cache_win128, cache_win512, cache_win2048)
    lanes = np.arange(GROUP_WIDTH)
    seg_b = jnp.asarray((lanes[:, None] // HEAD_DIM) == (lanes[None, :] // HEAD_DIM), dtype=bf16)
    gla0 = jnp.zeros((bp, N_HEADS, GLA_DK, HEAD_DIM), f32)
    rwkv0 = jnp.zeros((bp, N_HEADS, HEAD_DIM, HEAD_DIM), f32)
    shift0 = jnp.zeros((bp, RWKV_COLS), f32)
    yp = x_prompt.reshape(bp * tp, D_MODEL)
    ys = x_sample.reshape(bs * ts, D_MODEL)
    pg, pr, psh, pw = [], [], [], [[], [], []]
    sg, sr, ssh, sw, sv = [], [], [], [[], [], []], []
    for l in range(depth):
        p = _prep_layer(l, norm_gains, w_ff_gate, w_ff_up, w_ff_down, w_in, w_out, gla_w_alpha2, gla_b_alpha,
                        gla_norm, gmlp_ln_g, gmlp_ln_b, gmlp_ws, gmlp_bs, rwkv_mu, rwkv_w0, rwkv_w2, rwkv_a0,
                        rwkv_a2, rwkv_g2, rwkv_kk, rwkv_ka, rwkv_rk, rwkv_lnx_g, rwkv_lnx_b)
        yp, g_p, r_p, sh_p, _, kv_p = _trunk_layer(yp, p, seg_b, bp, tp, gla0, rwkv0, shift0, None)
        layer_caches = [c[l].reshape(bs, c.shape[2], 2 * GROUP_WIDTH) for c in caches]
        ys, g_s, r_s, sh_s, v_s, kv_s = _trunk_layer(ys, p, seg_b, bs, ts, state_gla[l], state_rwkv[l],
                                                     state_shift[l], layer_caches)
        pg.append(g_p)
        pr.append(r_p)
        psh.append(sh_p)
        sg.append(g_s)
        sr.append(r_s)
        ssh.append(sh_s)
        sv.append(v_s.reshape(bs, ts, GROUP_WIDTH))
        for gi in range(len(DIL_PAIRS)):
            pw[gi].append(kv_p[gi])
            sw[gi].append(kv_s[gi])
    return (yp.reshape(bp, tp, D_MODEL), ys.reshape(bs, ts, D_MODEL),
            jnp.stack(pg), jnp.stack(pr), jnp.stack(psh),
            jnp.stack(pw[0]), jnp.stack(pw[1]), jnp.stack(pw[2]),
            jnp.stack(sg), jnp.stack(sr), jnp.stack(ssh),
            jnp.stack(sw[0]), jnp.stack(sw[1]), jnp.stack(sw[2]),
            jnp.stack(sv))
```

```python
import functools

import jax
import jax.numpy as jnp
import numpy as np
from jax import lax
from jax.experimental import pallas as pl
from jax.experimental.pallas import tpu as pltpu

f32 = jnp.float32
bf16 = jnp.bfloat16

D_MODEL = 1024
HEAD_DIM = 64
N_HEADS = 4
GROUP_WIDTH = N_HEADS * HEAD_DIM
D_FF = 2816
EPS = 1e-6
GLA_DK = 32
GLA_RANK = 16
GLA_TAU = 16.0
GLA_CHUNK = 64
GMLP_CHUNK = 128
LN_EPS = 1e-5
RWKV_W_RANK = 64
RWKV_A_RANK = 64
RWKV_G_RANK = 128
RWKV_LNX_EPS = 64e-5
DIL_PAIRS = ((128, 1), (512, 4), (2048, 16))
N_DIL_HEADS = len(DIL_PAIRS) * N_HEADS
ALIBI_MAX_EXP = 8.0
GLA_COLS = 2 * N_HEADS * GLA_DK + GROUP_WIDTH + GLA_RANK + GROUP_WIDTH
GMLP_COLS = 2 * GROUP_WIDTH
RWKV_COLS = 3 * GROUP_WIDTH + RWKV_W_RANK + RWKV_A_RANK + RWKV_G_RANK
DIL_COLS = len(DIL_PAIRS) * 3 * GROUP_WIDTH

LANES = 128
SUBLANES = 8
VMEM_LIMIT_BYTES = 56 * 1024 * 1024

GLA_COLS_PAD = 896
DIL_SLABS = DIL_COLS // LANES // len(DIL_PAIRS)
FFN_CHUNK = 256
ROW_TILE = 512
GMLP_BLOCK_CHUNKS = 4
GLA_BLOCK_CHUNKS = 8
RWKV_BATCH = 8
RWKV_TBLOCK = 64
RWKV_GROUP = 2
RWKV_UNROLL = 4


def _alibi_slope(head):
    return float(2.0 ** (-ALIBI_MAX_EXP * (head + 1.0) / N_DIL_HEADS))


def _iota(shape, dim):
    return lax.broadcasted_iota(jnp.int32, shape, dim)


def _dot(a, b, precision=None):
    return jnp.dot(a, b, preferred_element_type=f32, precision=precision)


def _dot_tr(a, b, precision=None):
    return lax.dot_general(a, b, (((1,), (1,)), ((), ())), preferred_element_type=f32, precision=precision)


def _dot_tl(a, b, precision=None):
    return lax.dot_general(a, b, (((0,), (0,)), ((), ())), preferred_element_type=f32, precision=precision)


def _rms_rows(x, g):
    ms = jnp.mean(x * x, axis=-1, keepdims=True)
    return x * lax.rsqrt(ms + EPS) * g


def _softplus(x):
    return jnp.maximum(x, 0.0) + jnp.log1p(jnp.exp(-jnp.abs(x)))


def _split(x, n):
    parts = []
    for _ in range(n - 1):
        p = x.astype(bf16)
        parts.append(p)
        x = x - p.astype(f32)
    parts.append(x.astype(bf16))
    return parts


def _split_param(w):
    hi = w.astype(bf16)
    return hi, (w - hi.astype(f32)).astype(bf16)


def _dot_exact(x, m_b, n=2):
    return functools.reduce(jnp.add, [_dot(p, m_b) for p in _split(x, n)])


def _dot_param(x, w_hi, w_lo):
    x_hi, x_lo = _split(x, 2)
    return _dot(x_hi, w_hi) + _dot(x_lo, w_hi) + _dot(x_hi, w_lo)


def _params(*semantics):
    return pltpu.CompilerParams(dimension_semantics=semantics, vmem_limit_bytes=VMEM_LIMIT_BYTES)


def _resident(shape):
    zeros = (0,) * len(shape)
    return pl.BlockSpec(shape, lambda *_: zeros, pipeline_mode=pl.Buffered(1))


def _ffn_body(x_ref, gpre_ref, gpost_ref, wg_ref, wu_ref, wd_ref, o_ref):
    x = x_ref[...]
    xn = _rms_rows(x, gpre_ref[...]).astype(bf16)
    acc = None
    for c in range(D_FF // FFN_CHUNK):
        lo, hi = c * FFN_CHUNK, (c + 1) * FFN_CHUNK
        g = _dot(xn, wg_ref[:, lo:hi])
        u = _dot(xn, wu_ref[:, lo:hi])
        act = (g * jax.nn.sigmoid(g) * u).astype(bf16)
        part = _dot(act, wd_ref[lo:hi, :])
        acc = part if acc is None else acc + part
    o_ref[...] = x + 0.5 * _rms_rows(acc, gpost_ref[...])


def _ffn(x, g_pre, g_post, wg, wu, wd):
    n = x.shape[0]
    tm = min(ROW_TILE, n)
    row = pl.BlockSpec((tm, D_MODEL), lambda i: (i, 0))
    return pl.pallas_call(
        _ffn_body,
        out_shape=jax.ShapeDtypeStruct((n, D_MODEL), f32),
        grid=(n // tm,),
        in_specs=[row, _resident((1, D_MODEL)), _resident((1, D_MODEL)),
                  _resident((D_MODEL, D_FF)), _resident((D_MODEL, D_FF)), _resident((D_FF, D_MODEL))],
        out_specs=row,
        compiler_params=_params("parallel"),
        name="ffn",
    )(x, g_pre, g_post, wg, wu, wd)


def _proj_body(h_ref, g_ref, wgla_ref, wgmlp_ref, wrwkv_ref, wdil_ref, ogla_ref, ogmlp_ref, orwkv_ref, odil_ref):
    xn = _rms_rows(h_ref[...], g_ref[...]).astype(bf16)
    ogla_ref[...] = _dot(xn, wgla_ref[...])
    ogmlp_ref[...] = _dot(xn, wgmlp_ref[...])
    orwkv_ref[...] = _dot(xn, wrwkv_ref[...])
    res = _dot(xn, wdil_ref[...])
    for s in range(DIL_COLS // LANES):
        odil_ref[s // DIL_SLABS, s % DIL_SLABS] = res[:, s * LANES:(s + 1) * LANES]


def _proj(h, g, w_gla, w_gmlp, w_rwkv, w_dil):
    n = h.shape[0]
    tm = min(ROW_TILE, n)
    n_grp = len(DIL_PAIRS)
    return pl.pallas_call(
        _proj_body,
        out_shape=(jax.ShapeDtypeStruct((n, GLA_COLS_PAD), f32),
                   jax.ShapeDtypeStruct((n, GMLP_COLS), f32),
                   jax.ShapeDtypeStruct((n, RWKV_COLS), f32),
                   jax.ShapeDtypeStruct((n_grp, DIL_SLABS, n, LANES), f32)),
        grid=(n // tm,),
        in_specs=[pl.BlockSpec((tm, D_MODEL), lambda i: (i, 0)), _resident((1, D_MODEL)),
                  _resident((D_MODEL, GLA_COLS_PAD)), _resident((D_MODEL, GMLP_COLS)),
                  _resident((D_MODEL, RWKV_COLS)), _resident((D_MODEL, DIL_COLS))],
        out_specs=(pl.BlockSpec((tm, GLA_COLS_PAD), lambda i: (i, 0)),
                   pl.BlockSpec((tm, GMLP_COLS), lambda i: (i, 0)),
                   pl.BlockSpec((tm, RWKV_COLS), lambda i: (i, 0)),
                   pl.BlockSpec((n_grp, DIL_SLABS, tm, LANES), lambda i: (0, 0, i, 0))),
        compiler_params=_params("parallel"),
        name="proj",
    )(h, g, w_gla, w_gmlp, w_rwkv, w_dil)


def _outproj_body(h_ref, ogla_ref, ogmlp_ref, orwkv_ref, odil_ref, w_ref, g_ref, o_ref):
    gw = GROUP_WIDTH
    mix = _dot(ogla_ref[...].astype(bf16), w_ref[0:gw, :])
    mix += _dot(ogmlp_ref[...].astype(bf16), w_ref[gw:2 * gw, :])
    mix += _dot(orwkv_ref[...].astype(bf16), w_ref[2 * gw:3 * gw, :])
    for p in range(gw // LANES):
        mix += _dot(odil_ref[p].astype(bf16), w_ref[3 * gw + p * LANES:3 * gw + (p + 1) * LANES, :])
    o_ref[...] = h_ref[...] + _rms_rows(mix, g_ref[...])


def _outproj(h, o_gla, o_gmlp, o_rwkv, o_dil, w_out, g):
    n = h.shape[0]
    tm = min(ROW_TILE, n)
    row = pl.BlockSpec((tm, D_MODEL), lambda i: (i, 0))
    grp = pl.BlockSpec((tm, GROUP_WIDTH), lambda i: (i, 0))
    return pl.pallas_call(
        _outproj_body,
        out_shape=jax.ShapeDtypeStruct((n, D_MODEL), f32),
        grid=(n // tm,),
        in_specs=[row, grp, grp, grp, pl.BlockSpec((GROUP_WIDTH // LANES, tm, LANES), lambda i: (0, i, 0)),
                  _resident((D_MODEL, D_MODEL)), _resident((1, D_MODEL))],
        out_specs=row,
        compiler_params=_params("parallel"),
        name="outproj",
    )(h, o_gla, o_gmlp, o_rwkv, o_dil, w_out, g)


def _gla_body(c_ref, h0_ref, wa_hi_ref, wa_lo_ref, ba_ref, ng_ref, o_ref, sfin_ref, s_scr, *, n_chunks, valid_rows):
    C = GLA_CHUNK
    hk, hv = N_HEADS * GLA_DK, GROUP_WIDTH
    tb = pl.program_id(1)

    @pl.when(tb == 0)
    def _init():
        s_scr[...] = jnp.zeros((hk, hv), f32)
        for h in range(N_HEADS):
            s_scr[GLA_DK * h:GLA_DK * (h + 1), HEAD_DIM * h:HEAD_DIM * (h + 1)] = h0_ref[0, h]

    rr, cc = _iota((C, C), 0), _iota((C, C), 1)
    causal = rr >= cc
    tril = causal.astype(bf16)
    k_head = _iota((1, hk), 1) >> 5
    v_head = _iota((1, hv), 1) >> 6
    diag_blocks = (_iota((hk, hv), 0) >> 5) == (_iota((hk, hv), 1) >> 6)
    seg_b = ((_iota((hv, hv), 0) >> 6) == (_iota((hv, hv), 1) >> 6)).astype(bf16)
    ones_cv = jnp.ones((C, hv), bf16)

    z = _dot_param(c_ref[0, :, 2 * hk + 2 * hv:], wa_hi_ref[...], wa_lo_ref[...]) + ba_ref[...]
    log_a_all = -_softplus(-z) * (1.0 / GLA_TAU)
    if valid_rows < C:
        log_a_all = jnp.where(_iota(log_a_all.shape, 0) < valid_rows, log_a_all, 0.0)
    stash = []
    for ci in range(n_chunks):
        x = c_ref[0, ci * C:(ci + 1) * C, :]
        q = x[:, 0:hk] * (GLA_DK ** -0.5)
        k = x[:, hk:2 * hk]
        v = x[:, 2 * hk:2 * hk + hv]
        la_parts = _split(log_a_all[ci * C:(ci + 1) * C], 3)
        b = functools.reduce(jnp.add, [_dot(tril, p) for p in la_parts])
        b_last = b[C - 1:C, :]
        qe = q * jnp.exp(b)
        keb = (k * jnp.exp(-b)).astype(bf16)
        kd = k * jnp.exp(b_last - b)
        o_intra = None
        for h in range(N_HEADS):
            att = _dot_tr(jnp.where(k_head == h, qe, 0.0).astype(bf16), keb)
            att = jnp.where(causal, att, 0.0)
            part = _dot(att.astype(bf16), jnp.where(v_head == h, v, 0.0).astype(bf16))
            o_intra = part if o_intra is None else o_intra + part
        incr = jnp.where(diag_blocks, _dot_tl(kd.astype(bf16), v.astype(bf16)), 0.0)
        chunk_log_decay = functools.reduce(jnp.add, [_dot_tl(p, ones_cv) for p in la_parts])
        stash.append((qe.astype(bf16), o_intra, incr, jnp.exp(chunk_log_decay)))

    state = s_scr[...]
    outs = []
    for qe_b, o_intra, incr, decay in stash:
        outs.append(o_intra + _dot(qe_b, state.astype(bf16)))
        state = decay * state + incr
    s_scr[...] = state

    for ci, o in enumerate(outs):
        g = c_ref[0, ci * C:(ci + 1) * C, 2 * hk + hv:2 * hk + 2 * hv]
        ms = _dot_exact(o * o, seg_b) * (1.0 / HEAD_DIM)
        o_ref[0, ci * C:(ci + 1) * C, :] = o * lax.rsqrt(ms + EPS) * ng_ref[...] * (g * jax.nn.sigmoid(g))

    @pl.when(tb == pl.num_programs(1) - 1)
    def _fin():
        for h in range(N_HEADS):
            sfin_ref[0, h] = s_scr[GLA_DK * h:GLA_DK * (h + 1), HEAD_DIM * h:HEAD_DIM * (h + 1)]


def _gla(c, h0, wa_pad, b_alpha, norm_g, valid_rows):
    bsz, t, _ = c.shape
    tb = min(t, GLA_BLOCK_CHUNKS * GLA_CHUNK)
    wa_spec = pl.BlockSpec((LANES, N_HEADS * GLA_DK), lambda b, i: (0, 0))
    return pl.pallas_call(
        functools.partial(_gla_body, n_chunks=tb // GLA_CHUNK, valid_rows=valid_rows),
        out_shape=(jax.ShapeDtypeStruct((bsz, t, GROUP_WIDTH), f32),
                   jax.ShapeDtypeStruct((bsz, N_HEADS, GLA_DK, HEAD_DIM), f32)),
        grid=(bsz, t // tb),
        in_specs=[pl.BlockSpec((1, tb, GLA_COLS_PAD), lambda b, i: (b, i, 0)),
                  pl.BlockSpec((1, N_HEADS, GLA_DK, HEAD_DIM), lambda b, i: (b, 0, 0, 0)),
                  wa_spec, wa_spec,
                  pl.BlockSpec((1, N_HEADS * GLA_DK), lambda b, i: (0, 0)),
                  pl.BlockSpec((1, GROUP_WIDTH), lambda b, i: (0, 0))],
        out_specs=(pl.BlockSpec((1, tb, GROUP_WIDTH), lambda b, i: (b, i, 0)),
                   pl.BlockSpec((1, N_HEADS, GLA_DK, HEAD_DIM), lambda b, i: (b, 0, 0, 0))),
        scratch_shapes=[pltpu.VMEM((N_HEADS * GLA_DK, GROUP_WIDTH), f32)],
        compiler_params=_params("parallel", "arbitrary"),
        name="gla",
    )(c, h0, *wa_pad, b_alpha, norm_g)


def _gmlp_body(c_ref, lng_ref, lnb_ref, ws_ref, bias_ref, o_ref, *v_refs):
    gw = GROUP_WIDTH
    x = c_ref[...]
    ge = 0.5 * x * (1.0 + lax.erf(x * float(np.sqrt(0.5))))
    u, v = ge[:, :gw], ge[:, gw:]
    mean = jnp.mean(v, axis=-1, keepdims=True)
    d = v - mean
    var = jnp.mean(d * d, axis=-1, keepdims=True)
    vn = d * lax.rsqrt(var + LN_EPS) * lng_ref[...] + lnb_ref[...]
    n = GMLP_CHUNK
    causal = _iota((n, n), 0) >= _iota((n, n), 1)
    v_head = _iota((1, gw), 1) >> 6
    w_all = jnp.concatenate([jnp.where(causal, ws_ref[grp], 0.0).astype(bf16) for grp in range(N_HEADS)], axis=0)
    vb = vn.astype(bf16)
    for ci in range(x.shape[0] // n):
        rows = slice(ci * n, (ci + 1) * n)
        res = _dot(w_all, vb[rows])
        s = res[0:n]
        for grp in range(1, N_HEADS):
            s = jnp.where(v_head == grp, res[grp * n:(grp + 1) * n], s)
        o_ref[rows, :] = u[rows] * (s + bias_ref[...])
    if v_refs:
        v_refs[0][...] = vn


def _gmlp(c, ln_g, ln_b, ws, bias_rows, emit_v):
    n = c.shape[0]
    rows = min(n, GMLP_BLOCK_CHUNKS * GMLP_CHUNK)
    blk = pl.BlockSpec((rows, GROUP_WIDTH), lambda i: (i, 0))
    out_shape = [jax.ShapeDtypeStruct((n, GROUP_WIDTH), f32)]
    out_specs = [blk]
    if emit_v:
        out_shape.append(jax.ShapeDtypeStruct((n, GROUP_WIDTH), f32))
        out_specs.append(blk)
    return pl.pallas_call(
        _gmlp_body,
        out_shape=tuple(out_shape),
        grid=(n // rows,),
        in_specs=[pl.BlockSpec((rows, GMLP_COLS), lambda i: (i, 0)),
                  pl.BlockSpec((1, GROUP_WIDTH), lambda i: (0, 0)),
                  pl.BlockSpec((1, GROUP_WIDTH), lambda i: (0, 0)),
                  pl.BlockSpec((N_HEADS, GMLP_CHUNK, GMLP_CHUNK), lambda i: (0, 0, 0)),
                  pl.BlockSpec((GMLP_CHUNK, GROUP_WIDTH), lambda i: (0, 0))],
        out_specs=tuple(out_specs),
        compiler_params=_params("parallel"),
        name="gmlp",
    )(c, ln_g, ln_b, ws, bias_rows)


def _rwkv_body(c_ref, sp_ref, s0_ref, mu_ref, w0_ref, w2h_ref, w2l_ref, a0_ref, a2h_ref, a2l_ref, g2h_ref, g2l_ref,
               kk_ref, ka_ref, rk_ref, lg_ref, lb_ref, seg_ref, o_ref, sfin_ref,
               s_scr, carry_scr, r_scr, w_scr, k_scr, v_scr, a_scr, b_scr, y_scr, g_scr, bonus_scr, sr_scr,
               *, nb, n_steps):
    gw = GROUP_WIDTH
    tbr = c_ref.shape[1]
    tb = pl.program_id(1)
    seg_b = seg_ref[...]

    @pl.when(tb == 0)
    def _init():
        for b in range(nb):
            carry_scr[b] = sp_ref[b]
            for h in range(N_HEADS):
                s_scr[b, :, HEAD_DIM * h:HEAD_DIM * (h + 1)] = s0_ref[b, h]

    first_row = _iota((tbr, RWKV_COLS), 0) == 0
    for b in range(nb):
        c = c_ref[b]
        prev = jnp.where(first_row, carry_scr[b], pltpu.roll(c, 1, 0))
        carry_scr[b] = c[tbr - 1:tbr, :]
        xm = c + (prev - c) * mu_ref[...]
        r = xm[:, 0:gw]
        k = xm[:, gw:2 * gw]
        v = xm[:, 2 * gw:3 * gw]
        wa_lr = xm[:, 3 * gw:3 * gw + LANES]
        g_lr = xm[:, 3 * gw + LANES:]
        w = -_softplus(-(w0_ref[...] + _dot_param(jnp.tanh(wa_lr), w2h_ref[...], w2l_ref[...]))) - 0.5
        a = jax.nn.sigmoid(a0_ref[...] + _dot_param(wa_lr, a2h_ref[...], a2l_ref[...]))
        g = _dot_param(jax.nn.sigmoid(g_lr), g2h_ref[...], g2l_ref[...])
        kk = k * kk_ref[...]
        kk = kk / jnp.maximum(jnp.sqrt(_dot_exact(kk * kk, seg_b)), 1e-12)
        k2 = k * (1.0 + (a - 1.0) * ka_ref[...])
        r_scr[b] = r
        w_scr[b] = jnp.exp(-jnp.exp(w))
        k_scr[b] = k2
        v_scr[b] = v
        a_scr[b] = -kk
        b_scr[b] = kk * a
        g_scr[b] = g
        bonus_scr[b] = _dot_exact(r * k2 * rk_ref[...], seg_b) * v
        y_scr[b] = jnp.zeros((tbr, gw), f32)

    eye = (_iota((HEAD_DIM, gw), 1) & (HEAD_DIM - 1)) == _iota((HEAD_DIM, gw), 0)

    hd = HEAD_DIM

    groups = [list(range(g0, g0 + RWKV_GROUP)) for g0 in range(0, nb, RWKV_GROUP)]

    def feedback_dot(row, batches):
        states, parts = [], []
        for b in batches:
            state = s_scr[b]
            sa_in = state * a_scr[b, row, :]
            hi = sa_in.astype(bf16)
            parts += [hi, (sa_in - hi.astype(f32)).astype(bf16),
                      jnp.where(eye, v_scr[b, row, :], 0.0).astype(bf16)]
            states.append(state)
        return states, _dot(jnp.concatenate(parts, axis=0), seg_b)

    def update(row, batches, states, res):
        for i, b in enumerate(batches):
            base = 3 * hd * i
            sa = res[base:base + hd] + res[base + hd:base + 2 * hd]
            v_col = res[base + 2 * hd:base + 3 * hd]
            state = states[i] * w_scr[b, row, :] + sa * b_scr[b, row, :] + v_col * k_scr[b, row, :]
            s_scr[b] = state
            sr_scr[b] = state * r_scr[b, row, :]

    def readout(row, batches):
        res = _dot(jnp.concatenate([sr_scr[b].astype(bf16) for b in batches], axis=0), seg_b)
        for i, b in enumerate(batches):
            y = res[hd * i:hd * (i + 1)]
            y_scr[b, row, :] = jnp.sum(jnp.where(eye, y, 0.0), axis=0, keepdims=True)

    def step(t, carry):
        row = pl.ds(t, 1)
        prev_row = pl.ds(jnp.maximum(t - 1, 0), 1)
        fed = [feedback_dot(row, g) for g in groups]
        for g in groups:
            readout(prev_row, g)
        for g, f in zip(groups, fed):
            update(row, g, *f)
        return carry

    for b in range(nb):
        sr_scr[b] = jnp.zeros((hd, gw), f32)
    lax.fori_loop(0, n_steps, step, 0, unroll=RWKV_UNROLL)
    for g in groups:
        readout(pl.ds(n_steps - 1, 1), g)

    for b in range(nb):
        y = y_scr[b]
        d = y - _dot_exact(y, seg_b) * (1.0 / HEAD_DIM)
        var = _dot_exact(d * d, seg_b) * (1.0 / HEAD_DIM)
        yn = d * lax.rsqrt(var + RWKV_LNX_EPS) * lg_ref[...] + lb_ref[...]
        o_ref[b] = (yn + bonus_scr[b]) * g_scr[b]

    @pl.when(tb == pl.num_programs(1) - 1)
    def _fin():
        for b in range(nb):
            for h in range(N_HEADS):
                sfin_ref[b, h] = s_scr[b, :, HEAD_DIM * h:HEAD_DIM * (h + 1)]


def _rwkv(c, shift_prev, s0, p, seg_b, n_valid):
    bsz, t, _ = c.shape
    nb = RWKV_BATCH
    tbr = min(t, RWKV_TBLOCK)
    n_steps = min(n_valid, tbr)
    vec = lambda width: pl.BlockSpec((1, width), lambda b, i: (0, 0))
    mat = lambda rows: pl.BlockSpec((rows, GROUP_WIDTH), lambda b, i: (0, 0))
    rows_scr = pltpu.VMEM((nb, tbr, GROUP_WIDTH), f32)
    return pl.pallas_call(
        functools.partial(_rwkv_body, nb=nb, n_steps=n_steps),
        out_shape=(jax.ShapeDtypeStruct((bsz, t, GROUP_WIDTH), f32),
                   jax.ShapeDtypeStruct((bsz, N_HEADS, HEAD_DIM, HEAD_DIM), f32)),
        grid=(bsz // nb, t // tbr),
        in_specs=[pl.BlockSpec((nb, tbr, RWKV_COLS), lambda b, i: (b, i, 0)),
                  pl.BlockSpec((nb, 1, RWKV_COLS), lambda b, i: (b, 0, 0)),
                  pl.BlockSpec((nb, N_HEADS, HEAD_DIM, HEAD_DIM), lambda b, i: (b, 0, 0, 0)),
                  vec(RWKV_COLS), vec(GROUP_WIDTH), mat(LANES), mat(LANES), vec(GROUP_WIDTH), mat(LANES), mat(LANES),
                  mat(RWKV_G_RANK), mat(RWKV_G_RANK),
                  vec(GROUP_WIDTH), vec(GROUP_WIDTH), vec(GROUP_WIDTH), vec(GROUP_WIDTH), vec(GROUP_WIDTH),
                  mat(GROUP_WIDTH)],
        out_specs=(pl.BlockSpec((nb, tbr, GROUP_WIDTH), lambda b, i: (b, i, 0)),
                   pl.BlockSpec((nb, N_HEADS, HEAD_DIM, HEAD_DIM), lambda b, i: (b, 0, 0, 0))),
        scratch_shapes=[pltpu.VMEM((nb, HEAD_DIM, GROUP_WIDTH), f32), pltpu.VMEM((nb, 1, RWKV_COLS), f32)]
        + [rows_scr] * 9 + [pltpu.VMEM((nb, HEAD_DIM, GROUP_WIDTH), f32)],
        compiler_params=_params("parallel", "arbitrary"),
        name="rwkv",
    )(c, shift_prev, s0, p["mu"], p["w0"], *p["w2"], p["a0"], *p["a2"], *p["g2"], p["kk"], p["ka"], p["rk"],
      p["lnx_g"], p["lnx_b"], seg_b)


def _merge_rows(acc, mrun, den, o_new, m_new, d_new):
    m = jnp.maximum(mrun, m_new)
    a_old, a_new = jnp.exp(mrun - m), jnp.exp(m_new - m)
    return acc * a_old + o_new * a_new, m, den * a_old + d_new * a_new


def _dil_prompt_body(x_ref, o_ref, kv0_ref, kv1_ref, kv2_ref, acc_scr, m_scr, d_scr, *, t):
    grp = pl.program_id(1)
    blk = LANES
    n_pairs = GROUP_WIDTH // LANES
    rr, cc = _iota((blk, blk), 0), _iota((blk, blk), 1)
    first_head = _iota((1, LANES), 1) < HEAD_DIM
    ones_b = jnp.ones((blk, LANES), bf16)
    neg_inf = float("-inf")
    kv_refs = (kv0_ref, kv1_ref, kv2_ref)

    def group(gi, window, dilation):
        n_sub = t // dilation // blk
        dist_cur = (rr - cc).astype(f32) * float(dilation)
        dist_prev = (rr + blk - cc).astype(f32) * float(dilation)

        def combo(idx, carry):
            res, sub = idx // n_sub, idx % n_sub
            start = res + dilation * blk * sub
            pstart = jnp.maximum(start - dilation * blk, 0)
            rows = pl.ds(start, blk, stride=dilation) if dilation > 1 else pl.ds(start, blk)
            prows = pl.ds(pstart, blk, stride=dilation) if dilation > 1 else pl.ds(pstart, blk)
            no_prev = jnp.where(sub > 0, 0.0, neg_inf)
            raw = []
            for pair in range(n_pairs):
                q = x_ref[0, pair, rows, :] * (HEAD_DIM ** -0.5)
                kc = x_ref[0, 2 + pair, rows, :].astype(bf16)
                kp = x_ref[0, 2 + pair, prows, :].astype(bf16) if n_sub > 1 else None
                for hh in range(2):
                    lanes = first_head if hh == 0 else jnp.logical_not(first_head)
                    qm = jnp.where(lanes, q, 0.0).astype(bf16)
                    raw.append((_dot_tr(qm, kc), _dot_tr(qm, kp) if n_sub > 1 else None))
            for pair in range(n_pairs):
                vc = jnp.concatenate([x_ref[0, 4 + pair, rows, :].astype(bf16), ones_b], axis=1)
                if n_sub > 1:
                    vp = jnp.concatenate([x_ref[0, 4 + pair, prows, :].astype(bf16), ones_b], axis=1)
                o_pair = m_pair = d_pair = None
                for hh in range(2):
                    slope = _alibi_slope(gi * N_HEADS + 2 * pair + hh)
                    lanes = first_head if hh == 0 else jnp.logical_not(first_head)
                    s_cur, s_prev = raw[2 * pair + hh]
                    s_cur = jnp.where(rr >= cc, s_cur - slope * dist_cur, neg_inf)
                    mx = jnp.max(s_cur, axis=-1, keepdims=True)
                    if n_sub > 1:
                        s_prev = jnp.where(cc >= rr, s_prev - slope * dist_prev, neg_inf) + no_prev
                        mx = jnp.maximum(mx, jnp.max(s_prev, axis=-1, keepdims=True))
                    pv = _dot(jnp.exp(s_cur - mx).astype(bf16), vc)
                    if n_sub > 1:
                        pv += _dot(jnp.exp(s_prev - mx).astype(bf16), vp)
                    o_h, d_h = pv[:, :LANES], pv[:, LANES:]
                    m_h = jnp.broadcast_to(mx, (blk, LANES))
                    if hh == 0:
                        o_pair, m_pair, d_pair = o_h, m_h, d_h
                    else:
                        o_pair = jnp.where(lanes, o_h, o_pair)
                        m_pair = jnp.where(lanes, m_h, m_pair)
                        d_pair = jnp.where(lanes, d_h, d_pair)
                if gi > 0:
                    o_pair, m_pair, d_pair = _merge_rows(acc_scr[pair, rows, :], m_scr[pair, rows, :],
                                                         d_scr[pair, rows, :], o_pair, m_pair, d_pair)
                acc_scr[pair, rows, :] = o_pair
                m_scr[pair, rows, :] = m_pair
                d_scr[pair, rows, :] = d_pair
            return carry

        lax.fori_loop(0, t // blk, combo, 0)

        w = min(window, t)
        for j in range(2 * n_pairs):
            for c in range(w // blk):
                tile = x_ref[0, n_pairs + j, t - w + c * blk:t - w + (c + 1) * blk, :]
                kv_refs[gi][0, j * LANES:(j + 1) * LANES, c * blk:(c + 1) * blk] = tile.T

    for gi, (window, dilation) in enumerate(DIL_PAIRS):
        pl.when(grp == gi)(functools.partial(group, gi, window, dilation))

    @pl.when(grp == len(DIL_PAIRS) - 1)
    def _fin():
        o_ref[...] = acc_scr[...] / d_scr[...]


def _dil_prompt(x, bsz, t):
    n_pairs = GROUP_WIDTH // LANES
    scr = pltpu.VMEM((n_pairs, t, LANES), f32)
    kv_rows = [min(w, t) for w, _ in DIL_PAIRS]
    return pl.pallas_call(
        functools.partial(_dil_prompt_body, t=t),
        out_shape=(jax.ShapeDtypeStruct((n_pairs, bsz * t, LANES), f32),)
        + tuple(jax.ShapeDtypeStruct((bsz, 2 * GROUP_WIDTH, r), f32) for r in kv_rows),
        grid=(bsz, len(DIL_PAIRS)),
        in_specs=[pl.BlockSpec((1, DIL_SLABS, t, LANES), lambda b, g: (g, 0, b, 0))],
        out_specs=(pl.BlockSpec((n_pairs, t, LANES), lambda b, g: (0, b, 0)),)
        + tuple(pl.BlockSpec((1, 2 * GROUP_WIDTH, r), lambda b, g: (b, 0, 0)) for r in kv_rows),
        scratch_shapes=[scr, scr, scr],
        compiler_params=_params("parallel", "arbitrary"),
        name="dil_prompt",
    )(x)


def _dil_sample_body(x_ref, c0_ref, c1_ref, c2_ref, o_ref, *, t_new):
    caches = (c0_ref, c1_ref, c2_ref)
    nq = SUBLANES
    per = nq // t_new
    first_head = _iota((1, LANES), 1) < HEAD_DIM
    neg_inf = float("-inf")
    q_idx = _iota((nq, 1), 0) & (t_new - 1)
    q_bat = _iota((nq, 1), 0) >> 2
    rn, cn = _iota((nq, nq), 0), _iota((nq, nq), 1)
    dn = (rn & (t_new - 1)) - (cn & (t_new - 1))
    same = (rn >> 2) == (cn >> 2)
    for pair in range(GROUP_WIDTH // LANES):
        acc = mrun = den = None
        for gi, (window, dilation) in enumerate(DIL_PAIRS):
            q = x_ref[gi, pair] * (HEAD_DIM ** -0.5)
            kn = x_ref[gi, 2 + pair].astype(bf16)
            vn = x_ref[gi, 4 + pair].astype(bf16)
            dist_c = window + q_idx - _iota((nq, window), 1)
            ok_c = jnp.logical_and((dist_c & (dilation - 1)) == 0, dist_c <= window)
            ok_n = jnp.logical_and(jnp.logical_and(same, dn >= 0), (dn & (dilation - 1)) == 0)
            q2 = jnp.concatenate([jnp.where(first_head, q, 0.0), jnp.where(first_head, 0.0, q)], axis=0).astype(bf16)
            slope2 = jnp.where(_iota((2 * nq, 1), 0) < nq, _alibi_slope(gi * N_HEADS + 2 * pair),
                               _alibi_slope(gi * N_HEADS + 2 * pair + 1))
            dist_c2 = jnp.concatenate([dist_c, dist_c], axis=0).astype(f32)
            ok_c2 = jnp.concatenate([ok_c, ok_c], axis=0)
            dn2 = jnp.concatenate([dn, dn], axis=0).astype(f32)
            ok_n2 = jnp.concatenate([ok_n, ok_n], axis=0)
            s_new = jnp.where(ok_n2, _dot_tr(q2, kn) - slope2 * dn2, neg_inf)
            mx_new = jnp.max(s_new, axis=-1, keepdims=True)
            o_pair = m_pair = d_pair = None
            for s in range(per):
                kt = caches[gi][0, s, pair * LANES:(pair + 1) * LANES, :].astype(bf16)
                vt = caches[gi][0, s, GROUP_WIDTH + pair * LANES:GROUP_WIDTH + (pair + 1) * LANES, :].astype(bf16)
                s_c = jnp.where(ok_c2, _dot(q2, kt) - slope2 * dist_c2, neg_inf)
                mx = jnp.maximum(jnp.max(s_c, axis=-1, keepdims=True), mx_new)
                p_c, p_n = jnp.exp(s_c - mx), jnp.exp(s_new - mx)
                o2 = _dot_tr(p_c.astype(bf16), vt) + _dot(p_n.astype(bf16), vn)
                d2 = jnp.sum(p_c, axis=-1, keepdims=True) + jnp.sum(p_n, axis=-1, keepdims=True)
                o_s = jnp.where(first_head, o2[:nq], o2[nq:])
                m_s = jnp.where(first_head, jnp.broadcast_to(mx[:nq], (nq, LANES)), jnp.broadcast_to(mx[nq:], (nq, LANES)))
                d_s = jnp.where(first_head, jnp.broadcast_to(d2[:nq], (nq, LANES)), jnp.broadcast_to(d2[nq:], (nq, LANES)))
                if s == 0:
                    o_pair, m_pair, d_pair = o_s, m_s, d_s
                else:
                    mine = q_bat == s
                    o_pair = jnp.where(mine, o_s, o_pair)
                    m_pair = jnp.where(mine, m_s, m_pair)
                    d_pair = jnp.where(mine, d_s, d_pair)
            if gi == 0:
                acc, mrun, den = o_pair, m_pair, d_pair
            else:
                acc, mrun, den = _merge_rows(acc, mrun, den, o_pair, m_pair, d_pair)
        o_ref[pair] = acc / den


def _dil_sample(x, caches, layer, bsz, t_new):
    n_pairs = GROUP_WIDTH // LANES
    per = SUBLANES // t_new
    return pl.pallas_call(
        functools.partial(_dil_sample_body, t_new=t_new),
        out_shape=jax.ShapeDtypeStruct((n_pairs, bsz * t_new, LANES), f32),
        grid=(bsz // per,),
        in_specs=[pl.BlockSpec((len(DIL_PAIRS), DIL_SLABS, SUBLANES, LANES), lambda i: (0, 0, i, 0))]
        + [pl.BlockSpec((1, per, 2 * GROUP_WIDTH, w), lambda i: (layer, i, 0, 0)) for w, _ in DIL_PAIRS],
        out_specs=pl.BlockSpec((n_pairs, SUBLANES, LANES), lambda i: (0, i, 0)),
        compiler_params=_params("parallel"),
        name="dil_sample",
    )(x, *caches)


def _row(v):
    return v.reshape(1, -1)


def _prep_layer(l, norm_gains, w_ff_gate, w_ff_up, w_ff_down, w_in, w_out, gla_w_alpha2, gla_b_alpha, gla_norm,
                gmlp_ln_g, gmlp_ln_b, gmlp_ws, gmlp_bs, rwkv_mu, rwkv_w0, rwkv_w2, rwkv_a0, rwkv_a2, rwkv_g2,
                rwkv_kk, rwkv_ka, rwkv_rk, rwkv_lnx_g, rwkv_lnx_b):
    win = w_in[l]
    o1, o2, o3 = GLA_COLS, GLA_COLS + GMLP_COLS, GLA_COLS + GMLP_COLS + RWKV_COLS
    hk = N_HEADS * GLA_DK
    a_lo, a_hi = 2 * hk + GROUP_WIDTH, 2 * hk + GROUP_WIDTH + GLA_RANK
    w_gla = win[:, :o1]
    w_gla = jnp.concatenate([w_gla[:, :a_lo], w_gla[:, a_hi:], w_gla[:, a_lo:a_hi],
                             jnp.zeros((D_MODEL, GLA_COLS_PAD - GLA_COLS), f32)], axis=1)
    zeros_r = jnp.zeros((RWKV_W_RANK, GROUP_WIDTH), f32)
    return dict(
        norms=[_row(norm_gains[l, i]) for i in range(norm_gains.shape[1])],
        ff_gate=[w_ff_gate[l, i].astype(bf16) for i in range(2)],
        ff_up=[w_ff_up[l, i].astype(bf16) for i in range(2)],
        ff_down=[w_ff_down[l, i].astype(bf16) for i in range(2)],
        w_gla=w_gla.astype(bf16), w_gmlp=win[:, o1:o2].astype(bf16), w_rwkv=win[:, o2:o3].astype(bf16),
        w_dil=win[:, o3:].astype(bf16), w_out=w_out[l].astype(bf16),
        gla_wa=_split_param(jnp.concatenate([gla_w_alpha2[l], jnp.zeros((LANES - GLA_RANK, hk), f32)], axis=0)),
        gla_ba=_row(gla_b_alpha[l]), gla_norm=_row(gla_norm[l]),
        gmlp_ln_g=_row(gmlp_ln_g[l]), gmlp_ln_b=_row(gmlp_ln_b[l]), gmlp_ws=gmlp_ws[l], gmlp_bs=gmlp_bs[l],
        rwkv=dict(mu=_row(rwkv_mu[l]), w0=_row(rwkv_w0[l]),
                  w2=_split_param(jnp.concatenate([rwkv_w2[l], zeros_r], axis=0)), a0=_row(rwkv_a0[l]),
                  a2=_split_param(jnp.concatenate([zeros_r, rwkv_a2[l]], axis=0)), g2=_split_param(rwkv_g2[l]),
                  kk=_row(rwkv_kk[l]), ka=_row(rwkv_ka[l]), rk=_row(rwkv_rk[l]),
                  lnx_g=_row(rwkv_lnx_g[l]), lnx_b=_row(rwkv_lnx_b[l])),
    )


def _kv_rows(c_dil, bsz, t, rows):
    out = []
    for gi in range(len(DIL_PAIRS)):
        kv = c_dil[gi, 2:].reshape(2, GROUP_WIDTH // LANES, bsz, t, LANES)[:, :, :, t - rows[gi]:]
        kv = jnp.transpose(kv, (2, 3, 0, 1, 4))
        out.append(kv.reshape(bsz, rows[gi], 2, N_HEADS, HEAD_DIM))
    return out


def _channel_major_kv(kv):
    bsz, _, rows = kv.shape
    return jnp.transpose(kv.reshape(bsz, 2, N_HEADS, HEAD_DIM, rows), (0, 4, 1, 2, 3))


def _trunk_layer(x, p, seg_b, bsz, t, gla_h0, rwkv_s0, shift0, caches, layer):
    ng = p["norms"]
    n = bsz * t
    h = _ffn(x, ng[0], ng[1], p["ff_gate"][0], p["ff_up"][0], p["ff_down"][0])
    c_gla, c_gmlp, c_rwkv, c_dil = _proj(h, ng[2], p["w_gla"], p["w_gmlp"], p["w_rwkv"], p["w_dil"])
    is_prompt = caches is None

    t_gla = max(t, GLA_CHUNK)
    cg = c_gla.reshape(bsz, t, GLA_COLS_PAD)
    if t_gla != t:
        cg = jnp.pad(cg, ((0, 0), (0, t_gla - t), (0, 0)))
    o_gla, gla_s = _gla(cg, gla_h0, p["gla_wa"], p["gla_ba"], p["gla_norm"], min(t, GLA_CHUNK))
    o_gla = o_gla[:, :t].reshape(n, GROUP_WIDTH)

    if t >= GMLP_CHUNK:
        ws = p["gmlp_ws"]
        bias_rows = jnp.repeat(p["gmlp_bs"].T, HEAD_DIM, axis=1)
    else:
        per = GMLP_CHUNK // t
        eye = jnp.eye(per, dtype=f32)
        ws = jnp.einsum("ab,gij->gaibj", eye, p["gmlp_ws"][:, :t, :t]).reshape(N_HEADS, GMLP_CHUNK, GMLP_CHUNK)
        bias_rows = jnp.tile(jnp.repeat(p["gmlp_bs"][:, :t].T, HEAD_DIM, axis=1), (per, 1))
    gm = _gmlp(c_gmlp, p["gmlp_ln_g"], p["gmlp_ln_b"], ws, bias_rows, emit_v=not is_prompt)
    o_gmlp = gm[0]
    gmlp_v = None if is_prompt else gm[1]

    t_rw = max(t, SUBLANES)
    cr = c_rwkv.reshape(bsz, t, RWKV_COLS)
    shift = cr[:, -1]
    if t_rw != t:
        cr = jnp.pad(cr, ((0, 0), (0, t_rw - t), (0, 0)))
    o_rwkv, rwkv_s = _rwkv(cr, shift0.reshape(bsz, 1, RWKV_COLS), rwkv_s0, p["rwkv"], seg_b, t)
    o_rwkv = o_rwkv[:, :t].reshape(n, GROUP_WIDTH)

    if is_prompt:
        o_dil, *kv_cm = _dil_prompt(c_dil, bsz, t)
        kv_new = [_channel_major_kv(kv) for kv in kv_cm]
    else:
        o_dil = _dil_sample(c_dil, caches, layer, bsz, t)
        kv_new = _kv_rows(c_dil, bsz, t, [min(w, t) for w, _ in DIL_PAIRS])

    h = _outproj(h, o_gla, o_gmlp, o_rwkv, o_dil, p["w_out"], ng[3])
    y = _ffn(h, ng[4], ng[5], p["ff_gate"][1], p["ff_up"][1], p["ff_down"][1])
    return y, gla_s, rwkv_s, shift, gmlp_v, kv_new


def kernel(x_prompt, x_sample, cache_win128, cache_win512, cache_win2048, state_gla, state_rwkv, state_shift,
           norm_gains, w_ff_gate, w_ff_up, w_ff_down, w_in, w_out, gla_w_alpha2, gla_b_alpha, gla_norm,
           gmlp_ln_g, gmlp_ln_b, gmlp_ws, gmlp_bs, rwkv_mu, rwkv_w0, rwkv_w2, rwkv_a0, rwkv_a2, rwkv_g2,
           rwkv_kk, rwkv_ka, rwkv_rk, rwkv_lnx_g, rwkv_lnx_b):
    bp, tp, _ = x_prompt.shape
    bs, ts, _ = x_sample.shape
    depth = norm_gains.shape[0]
    assert tp % GMLP_CHUNK == 0 and tp % (DIL_PAIRS[-1][1] * LANES) == 0 and bp % RWKV_BATCH == 0
    assert ts == 4 and bs % RWKV_BATCH == 0 and (bs * ts) % GMLP_CHUNK == 0
    caches = [jnp.transpose(c, (0, 1, 3, 4, 5, 2)).reshape(depth, bs, 2 * GROUP_WIDTH, c.shape[2])
              for c in (cache_win128, cache_win512, cache_win2048)]
    lanes = np.arange(GROUP_WIDTH)
    seg_b = jnp.asarray((lanes[:, None] // HEAD_DIM) == (lanes[None, :] // HEAD_DIM), dtype=bf16)
    gla0 = jnp.zeros((bp, N_HEADS, GLA_DK, HEAD_DIM), f32)
    rwkv0 = jnp.zeros((bp, N_HEADS, HEAD_DIM, HEAD_DIM), f32)
    shift0 = jnp.zeros((bp, RWKV_COLS), f32)
    yp = x_prompt.reshape(bp * tp, D_MODEL)
    ys = x_sample.reshape(bs * ts, D_MODEL)
    pg, pr, psh, pw = [], [], [], [[], [], []]
    sg, sr, ssh, sw, sv = [], [], [], [[], [], []], []
    for l in range(depth):
        p = _prep_layer(l, norm_gains, w_ff_gate, w_ff_up, w_ff_down, w_in, w_out, gla_w_alpha2, gla_b_alpha,
                        gla_norm, gmlp_ln_g, gmlp_ln_b, gmlp_ws, gmlp_bs, rwkv_mu, rwkv_w0, rwkv_w2, rwkv_a0,
                        rwkv_a2, rwkv_g2, rwkv_kk, rwkv_ka, rwkv_rk, rwkv_lnx_g, rwkv_lnx_b)
        yp, g_p, r_p, sh_p, _, kv_p = _trunk_layer(yp, p, seg_b, bp, tp, gla0, rwkv0, shift0, None, l)
        ys, g_s, r_s, sh_s, v_s, kv_s = _trunk_layer(ys, p, seg_b, bs, ts, state_gla[l], state_rwkv[l],
                                                     state_shift[l], caches, l)
        pg.append(g_p)
        pr.append(r_p)
        psh.append(sh_p)
        sg.append(g_s)
        sr.append(r_s)
        ssh.append(sh_s)
        sv.append(v_s.reshape(bs, ts, GROUP_WIDTH))
        for gi in range(len(DIL_PAIRS)):
            pw[gi].append(kv_p[gi])
            sw[gi].append(kv_s[gi])
    return (yp.reshape(bp, tp, D_MODEL), ys.reshape(bs, ts, D_MODEL),
            jnp.stack(pg), jnp.stack(pr), jnp.stack(psh),
            jnp.stack(pw[0]), jnp.stack(pw[1]), jnp.stack(pw[2]),
            jnp.stack(sg), jnp.stack(sr), jnp.stack(ssh),
            jnp.stack(sw[0]), jnp.stack(sw[1]), jnp.stack(sw[2]),
            jnp.stack(sv))
```

```python
import functools

import jax
import jax.numpy as jnp
import numpy as np
from jax import lax
from jax.experimental import pallas as pl
from jax.experimental.pallas import tpu as pltpu

f32 = jnp.float32
bf16 = jnp.bfloat16

D_MODEL = 1024
HEAD_DIM = 64
N_HEADS = 4
GROUP_WIDTH = N_HEADS * HEAD_DIM
D_FF = 2816
EPS = 1e-6
GLA_DK = 32
GLA_RANK = 16
GLA_TAU = 16.0
GLA_CHUNK = 64
GMLP_CHUNK = 128
LN_EPS = 1e-5
RWKV_W_RANK = 64
RWKV_A_RANK = 64
RWKV_G_RANK = 128
RWKV_LNX_EPS = 64e-5
DIL_PAIRS = ((128, 1), (512, 4), (2048, 16))
N_DIL_HEADS = len(DIL_PAIRS) * N_HEADS
ALIBI_MAX_EXP = 8.0
GLA_COLS = 2 * N_HEADS * GLA_DK + GROUP_WIDTH + GLA_RANK + GROUP_WIDTH
GMLP_COLS = 2 * GROUP_WIDTH
RWKV_COLS = 3 * GROUP_WIDTH + RWKV_W_RANK + RWKV_A_RANK + RWKV_G_RANK
DIL_COLS = len(DIL_PAIRS) * 3 * GROUP_WIDTH

LANES = 128
SUBLANES = 8
VMEM_LIMIT_BYTES = 56 * 1024 * 1024

GLA_COLS_PAD = 896
DIL_SLABS = DIL_COLS // LANES // len(DIL_PAIRS)
FFN_CHUNK = 256
ROW_TILE = 512
GMLP_BLOCK_CHUNKS = 4
DIL_COMBOS = 2
GLA_BLOCK_CHUNKS = 8
RWKV_BATCH = 8
RWKV_TBLOCK = 128
RWKV_GROUP = 2
RWKV_UNROLL = 4


def _alibi_slope(head):
    return float(2.0 ** (-ALIBI_MAX_EXP * (head + 1.0) / N_DIL_HEADS))


def _iota(shape, dim):
    return lax.broadcasted_iota(jnp.int32, shape, dim)


def _dot(a, b, precision=None):
    return jnp.dot(a, b, preferred_element_type=f32, precision=precision)


def _dot_tr(a, b, precision=None):
    return lax.dot_general(a, b, (((1,), (1,)), ((), ())), preferred_element_type=f32, precision=precision)


def _dot_tl(a, b, precision=None):
    return lax.dot_general(a, b, (((0,), (0,)), ((), ())), preferred_element_type=f32, precision=precision)


def _rms_rows(x, g):
    ms = jnp.mean(x * x, axis=-1, keepdims=True)
    return x * lax.rsqrt(ms + EPS) * g


def _softplus(x):
    return jnp.maximum(x, 0.0) + jnp.log1p(jnp.exp(-jnp.abs(x)))


def _split(x, n):
    parts = []
    for _ in range(n - 1):
        p = x.astype(bf16)
        parts.append(p)
        x = x - p.astype(f32)
    parts.append(x.astype(bf16))
    return parts


def _split_param(w):
    hi = w.astype(bf16)
    return hi, (w - hi.astype(f32)).astype(bf16)


def _dot_exact(x, m_b, n=2):
    return functools.reduce(jnp.add, [_dot(p, m_b) for p in _split(x, n)])


def _dot_param(x, w_hi, w_lo):
    x_hi, x_lo = _split(x, 2)
    return _dot(x_hi, w_hi) + _dot(x_lo, w_hi) + _dot(x_hi, w_lo)


def _params(*semantics):
    return pltpu.CompilerParams(dimension_semantics=semantics, vmem_limit_bytes=VMEM_LIMIT_BYTES)


def _resident(shape):
    zeros = (0,) * len(shape)
    return pl.BlockSpec(shape, lambda *_: zeros, pipeline_mode=pl.Buffered(1))


def _ffn_body(*refs, with_mix):
    if with_mix:
        (h_ref, ogla_ref, ogmlp_ref, orwkv_ref, odil_ref, wout_ref, gmix_ref,
         gpre_ref, gpost_ref, wg_ref, wu_ref, wd_ref, o_ref) = refs
        gw = GROUP_WIDTH
        mix = _dot(ogla_ref[...].astype(bf16), wout_ref[0, 0:gw, :])
        mix += _dot(ogmlp_ref[...].astype(bf16), wout_ref[0, gw:2 * gw, :])
        mix += _dot(orwkv_ref[...].astype(bf16), wout_ref[0, 2 * gw:3 * gw, :])
        for p in range(gw // LANES):
            mix += _dot(odil_ref[p].astype(bf16), wout_ref[0, 3 * gw + p * LANES:3 * gw + (p + 1) * LANES, :])
        x = h_ref[...] + _rms_rows(mix, gmix_ref[...])
    else:
        x_ref, gpre_ref, gpost_ref, wg_ref, wu_ref, wd_ref, o_ref = refs
        x = x_ref[...]
    xn = _rms_rows(x, gpre_ref[...]).astype(bf16)
    acc = None
    for c in range(D_FF // FFN_CHUNK):
        lo, hi = c * FFN_CHUNK, (c + 1) * FFN_CHUNK
        g = _dot(xn, wg_ref[0, 0, :, lo:hi])
        u = _dot(xn, wu_ref[0, 0, :, lo:hi])
        act = (g * jax.nn.sigmoid(g) * u).astype(bf16)
        part = _dot(act, wd_ref[0, 0, lo:hi, :])
        acc = part if acc is None else acc + part
    o_ref[...] = x + 0.5 * _rms_rows(acc, gpost_ref[...])


def _ffn(x, g_pre, g_post, wg, wu, wd, layer, idx, mix=None):
    n = x.shape[0]
    tm = min(ROW_TILE, n)
    row = pl.BlockSpec((tm, D_MODEL), lambda i: (i, 0))
    once = dict(pipeline_mode=pl.Buffered(1))
    w_specs = [pl.BlockSpec((1, 1, D_MODEL, D_FF), lambda i: (layer, idx, 0, 0), **once),
               pl.BlockSpec((1, 1, D_MODEL, D_FF), lambda i: (layer, idx, 0, 0), **once),
               pl.BlockSpec((1, 1, D_FF, D_MODEL), lambda i: (layer, idx, 0, 0), **once)]
    in_specs, args = [row], [x]
    if mix is not None:
        o_gla, o_gmlp, o_rwkv, o_dil, w_out, g_mix = mix
        grp = pl.BlockSpec((tm, GROUP_WIDTH), lambda i: (i, 0))
        in_specs += [grp, grp, grp, pl.BlockSpec((GROUP_WIDTH // LANES, tm, LANES), lambda i: (0, i, 0)),
                     pl.BlockSpec((1, D_MODEL, D_MODEL), lambda i: (layer, 0, 0), **once), _resident((1, D_MODEL))]
        args += [o_gla, o_gmlp, o_rwkv, o_dil, w_out, g_mix]
    return pl.pallas_call(
        functools.partial(_ffn_body, with_mix=mix is not None),
        out_shape=jax.ShapeDtypeStruct((n, D_MODEL), f32),
        grid=(n // tm,),
        in_specs=in_specs + [_resident((1, D_MODEL)), _resident((1, D_MODEL))] + w_specs,
        out_specs=row,
        compiler_params=_params("parallel"),
        name="ffn_mix" if mix is not None else "ffn",
    )(*args, g_pre, g_post, wg, wu, wd)


def _proj_body(h_ref, g_ref, wgla_ref, wgmlp_ref, wrwkv_ref, wdil_ref, ogla_ref, ogmlp_ref, orwkv_ref, odil_ref):
    xn = _rms_rows(h_ref[...], g_ref[...]).astype(bf16)
    ogla_ref[...] = _dot(xn, wgla_ref[...])
    ogmlp_ref[...] = _dot(xn, wgmlp_ref[...])
    orwkv_ref[...] = _dot(xn, wrwkv_ref[...])
    res = _dot(xn, wdil_ref[...])
    for s in range(DIL_COLS // LANES):
        odil_ref[s // DIL_SLABS, s % DIL_SLABS] = res[:, s * LANES:(s + 1) * LANES]


def _proj(h, g, w_gla, w_gmlp, w_rwkv, w_dil):
    n = h.shape[0]
    tm = min(ROW_TILE, n)
    n_grp = len(DIL_PAIRS)
    return pl.pallas_call(
        _proj_body,
        out_shape=(jax.ShapeDtypeStruct((n, GLA_COLS_PAD), f32),
                   jax.ShapeDtypeStruct((n, GMLP_COLS), f32),
                   jax.ShapeDtypeStruct((n, RWKV_COLS), f32),
                   jax.ShapeDtypeStruct((n_grp, DIL_SLABS, n, LANES), f32)),
        grid=(n // tm,),
        in_specs=[pl.BlockSpec((tm, D_MODEL), lambda i: (i, 0)), _resident((1, D_MODEL)),
                  _resident((D_MODEL, GLA_COLS_PAD)), _resident((D_MODEL, GMLP_COLS)),
                  _resident((D_MODEL, RWKV_COLS)), _resident((D_MODEL, DIL_COLS))],
        out_specs=(pl.BlockSpec((tm, GLA_COLS_PAD), lambda i: (i, 0)),
                   pl.BlockSpec((tm, GMLP_COLS), lambda i: (i, 0)),
                   pl.BlockSpec((tm, RWKV_COLS), lambda i: (i, 0)),
                   pl.BlockSpec((n_grp, DIL_SLABS, tm, LANES), lambda i: (0, 0, i, 0))),
        compiler_params=_params("parallel"),
        name="proj",
    )(h, g, w_gla, w_gmlp, w_rwkv, w_dil)


def _gla_body(c_ref, h0_ref, wa_hi_ref, wa_lo_ref, ba_ref, ng_ref, o_ref, sfin_ref, s_scr, *, n_chunks, valid_rows):
    C = GLA_CHUNK
    hk, hv = N_HEADS * GLA_DK, GROUP_WIDTH
    tb = pl.program_id(1)

    @pl.when(tb == 0)
    def _init():
        s_scr[...] = jnp.zeros((hk, hv), f32)
        for h in range(N_HEADS):
            s_scr[GLA_DK * h:GLA_DK * (h + 1), HEAD_DIM * h:HEAD_DIM * (h + 1)] = h0_ref[0, h]

    rr, cc = _iota((C, C), 0), _iota((C, C), 1)
    causal = rr >= cc
    tril = causal.astype(bf16)
    k_head = _iota((1, hk), 1) >> 5
    v_head = _iota((1, hv), 1) >> 6
    diag_blocks = (_iota((hk, hv), 0) >> 5) == (_iota((hk, hv), 1) >> 6)
    seg_b = ((_iota((hv, hv), 0) >> 6) == (_iota((hv, hv), 1) >> 6)).astype(bf16)
    ones_cv = jnp.ones((C, hv), bf16)

    z = _dot_param(c_ref[0, :, 2 * hk + 2 * hv:], wa_hi_ref[...], wa_lo_ref[...]) + ba_ref[...]
    log_a_all = -_softplus(-z) * (1.0 / GLA_TAU)
    if valid_rows < C:
        log_a_all = jnp.where(_iota(log_a_all.shape, 0) < valid_rows, log_a_all, 0.0)
    stash = []
    for ci in range(n_chunks):
        x = c_ref[0, ci * C:(ci + 1) * C, :]
        q = x[:, 0:hk] * (GLA_DK ** -0.5)
        k = x[:, hk:2 * hk]
        v = x[:, 2 * hk:2 * hk + hv]
        la_parts = _split(log_a_all[ci * C:(ci + 1) * C], 3)
        b = functools.reduce(jnp.add, [_dot(tril, p) for p in la_parts])
        b_last = b[C - 1:C, :]
        qe = q * jnp.exp(b)
        keb = (k * jnp.exp(-b)).astype(bf16)
        kd = k * jnp.exp(b_last - b)
        o_intra = None
        for h in range(N_HEADS):
            att = _dot_tr(jnp.where(k_head == h, qe, 0.0).astype(bf16), keb)
            att = jnp.where(causal, att, 0.0)
            part = _dot(att.astype(bf16), jnp.where(v_head == h, v, 0.0).astype(bf16))
            o_intra = part if o_intra is None else o_intra + part
        incr = jnp.where(diag_blocks, _dot_tl(kd.astype(bf16), v.astype(bf16)), 0.0)
        chunk_log_decay = functools.reduce(jnp.add, [_dot_tl(p, ones_cv) for p in la_parts])
        stash.append((qe.astype(bf16), o_intra, incr, jnp.exp(chunk_log_decay)))

    state = s_scr[...]
    outs = []
    for qe_b, o_intra, incr, decay in stash:
        outs.append(o_intra + _dot(qe_b, state.astype(bf16)))
        state = decay * state + incr
    s_scr[...] = state

    for ci, o in enumerate(outs):
        g = c_ref[0, ci * C:(ci + 1) * C, 2 * hk + hv:2 * hk + 2 * hv]
        ms = _dot_exact(o * o, seg_b) * (1.0 / HEAD_DIM)
        o_ref[0, ci * C:(ci + 1) * C, :] = o * lax.rsqrt(ms + EPS) * ng_ref[...] * (g * jax.nn.sigmoid(g))

    @pl.when(tb == pl.num_programs(1) - 1)
    def _fin():
        for h in range(N_HEADS):
            sfin_ref[0, h] = s_scr[GLA_DK * h:GLA_DK * (h + 1), HEAD_DIM * h:HEAD_DIM * (h + 1)]


def _gla(c, h0, wa_pad, b_alpha, norm_g, valid_rows):
    bsz, t, _ = c.shape
    tb = min(t, GLA_BLOCK_CHUNKS * GLA_CHUNK)
    wa_spec = pl.BlockSpec((LANES, N_HEADS * GLA_DK), lambda b, i: (0, 0))
    return pl.pallas_call(
        functools.partial(_gla_body, n_chunks=tb // GLA_CHUNK, valid_rows=valid_rows),
        out_shape=(jax.ShapeDtypeStruct((bsz, t, GROUP_WIDTH), f32),
                   jax.ShapeDtypeStruct((bsz, N_HEADS, GLA_DK, HEAD_DIM), f32)),
        grid=(bsz, t // tb),
        in_specs=[pl.BlockSpec((1, tb, GLA_COLS_PAD), lambda b, i: (b, i, 0)),
                  pl.BlockSpec((1, N_HEADS, GLA_DK, HEAD_DIM), lambda b, i: (b, 0, 0, 0)),
                  wa_spec, wa_spec,
                  pl.BlockSpec((1, N_HEADS * GLA_DK), lambda b, i: (0, 0)),
                  pl.BlockSpec((1, GROUP_WIDTH), lambda b, i: (0, 0))],
        out_specs=(pl.BlockSpec((1, tb, GROUP_WIDTH), lambda b, i: (b, i, 0)),
                   pl.BlockSpec((1, N_HEADS, GLA_DK, HEAD_DIM), lambda b, i: (b, 0, 0, 0))),
        scratch_shapes=[pltpu.VMEM((N_HEADS * GLA_DK, GROUP_WIDTH), f32)],
        compiler_params=_params("parallel", "arbitrary"),
        name="gla",
    )(c, h0, *wa_pad, b_alpha, norm_g)


def _gmlp_body(c_ref, lng_ref, lnb_ref, ws_ref, bias_ref, o_ref, *v_refs):
    gw = GROUP_WIDTH
    x = c_ref[...]
    ge = 0.5 * x * (1.0 + lax.erf(x * float(np.sqrt(0.5))))
    u, v = ge[:, :gw], ge[:, gw:]
    mean = jnp.mean(v, axis=-1, keepdims=True)
    d = v - mean
    var = jnp.mean(d * d, axis=-1, keepdims=True)
    vn = d * lax.rsqrt(var + LN_EPS) * lng_ref[...] + lnb_ref[...]
    n = GMLP_CHUNK
    causal = _iota((n, n), 0) >= _iota((n, n), 1)
    v_head = _iota((1, gw), 1) >> 6
    w_all = jnp.concatenate([jnp.where(causal, ws_ref[grp], 0.0).astype(bf16) for grp in range(N_HEADS)], axis=0)
    vb = vn.astype(bf16)
    for ci in range(x.shape[0] // n):
        rows = slice(ci * n, (ci + 1) * n)
        res = _dot(w_all, vb[rows])
        s = res[0:n]
        for grp in range(1, N_HEADS):
            s = jnp.where(v_head == grp, res[grp * n:(grp + 1) * n], s)
        o_ref[rows, :] = u[rows] * (s + bias_ref[...])
    if v_refs:
        v_refs[0][...] = vn


def _gmlp(c, ln_g, ln_b, ws, bias_rows, emit_v):
    n = c.shape[0]
    rows = min(n, GMLP_BLOCK_CHUNKS * GMLP_CHUNK)
    blk = pl.BlockSpec((rows, GROUP_WIDTH), lambda i: (i, 0))
    out_shape = [jax.ShapeDtypeStruct((n, GROUP_WIDTH), f32)]
    out_specs = [blk]
    if emit_v:
        out_shape.append(jax.ShapeDtypeStruct((n, GROUP_WIDTH), f32))
        out_specs.append(blk)
    return pl.pallas_call(
        _gmlp_body,
        out_shape=tuple(out_shape),
        grid=(n // rows,),
        in_specs=[pl.BlockSpec((rows, GMLP_COLS), lambda i: (i, 0)),
                  pl.BlockSpec((1, GROUP_WIDTH), lambda i: (0, 0)),
                  pl.BlockSpec((1, GROUP_WIDTH), lambda i: (0, 0)),
                  pl.BlockSpec((N_HEADS, GMLP_CHUNK, GMLP_CHUNK), lambda i: (0, 0, 0)),
                  pl.BlockSpec((GMLP_CHUNK, GROUP_WIDTH), lambda i: (0, 0))],
        out_specs=tuple(out_specs),
        compiler_params=_params("parallel"),
        name="gmlp",
    )(c, ln_g, ln_b, ws, bias_rows)


def _rwkv_body(c_ref, sp_ref, s0_ref, mu_ref, w0_ref, w2h_ref, w2l_ref, a0_ref, a2h_ref, a2l_ref, g2h_ref, g2l_ref,
               kk_ref, ka_ref, rk_ref, lg_ref, lb_ref, seg_ref, o_ref, sfin_ref,
               s_scr, carry_scr, r_scr, w_scr, k_scr, v_scr, a_scr, b_scr, y_scr, g_scr, bonus_scr, sr_scr,
               *, nb, n_steps):
    gw = GROUP_WIDTH
    tbr = c_ref.shape[1]
    tb = pl.program_id(1)
    seg_b = seg_ref[...]

    @pl.when(tb == 0)
    def _init():
        for b in range(nb):
            carry_scr[b] = sp_ref[b]
            for h in range(N_HEADS):
                s_scr[b, :, HEAD_DIM * h:HEAD_DIM * (h + 1)] = s0_ref[b, h]

    first_row = _iota((tbr, RWKV_COLS), 0) == 0
    mixed = []
    for b in range(nb):
        c = c_ref[b]
        prev = jnp.where(first_row, carry_scr[b], pltpu.roll(c, 1, 0))
        carry_scr[b] = c[tbr - 1:tbr, :]
        mixed.append(c + (prev - c) * mu_ref[...])
    xm = jnp.concatenate(mixed, axis=0)
    r = xm[:, 0:gw]
    k = xm[:, gw:2 * gw]
    v = xm[:, 2 * gw:3 * gw]
    wa_lr = xm[:, 3 * gw:3 * gw + LANES]
    g_lr = xm[:, 3 * gw + LANES:]
    w = -_softplus(-(w0_ref[...] + _dot_param(jnp.tanh(wa_lr), w2h_ref[...], w2l_ref[...]))) - 0.5
    a = jax.nn.sigmoid(a0_ref[...] + _dot_param(wa_lr, a2h_ref[...], a2l_ref[...]))
    kk = k * kk_ref[...]
    kk = kk / jnp.maximum(jnp.sqrt(_dot_exact(kk * kk, seg_b)), 1e-12)
    k2 = k * (1.0 + (a - 1.0) * ka_ref[...])
    per_batch = (nb, tbr, gw)
    r_scr[...] = r.reshape(per_batch)
    w_scr[...] = jnp.exp(-jnp.exp(w)).reshape(per_batch)
    k_scr[...] = k2.reshape(per_batch)
    v_scr[...] = v.reshape(per_batch)
    a_scr[...] = (-kk).reshape(per_batch)
    b_scr[...] = (kk * a).reshape(per_batch)
    g_scr[...] = _dot_param(jax.nn.sigmoid(g_lr), g2h_ref[...], g2l_ref[...]).reshape(per_batch)
    bonus_scr[...] = (_dot_exact(r * k2 * rk_ref[...], seg_b) * v).reshape(per_batch)
    y_scr[...] = jnp.zeros(per_batch, f32)

    eye = (_iota((HEAD_DIM, gw), 1) & (HEAD_DIM - 1)) == _iota((HEAD_DIM, gw), 0)

    hd = HEAD_DIM

    groups = [list(range(g0, g0 + RWKV_GROUP)) for g0 in range(0, nb, RWKV_GROUP)]

    def feedback_dot(row, batches):
        states, parts = [], []
        for b in batches:
            state = s_scr[b]
            sa_in = state * a_scr[b, row, :]
            parts += [sa_in.astype(bf16), jnp.where(eye, v_scr[b, row, :], 0.0).astype(bf16)]
            states.append(state)
        return states, _dot(jnp.concatenate(parts, axis=0), seg_b)

    def update(row, batches, states, res):
        for i, b in enumerate(batches):
            base = 2 * hd * i
            sa = res[base:base + hd]
            v_col = res[base + hd:base + 2 * hd]
            state = states[i] * w_scr[b, row, :] + sa * b_scr[b, row, :] + v_col * k_scr[b, row, :]
            s_scr[b] = state
            sr_scr[b] = state * r_scr[b, row, :]

    def readout(row, batches):
        res = _dot(jnp.concatenate([sr_scr[b].astype(bf16) for b in batches], axis=0), seg_b)
        for i, b in enumerate(batches):
            y = res[hd * i:hd * (i + 1)]
            y_scr[b, row, :] = jnp.sum(jnp.where(eye, y, 0.0), axis=0, keepdims=True)

    def step(t, carry):
        row = pl.ds(t, 1)
        prev_row = pl.ds(jnp.maximum(t - 1, 0), 1)
        fed = [feedback_dot(row, g) for g in groups]
        for g in groups:
            readout(prev_row, g)
        for g, f in zip(groups, fed):
            update(row, g, *f)
        return carry

    for b in range(nb):
        sr_scr[b] = jnp.zeros((hd, gw), f32)
    lax.fori_loop(0, n_steps, step, 0, unroll=RWKV_UNROLL)
    for g in groups:
        readout(pl.ds(n_steps - 1, 1), g)

    rows_all = (nb * tbr, gw)
    y = y_scr[...].reshape(rows_all)
    d = y - _dot_exact(y, seg_b) * (1.0 / HEAD_DIM)
    var = _dot_exact(d * d, seg_b) * (1.0 / HEAD_DIM)
    yn = d * lax.rsqrt(var + RWKV_LNX_EPS) * lg_ref[...] + lb_ref[...]
    o_ref[...] = ((yn + bonus_scr[...].reshape(rows_all)) * g_scr[...].reshape(rows_all)).reshape(per_batch)

    @pl.when(tb == pl.num_programs(1) - 1)
    def _fin():
        for b in range(nb):
            for h in range(N_HEADS):
                sfin_ref[b, h] = s_scr[b, :, HEAD_DIM * h:HEAD_DIM * (h + 1)]


def _rwkv(c, shift_prev, s0, p, seg_b, n_valid):
    bsz, t, _ = c.shape
    nb = RWKV_BATCH
    tbr = min(t, RWKV_TBLOCK)
    n_steps = min(n_valid, tbr)
    vec = lambda width: pl.BlockSpec((1, width), lambda b, i: (0, 0))
    mat = lambda rows: pl.BlockSpec((rows, GROUP_WIDTH), lambda b, i: (0, 0))
    rows_scr = pltpu.VMEM((nb, tbr, GROUP_WIDTH), f32)
    return pl.pallas_call(
        functools.partial(_rwkv_body, nb=nb, n_steps=n_steps),
        out_shape=(jax.ShapeDtypeStruct((bsz, t, GROUP_WIDTH), f32),
                   jax.ShapeDtypeStruct((bsz, N_HEADS, HEAD_DIM, HEAD_DIM), f32)),
        grid=(bsz // nb, t // tbr),
        in_specs=[pl.BlockSpec((nb, tbr, RWKV_COLS), lambda b, i: (b, i, 0)),
                  pl.BlockSpec((nb, 1, RWKV_COLS), lambda b, i: (b, 0, 0)),
                  pl.BlockSpec((nb, N_HEADS, HEAD_DIM, HEAD_DIM), lambda b, i: (b, 0, 0, 0)),
                  vec(RWKV_COLS), vec(GROUP_WIDTH), mat(LANES), mat(LANES), vec(GROUP_WIDTH), mat(LANES), mat(LANES),
                  mat(RWKV_G_RANK), mat(RWKV_G_RANK),
                  vec(GROUP_WIDTH), vec(GROUP_WIDTH), vec(GROUP_WIDTH), vec(GROUP_WIDTH), vec(GROUP_WIDTH),
                  mat(GROUP_WIDTH)],
        out_specs=(pl.BlockSpec((nb, tbr, GROUP_WIDTH), lambda b, i: (b, i, 0)),
                   pl.BlockSpec((nb, N_HEADS, HEAD_DIM, HEAD_DIM), lambda b, i: (b, 0, 0, 0))),
        scratch_shapes=[pltpu.VMEM((nb, HEAD_DIM, GROUP_WIDTH), f32), pltpu.VMEM((nb, 1, RWKV_COLS), f32)]
        + [rows_scr] * 9 + [pltpu.VMEM((nb, HEAD_DIM, GROUP_WIDTH), f32)],
        compiler_params=_params("parallel", "arbitrary"),
        name="rwkv",
    )(c, shift_prev, s0, p["mu"], p["w0"], *p["w2"], p["a0"], *p["a2"], *p["g2"], p["kk"], p["ka"], p["rk"],
      p["lnx_g"], p["lnx_b"], seg_b)


def _merge_rows(acc, mrun, den, o_new, m_new, d_new):
    m = jnp.maximum(mrun, m_new)
    a_old, a_new = jnp.exp(mrun - m), jnp.exp(m_new - m)
    return acc * a_old + o_new * a_new, m, den * a_old + d_new * a_new


def _dil_prompt_body(x_ref, o_ref, kv0_ref, kv1_ref, kv2_ref, acc_scr, m_scr, d_scr, *, t):
    grp = pl.program_id(1)
    blk = LANES
    n_pairs = GROUP_WIDTH // LANES
    rr, cc = _iota((blk, blk), 0), _iota((blk, blk), 1)
    first_head = _iota((1, LANES), 1) < HEAD_DIM
    ones_b = jnp.ones((blk, LANES), bf16)
    neg_inf = float("-inf")
    kv_refs = (kv0_ref, kv1_ref, kv2_ref)

    def group(gi, window, dilation):
        n_sub = t // dilation // blk
        dist_cur = (rr - cc).astype(f32) * float(dilation)
        dist_prev = (rr + blk - cc).astype(f32) * float(dilation)

        def scores(idx):
            res, sub = idx // n_sub, idx % n_sub
            start = res + dilation * blk * sub
            pstart = jnp.maximum(start - dilation * blk, 0)
            rows = pl.ds(start, blk, stride=dilation) if dilation > 1 else pl.ds(start, blk)
            prows = pl.ds(pstart, blk, stride=dilation) if dilation > 1 else pl.ds(pstart, blk)
            no_prev = jnp.where(sub > 0, 0.0, neg_inf)
            raw = []
            for pair in range(n_pairs):
                q = x_ref[0, pair, rows, :] * (HEAD_DIM ** -0.5)
                kc = x_ref[0, 2 + pair, rows, :].astype(bf16)
                kp = x_ref[0, 2 + pair, prows, :].astype(bf16) if n_sub > 1 else None
                for hh in range(2):
                    lanes = first_head if hh == 0 else jnp.logical_not(first_head)
                    qm = jnp.where(lanes, q, 0.0).astype(bf16)
                    raw.append((_dot_tr(qm, kc), _dot_tr(qm, kp) if n_sub > 1 else None))
            return rows, prows, no_prev, raw

        def finish(rows, prows, no_prev, raw):
            for pair in range(n_pairs):
                vc = jnp.concatenate([x_ref[0, 4 + pair, rows, :].astype(bf16), ones_b], axis=1)
                if n_sub > 1:
                    vp = jnp.concatenate([x_ref[0, 4 + pair, prows, :].astype(bf16), ones_b], axis=1)
                o_pair = m_pair = d_pair = None
                for hh in range(2):
                    slope = _alibi_slope(gi * N_HEADS + 2 * pair + hh)
                    lanes = first_head if hh == 0 else jnp.logical_not(first_head)
                    s_cur, s_prev = raw[2 * pair + hh]
                    s_cur = jnp.where(rr >= cc, s_cur - slope * dist_cur, neg_inf)
                    mx = jnp.max(s_cur, axis=-1, keepdims=True)
                    if n_sub > 1:
                        s_prev = jnp.where(cc >= rr, s_prev - slope * dist_prev, neg_inf) + no_prev
                        mx = jnp.maximum(mx, jnp.max(s_prev, axis=-1, keepdims=True))
                    pv = _dot(jnp.exp(s_cur - mx).astype(bf16), vc)
                    if n_sub > 1:
                        pv += _dot(jnp.exp(s_prev - mx).astype(bf16), vp)
                    o_h, d_h = pv[:, :LANES], pv[:, LANES:]
                    m_h = jnp.broadcast_to(mx, (blk, LANES))
                    if hh == 0:
                        o_pair, m_pair, d_pair = o_h, m_h, d_h
                    else:
                        o_pair = jnp.where(lanes, o_h, o_pair)
                        m_pair = jnp.where(lanes, m_h, m_pair)
                        d_pair = jnp.where(lanes, d_h, d_pair)
                if gi > 0:
                    o_pair, m_pair, d_pair = _merge_rows(acc_scr[pair, rows, :], m_scr[pair, rows, :],
                                                         d_scr[pair, rows, :], o_pair, m_pair, d_pair)
                acc_scr[pair, rows, :] = o_pair
                m_scr[pair, rows, :] = m_pair
                d_scr[pair, rows, :] = d_pair

        def combos(i, carry):
            ctx = [scores(DIL_COMBOS * i + j) for j in range(DIL_COMBOS)]
            for c in ctx:
                finish(*c)
            return carry

        lax.fori_loop(0, t // blk // DIL_COMBOS, combos, 0)

        w = min(window, t)
        for j in range(2 * n_pairs):
            for c in range(w // blk):
                tile = x_ref[0, n_pairs + j, t - w + c * blk:t - w + (c + 1) * blk, :]
                kv_refs[gi][0, j * LANES:(j + 1) * LANES, c * blk:(c + 1) * blk] = tile.T

    for gi, (window, dilation) in enumerate(DIL_PAIRS):
        pl.when(grp == gi)(functools.partial(group, gi, window, dilation))

    @pl.when(grp == len(DIL_PAIRS) - 1)
    def _fin():
        o_ref[...] = acc_scr[...] / d_scr[...]


def _dil_prompt(x, bsz, t):
    n_pairs = GROUP_WIDTH // LANES
    scr = pltpu.VMEM((n_pairs, t, LANES), f32)
    kv_rows = [min(w, t) for w, _ in DIL_PAIRS]
    return pl.pallas_call(
        functools.partial(_dil_prompt_body, t=t),
        out_shape=(jax.ShapeDtypeStruct((n_pairs, bsz * t, LANES), f32),)
        + tuple(jax.ShapeDtypeStruct((bsz, 2 * GROUP_WIDTH, r), f32) for r in kv_rows),
        grid=(bsz, len(DIL_PAIRS)),
        in_specs=[pl.BlockSpec((1, DIL_SLABS, t, LANES), lambda b, g: (g, 0, b, 0))],
        out_specs=(pl.BlockSpec((n_pairs, t, LANES), lambda b, g: (0, b, 0)),)
        + tuple(pl.BlockSpec((1, 2 * GROUP_WIDTH, r), lambda b, g: (b, 0, 0)) for r in kv_rows),
        scratch_shapes=[scr, scr, scr],
        compiler_params=_params("parallel", "arbitrary"),
        name="dil_prompt",
    )(x)


def _dil_sample_body(x_ref, c0_ref, c1_ref, c2_ref, o_ref, *, t_new):
    caches = (c0_ref, c1_ref, c2_ref)
    nq = SUBLANES
    per = nq // t_new
    first_head = _iota((1, LANES), 1) < HEAD_DIM
    neg_inf = float("-inf")
    q_idx = _iota((nq, 1), 0) & (t_new - 1)
    q_bat = _iota((nq, 1), 0) >> 2
    rn, cn = _iota((nq, nq), 0), _iota((nq, nq), 1)
    dn = (rn & (t_new - 1)) - (cn & (t_new - 1))
    same = (rn >> 2) == (cn >> 2)
    for pair in range(GROUP_WIDTH // LANES):
        acc = mrun = den = None
        for gi, (window, dilation) in enumerate(DIL_PAIRS):
            q = x_ref[gi, pair] * (HEAD_DIM ** -0.5)
            kn = x_ref[gi, 2 + pair].astype(bf16)
            vn = x_ref[gi, 4 + pair].astype(bf16)
            dist_c = window + q_idx - _iota((nq, window), 1)
            ok_c = jnp.logical_and((dist_c & (dilation - 1)) == 0, dist_c <= window)
            ok_n = jnp.logical_and(jnp.logical_and(same, dn >= 0), (dn & (dilation - 1)) == 0)
            q2 = jnp.concatenate([jnp.where(first_head, q, 0.0), jnp.where(first_head, 0.0, q)], axis=0).astype(bf16)
            slope2 = jnp.where(_iota((2 * nq, 1), 0) < nq, _alibi_slope(gi * N_HEADS + 2 * pair),
                               _alibi_slope(gi * N_HEADS + 2 * pair + 1))
            dist_c2 = jnp.concatenate([dist_c, dist_c], axis=0).astype(f32)
            ok_c2 = jnp.concatenate([ok_c, ok_c], axis=0)
            dn2 = jnp.concatenate([dn, dn], axis=0).astype(f32)
            ok_n2 = jnp.concatenate([ok_n, ok_n], axis=0)
            s_new = jnp.where(ok_n2, _dot_tr(q2, kn) - slope2 * dn2, neg_inf)
            mx_new = jnp.max(s_new, axis=-1, keepdims=True)
            o_pair = m_pair = d_pair = None
            for s in range(per):
                kt = caches[gi][0, s, pair * LANES:(pair + 1) * LANES, :].astype(bf16)
                vt = caches[gi][0, s, GROUP_WIDTH + pair * LANES:GROUP_WIDTH + (pair + 1) * LANES, :].astype(bf16)
                s_c = jnp.where(ok_c2, _dot(q2, kt) - slope2 * dist_c2, neg_inf)
                mx = jnp.maximum(jnp.max(s_c, axis=-1, keepdims=True), mx_new)
                p_c, p_n = jnp.exp(s_c - mx), jnp.exp(s_new - mx)
                o2 = _dot_tr(p_c.astype(bf16), vt) + _dot(p_n.astype(bf16), vn)
                d2 = jnp.sum(p_c, axis=-1, keepdims=True) + jnp.sum(p_n, axis=-1, keepdims=True)
                o_s = jnp.where(first_head, o2[:nq], o2[nq:])
                m_s = jnp.where(first_head, jnp.broadcast_to(mx[:nq], (nq, LANES)), jnp.broadcast_to(mx[nq:], (nq, LANES)))
                d_s = jnp.where(first_head, jnp.broadcast_to(d2[:nq], (nq, LANES)), jnp.broadcast_to(d2[nq:], (nq, LANES)))
                if s == 0:
                    o_pair, m_pair, d_pair = o_s, m_s, d_s
                else:
                    mine = q_bat == s
                    o_pair = jnp.where(mine, o_s, o_pair)
                    m_pair = jnp.where(mine, m_s, m_pair)
                    d_pair = jnp.where(mine, d_s, d_pair)
            if gi == 0:
                acc, mrun, den = o_pair, m_pair, d_pair
            else:
                acc, mrun, den = _merge_rows(acc, mrun, den, o_pair, m_pair, d_pair)
        o_ref[pair] = acc / den


def _dil_sample(x, caches, layer, bsz, t_new):
    n_pairs = GROUP_WIDTH // LANES
    per = SUBLANES // t_new
    return pl.pallas_call(
        functools.partial(_dil_sample_body, t_new=t_new),
        out_shape=jax.ShapeDtypeStruct((n_pairs, bsz * t_new, LANES), f32),
        grid=(bsz // per,),
        in_specs=[pl.BlockSpec((len(DIL_PAIRS), DIL_SLABS, SUBLANES, LANES), lambda i: (0, 0, i, 0))]
        + [pl.BlockSpec((1, per, 2 * GROUP_WIDTH, w), lambda i: (layer, i, 0, 0)) for w, _ in DIL_PAIRS],
        out_specs=pl.BlockSpec((n_pairs, SUBLANES, LANES), lambda i: (0, i, 0)),
        compiler_params=_params("parallel"),
        name="dil_sample",
    )(x, *caches)


def _row(v):
    return v.reshape(1, -1)


def _prep_layer(l, norm_gains, w_ff_gate, w_ff_up, w_ff_down, w_in, w_out, gla_w_alpha2, gla_b_alpha, gla_norm,
                gmlp_ln_g, gmlp_ln_b, gmlp_ws, gmlp_bs, rwkv_mu, rwkv_w0, rwkv_w2, rwkv_a0, rwkv_a2, rwkv_g2,
                rwkv_kk, rwkv_ka, rwkv_rk, rwkv_lnx_g, rwkv_lnx_b):
    win = w_in[l]
    o1, o2, o3 = GLA_COLS, GLA_COLS + GMLP_COLS, GLA_COLS + GMLP_COLS + RWKV_COLS
    hk = N_HEADS * GLA_DK
    a_lo, a_hi = 2 * hk + GROUP_WIDTH, 2 * hk + GROUP_WIDTH + GLA_RANK
    w_gla = win[:, :o1]
    w_gla = jnp.concatenate([w_gla[:, :a_lo], w_gla[:, a_hi:], w_gla[:, a_lo:a_hi],
                             jnp.zeros((D_MODEL, GLA_COLS_PAD - GLA_COLS), f32)], axis=1)
    zeros_r = jnp.zeros((RWKV_W_RANK, GROUP_WIDTH), f32)
    return dict(
        norms=[_row(norm_gains[l, i]) for i in range(norm_gains.shape[1])],
        w_gla=w_gla.astype(bf16), w_gmlp=win[:, o1:o2].astype(bf16), w_rwkv=win[:, o2:o3].astype(bf16),
        w_dil=win[:, o3:].astype(bf16),
        gla_wa=_split_param(jnp.concatenate([gla_w_alpha2[l], jnp.zeros((LANES - GLA_RANK, hk), f32)], axis=0)),
        gla_ba=_row(gla_b_alpha[l]), gla_norm=_row(gla_norm[l]),
        gmlp_ln_g=_row(gmlp_ln_g[l]), gmlp_ln_b=_row(gmlp_ln_b[l]), gmlp_ws=gmlp_ws[l], gmlp_bs=gmlp_bs[l],
        rwkv=dict(mu=_row(rwkv_mu[l]), w0=_row(rwkv_w0[l]),
                  w2=_split_param(jnp.concatenate([rwkv_w2[l], zeros_r], axis=0)), a0=_row(rwkv_a0[l]),
                  a2=_split_param(jnp.concatenate([zeros_r, rwkv_a2[l]], axis=0)), g2=_split_param(rwkv_g2[l]),
                  kk=_row(rwkv_kk[l]), ka=_row(rwkv_ka[l]), rk=_row(rwkv_rk[l]),
                  lnx_g=_row(rwkv_lnx_g[l]), lnx_b=_row(rwkv_lnx_b[l])),
    )


def _kv_rows(c_dil, bsz, t, rows):
    out = []
    for gi in range(len(DIL_PAIRS)):
        kv = c_dil[gi, 2:].reshape(2, GROUP_WIDTH // LANES, bsz, t, LANES)[:, :, :, t - rows[gi]:]
        kv = jnp.transpose(kv, (2, 3, 0, 1, 4))
        out.append(kv.reshape(bsz, rows[gi], 2, N_HEADS, HEAD_DIM))
    return out


def _channel_major_kv(kv):
    bsz, _, rows = kv.shape
    return jnp.transpose(kv.reshape(bsz, 2, N_HEADS, HEAD_DIM, rows), (0, 4, 1, 2, 3))


def _trunk_layer(x, p, ffw, seg_b, bsz, t, gla_h0, rwkv_s0, shift0, caches, layer):
    ng = p["norms"]
    n = bsz * t
    h = _ffn(x, ng[0], ng[1], ffw["gate"], ffw["up"], ffw["down"], layer, 0)
    c_gla, c_gmlp, c_rwkv, c_dil = _proj(h, ng[2], p["w_gla"], p["w_gmlp"], p["w_rwkv"], p["w_dil"])
    is_prompt = caches is None

    t_gla = max(t, GLA_CHUNK)
    cg = c_gla.reshape(bsz, t, GLA_COLS_PAD)
    if t_gla != t:
        cg = jnp.pad(cg, ((0, 0), (0, t_gla - t), (0, 0)))
    o_gla, gla_s = _gla(cg, gla_h0, p["gla_wa"], p["gla_ba"], p["gla_norm"], min(t, GLA_CHUNK))
    o_gla = o_gla[:, :t].reshape(n, GROUP_WIDTH)

    if t >= GMLP_CHUNK:
        ws = p["gmlp_ws"]
        bias_rows = jnp.repeat(p["gmlp_bs"].T, HEAD_DIM, axis=1)
    else:
        per = GMLP_CHUNK // t
        eye = jnp.eye(per, dtype=f32)
        ws = jnp.einsum("ab,gij->gaibj", eye, p["gmlp_ws"][:, :t, :t]).reshape(N_HEADS, GMLP_CHUNK, GMLP_CHUNK)
        bias_rows = jnp.tile(jnp.repeat(p["gmlp_bs"][:, :t].T, HEAD_DIM, axis=1), (per, 1))
    gm = _gmlp(c_gmlp, p["gmlp_ln_g"], p["gmlp_ln_b"], ws, bias_rows, emit_v=not is_prompt)
    o_gmlp = gm[0]
    gmlp_v = None if is_prompt else gm[1]

    t_rw = max(t, SUBLANES)
    cr = c_rwkv.reshape(bsz, t, RWKV_COLS)
    shift = cr[:, -1]
    if t_rw != t:
        cr = jnp.pad(cr, ((0, 0), (0, t_rw - t), (0, 0)))
    o_rwkv, rwkv_s = _rwkv(cr, shift0.reshape(bsz, 1, RWKV_COLS), rwkv_s0, p["rwkv"], seg_b, t)
    o_rwkv = o_rwkv[:, :t].reshape(n, GROUP_WIDTH)

    if is_prompt:
        o_dil, *kv_cm = _dil_prompt(c_dil, bsz, t)
        kv_new = [_channel_major_kv(kv) for kv in kv_cm]
    else:
        o_dil = _dil_sample(c_dil, caches, layer, bsz, t)
        kv_new = _kv_rows(c_dil, bsz, t, [min(w, t) for w, _ in DIL_PAIRS])

    y = _ffn(h, ng[4], ng[5], ffw["gate"], ffw["up"], ffw["down"], layer, 1,
             mix=(o_gla, o_gmlp, o_rwkv, o_dil, ffw["out"], ng[3]))
    return y, gla_s, rwkv_s, shift, gmlp_v, kv_new


def kernel(x_prompt, x_sample, cache_win128, cache_win512, cache_win2048, state_gla, state_rwkv, state_shift,
           norm_gains, w_ff_gate, w_ff_up, w_ff_down, w_in, w_out, gla_w_alpha2, gla_b_alpha, gla_norm,
           gmlp_ln_g, gmlp_ln_b, gmlp_ws, gmlp_bs, rwkv_mu, rwkv_w0, rwkv_w2, rwkv_a0, rwkv_a2, rwkv_g2,
           rwkv_kk, rwkv_ka, rwkv_rk, rwkv_lnx_g, rwkv_lnx_b):
    bp, tp, _ = x_prompt.shape
    bs, ts, _ = x_sample.shape
    depth = norm_gains.shape[0]
    assert tp % GMLP_CHUNK == 0 and tp % (DIL_PAIRS[-1][1] * LANES) == 0 and bp % RWKV_BATCH == 0
    assert ts == 4 and bs % RWKV_BATCH == 0 and (bs * ts) % GMLP_CHUNK == 0
    caches = [jnp.transpose(c, (0, 1, 3, 4, 5, 2)).reshape(depth, bs, 2 * GROUP_WIDTH, c.shape[2])
              for c in (cache_win128, cache_win512, cache_win2048)]
    lanes = np.arange(GROUP_WIDTH)
    seg_b = jnp.asarray((lanes[:, None] // HEAD_DIM) == (lanes[None, :] // HEAD_DIM), dtype=bf16)
    ffw = dict(gate=w_ff_gate.astype(bf16), up=w_ff_up.astype(bf16), down=w_ff_down.astype(bf16),
               out=w_out.astype(bf16))
    gla0 = jnp.zeros((bp, N_HEADS, GLA_DK, HEAD_DIM), f32)
    rwkv0 =jnp.zeros((bp, N_HEADS, HEAD_DIM, HEAD_DIM), f32)
    shift0 = jnp.zeros((bp, RWKV_COLS), f32)
    yp = x_prompt.reshape(bp * tp, D_MODEL)
    ys = x_sample.reshape(bs * ts, D_MODEL)
    pg, pr, psh, pw = [], [], [], [[], [], []]
    sg, sr, ssh, sw, sv = [], [], [], [[], [], []], []
    for l in range(depth):
        p = _prep_layer(l, norm_gains, w_ff_gate, w_ff_up, w_ff_down, w_in, w_out, gla_w_alpha2, gla_b_alpha,
                        gla_norm, gmlp_ln_g, gmlp_ln_b, gmlp_ws, gmlp_bs, rwkv_mu, rwkv_w0, rwkv_w2, rwkv_a0,
                        rwkv_a2, rwkv_g2, rwkv_kk, rwkv_ka, rwkv_rk, rwkv_lnx_g, rwkv_lnx_b)
        yp, g_p, r_p, sh_p, _, kv_p = _trunk_layer(yp, p, ffw, seg_b, bp, tp, gla0, rwkv0, shift0, None, l)
        ys, g_s, r_s, sh_s, v_s, kv_s = _trunk_layer(ys, p, ffw, seg_b, bs, ts, state_gla[l], state_rwkv[l],
                                                     state_shift[l], caches, l)
        pg.append(g_p)
        pr.append(r_p)
        psh.append(sh_p)
        sg.append(g_s)
        sr.append(r_s)
        ssh.append(sh_s)
        sv.append(v_s.reshape(bs, ts, GROUP_WIDTH))
        for gi in range(len(DIL_PAIRS)):
            pw[gi].append(kv_p[gi])
            sw[gi].append(kv_s[gi])
    return (yp.reshape(bp, tp, D_MODEL), ys.reshape(bs, ts, D_MODEL),
            jnp.stack(pg), jnp.stack(pr), jnp.stack(psh),
            jnp.stack(pw[0]), jnp.stack(pw[1]), jnp.stack(pw[2]),
            jnp.stack(sg), jnp.stack(sr), jnp.stack(ssh),
            jnp.stack(sw[0]), jnp.stack(sw[1]), jnp.stack(sw[2]),
            jnp.stack(sv))
```

```python
import functools

import jax
import jax.numpy as jnp
import numpy as np
from jax import lax
from jax.experimental import pallas as pl
from jax.experimental.pallas import tpu as pltpu

f32 = jnp.float32
bf16 = jnp.bfloat16

D_MODEL = 1024
HEAD_DIM = 64
N_HEADS = 4
GROUP_WIDTH = N_HEADS * HEAD_DIM
D_FF = 2816
EPS = 1e-6
GLA_DK = 32
GLA_RANK = 16
GLA_TAU = 16.0
GLA_CHUNK = 64
GMLP_CHUNK = 128
LN_EPS = 1e-5
RWKV_W_RANK = 64
RWKV_A_RANK = 64
RWKV_G_RANK = 128
RWKV_LNX_EPS = 64e-5
DIL_PAIRS = ((128, 1), (512, 4), (2048, 16))
N_DIL_HEADS = len(DIL_PAIRS) * N_HEADS
ALIBI_MAX_EXP = 8.0
GLA_COLS = 2 * N_HEADS * GLA_DK + GROUP_WIDTH + GLA_RANK + GROUP_WIDTH
GMLP_COLS = 2 * GROUP_WIDTH
RWKV_COLS = 3 * GROUP_WIDTH + RWKV_W_RANK + RWKV_A_RANK + RWKV_G_RANK
DIL_COLS = len(DIL_PAIRS) * 3 * GROUP_WIDTH

LANES = 128
SUBLANES = 8
VMEM_LIMIT_BYTES = 56 * 1024 * 1024

GLA_COLS_PAD = 896
DIL_SLABS = DIL_COLS // LANES // len(DIL_PAIRS)
FFN_CHUNK = 256
ROW_TILE = 512
FFN_ROW_TILE = 512
FFN_SUBTILE = 512
GMLP_BLOCK_CHUNKS = 4
DIL_COMBOS = 2
GLA_BLOCK_CHUNKS = 8
RWKV_BATCH = 8
RWKV_TBLOCK = 128
RWKV_GROUP = 2
RWKV_UNROLL = 4


def _alibi_slope(head):
    return float(2.0 ** (-ALIBI_MAX_EXP * (head + 1.0) / N_DIL_HEADS))


def _iota(shape, dim):
    return lax.broadcasted_iota(jnp.int32, shape, dim)


def _dot(a, b, precision=None):
    return jnp.dot(a, b, preferred_element_type=f32, precision=precision)


def _dot_tr(a, b, precision=None):
    return lax.dot_general(a, b, (((1,), (1,)), ((), ())), preferred_element_type=f32, precision=precision)


def _dot_tl(a, b, precision=None):
    return lax.dot_general(a, b, (((0,), (0,)), ((), ())), preferred_element_type=f32, precision=precision)


def _rms_rows(x, g):
    ms = jnp.mean(x * x, axis=-1, keepdims=True)
    return x * lax.rsqrt(ms + EPS) * g


def _softplus(x):
    return jnp.maximum(x, 0.0) + jnp.log1p(jnp.exp(-jnp.abs(x)))


def _split(x, n):
    parts = []
    for _ in range(n - 1):
        p = x.astype(bf16)
        parts.append(p)
        x = x - p.astype(f32)
    parts.append(x.astype(bf16))
    return parts


def _split_param(w):
    hi = w.astype(bf16)
    return hi, (w - hi.astype(f32)).astype(bf16)


def _dot_exact(x, m_b, n=2):
    return functools.reduce(jnp.add, [_dot(p, m_b) for p in _split(x, n)])


def _dot_param(x, w_hi, w_lo):
    x_hi, x_lo = _split(x, 2)
    return _dot(x_hi, w_hi) + _dot(x_lo, w_hi) + _dot(x_hi, w_lo)


def _params(*semantics):
    return pltpu.CompilerParams(dimension_semantics=semantics, vmem_limit_bytes=VMEM_LIMIT_BYTES)


def _resident(shape):
    zeros = (0,) * len(shape)
    return pl.BlockSpec(shape, lambda *_: zeros, pipeline_mode=pl.Buffered(1))


def _ffn_body(*refs, with_mix):
    if with_mix:
        (h_ref, ogla_ref, ogmlp_ref, orwkv_ref, odil_ref, wout_ref, gmix_ref,
         gpre_ref, gpost_ref, wg_ref, wu_ref, wd_ref, o_ref) = refs
        x_ref = h_ref
    else:
        x_ref, gpre_ref, gpost_ref, wg_ref, wu_ref, wd_ref, o_ref = refs
    tm = x_ref.shape[0]
    sub = min(FFN_SUBTILE, tm)
    for s0 in range(0, tm, sub):
        rows = slice(s0, s0 + sub)
        x = x_ref[rows, :]
        if with_mix:
            gw = GROUP_WIDTH
            mix = _dot(ogla_ref[rows, :].astype(bf16), wout_ref[0, 0:gw, :])
            mix += _dot(ogmlp_ref[rows, :].astype(bf16), wout_ref[0, gw:2 * gw, :])
            mix += _dot(orwkv_ref[rows, :].astype(bf16), wout_ref[0, 2 * gw:3 * gw, :])
            for p in range(gw // LANES):
                mix += _dot(odil_ref[p, rows, :].astype(bf16),
                            wout_ref[0, 3 * gw + p * LANES:3 * gw + (p + 1) * LANES, :])
            x = x + _rms_rows(mix, gmix_ref[...])
        xn = _rms_rows(x, gpre_ref[...]).astype(bf16)
        acc = None
        for c in range(D_FF // FFN_CHUNK):
            lo, hi = c * FFN_CHUNK, (c + 1) * FFN_CHUNK
            g = _dot(xn, wg_ref[0, 0, :, lo:hi])
            u = _dot(xn, wu_ref[0, 0, :, lo:hi])
            act = (g * jax.nn.sigmoid(g) * u).astype(bf16)
            part = _dot(act, wd_ref[0, 0, lo:hi, :])
            acc = part if acc is None else acc + part
        o_ref[rows, :] = x + 0.5 * _rms_rows(acc, gpost_ref[...])


def _ffn(x, g_pre, g_post, wg, wu, wd, layer, idx, mix=None):
    n = x.shape[0]
    tm = min(FFN_ROW_TILE, n)
    row = pl.BlockSpec((tm, D_MODEL), lambda i: (i, 0))
    once = dict(pipeline_mode=pl.Buffered(1))
    w_specs = [pl.BlockSpec((1, 1, D_MODEL, D_FF), lambda i: (layer, idx, 0, 0), **once),
               pl.BlockSpec((1, 1, D_MODEL, D_FF), lambda i: (layer, idx, 0, 0), **once),
               pl.BlockSpec((1, 1, D_FF, D_MODEL), lambda i: (layer, idx, 0, 0), **once)]
    in_specs, args = [row], [x]
    if mix is not None:
        o_gla, o_gmlp, o_rwkv, o_dil, w_out, g_mix = mix
        grp = pl.BlockSpec((tm, GROUP_WIDTH), lambda i: (i, 0))
        in_specs += [grp, grp, grp, pl.BlockSpec((GROUP_WIDTH // LANES, tm, LANES), lambda i: (0, i, 0)),
                     pl.BlockSpec((1, D_MODEL, D_MODEL), lambda i: (layer, 0, 0), **once), _resident((1, D_MODEL))]
        args += [o_gla, o_gmlp, o_rwkv, o_dil, w_out, g_mix]
    return pl.pallas_call(
        functools.partial(_ffn_body, with_mix=mix is not None),
        out_shape=jax.ShapeDtypeStruct((n, D_MODEL), f32),
        grid=(n // tm,),
        in_specs=in_specs + [_resident((1, D_MODEL)), _resident((1, D_MODEL))] + w_specs,
        out_specs=row,
        compiler_params=_params("parallel"),
        name="ffn_mix" if mix is not None else "ffn",
    )(*args, g_pre, g_post, wg, wu, wd)


def _proj_body(h_ref, g_ref, wgla_ref, wgmlp_ref, wrwkv_ref, wdil_ref, ogla_ref, ogmlp_ref, orwkv_ref, odil_ref):
    xn = _rms_rows(h_ref[...], g_ref[...]).astype(bf16)
    ogla_ref[...] = _dot(xn, wgla_ref[...])
    ogmlp_ref[...] = _dot(xn, wgmlp_ref[...])
    orwkv_ref[...] = _dot(xn, wrwkv_ref[...])
    res = _dot(xn, wdil_ref[...])
    for s in range(DIL_COLS // LANES):
        odil_ref[s // DIL_SLABS, s % DIL_SLABS] = res[:, s * LANES:(s + 1) * LANES]


def _proj(h, g, w_gla, w_gmlp, w_rwkv, w_dil):
    n = h.shape[0]
    tm = min(ROW_TILE, n)
    n_grp = len(DIL_PAIRS)
    return pl.pallas_call(
        _proj_body,
        out_shape=(jax.ShapeDtypeStruct((n, GLA_COLS_PAD), f32),
                   jax.ShapeDtypeStruct((n, GMLP_COLS), f32),
                   jax.ShapeDtypeStruct((n, RWKV_COLS), f32),
                   jax.ShapeDtypeStruct((n_grp, DIL_SLABS, n, LANES), f32)),
        grid=(n // tm,),
        in_specs=[pl.BlockSpec((tm, D_MODEL), lambda i: (i, 0)), _resident((1, D_MODEL)),
                  _resident((D_MODEL, GLA_COLS_PAD)), _resident((D_MODEL, GMLP_COLS)),
                  _resident((D_MODEL, RWKV_COLS)), _resident((D_MODEL, DIL_COLS))],
        out_specs=(pl.BlockSpec((tm, GLA_COLS_PAD), lambda i: (i, 0)),
                   pl.BlockSpec((tm, GMLP_COLS), lambda i: (i, 0)),
                   pl.BlockSpec((tm, RWKV_COLS), lambda i: (i, 0)),
                   pl.BlockSpec((n_grp, DIL_SLABS, tm, LANES), lambda i: (0, 0, i, 0))),
        compiler_params=_params("parallel"),
        name="proj",
    )(h, g, w_gla, w_gmlp, w_rwkv, w_dil)


def _gla_body(c_ref, h0_ref, wa_hi_ref, wa_lo_ref, ba_ref, ng_ref, o_ref, sfin_ref, s_scr, *, n_chunks, valid_rows):
    C = GLA_CHUNK
    hk, hv = N_HEADS * GLA_DK, GROUP_WIDTH
    tb = pl.program_id(1)

    @pl.when(tb == 0)
    def _init():
        s_scr[...] = jnp.zeros((hk, hv), f32)
        for h in range(N_HEADS):
            s_scr[GLA_DK * h:GLA_DK * (h + 1), HEAD_DIM * h:HEAD_DIM * (h + 1)] = h0_ref[0, h]

    rr, cc = _iota((C, C), 0), _iota((C, C), 1)
    causal = rr >= cc
    tril = causal.astype(bf16)
    k_head = _iota((1, hk), 1) >> 5
    v_head = _iota((1, hv), 1) >> 6
    diag_blocks = (_iota((hk, hv), 0) >> 5) == (_iota((hk, hv), 1) >> 6)
    seg_b = ((_iota((hv, hv), 0) >> 6) == (_iota((hv, hv), 1) >> 6)).astype(bf16)
    ones_cv = jnp.ones((C, hv), bf16)

    z = _dot_param(c_ref[0, :, 2 * hk + 2 * hv:], wa_hi_ref[...], wa_lo_ref[...]) + ba_ref[...]
    log_a_all = -_softplus(-z) * (1.0 / GLA_TAU)
    if valid_rows < C:
        log_a_all = jnp.where(_iota(log_a_all.shape, 0) < valid_rows, log_a_all, 0.0)
    stash = []
    for ci in range(n_chunks):
        x = c_ref[0, ci * C:(ci + 1) * C, :]
        q = x[:, 0:hk] * (GLA_DK ** -0.5)
        k = x[:, hk:2 * hk]
        v = x[:, 2 * hk:2 * hk + hv]
        la_parts = _split(log_a_all[ci * C:(ci + 1) * C], 3)
        b = functools.reduce(jnp.add, [_dot(tril, p) for p in la_parts])
        b_last = b[C - 1:C, :]
        qe = q * jnp.exp(b)
        keb = (k * jnp.exp(-b)).astype(bf16)
        kd = k * jnp.exp(b_last - b)
        o_intra = None
        for h in range(N_HEADS):
            att = _dot_tr(jnp.where(k_head == h, qe, 0.0).astype(bf16), keb)
            att = jnp.where(causal, att, 0.0)
            part = _dot(att.astype(bf16), jnp.where(v_head == h, v, 0.0).astype(bf16))
            o_intra = part if o_intra is None else o_intra + part
        incr = jnp.where(diag_blocks, _dot_tl(kd.astype(bf16), v.astype(bf16)), 0.0)
        chunk_log_decay = functools.reduce(jnp.add, [_dot_tl(p, ones_cv) for p in la_parts])
        stash.append((qe.astype(bf16), o_intra, incr, jnp.exp(chunk_log_decay)))

    state = s_scr[...]
    outs = []
    for qe_b, o_intra, incr, decay in stash:
        outs.append(o_intra + _dot(qe_b, state.astype(bf16)))
        state = decay * state + incr
    s_scr[...] = state

    for ci, o in enumerate(outs):
        g = c_ref[0, ci * C:(ci + 1) * C, 2 * hk + hv:2 * hk + 2 * hv]
        ms = _dot_exact(o * o, seg_b) * (1.0 / HEAD_DIM)
        o_ref[0, ci * C:(ci + 1) * C, :] = o * lax.rsqrt(ms + EPS) * ng_ref[...] * (g * jax.nn.sigmoid(g))

    @pl.when(tb == pl.num_programs(1) - 1)
    def _fin():
        for h in range(N_HEADS):
            sfin_ref[0, h] = s_scr[GLA_DK * h:GLA_DK * (h + 1), HEAD_DIM * h:HEAD_DIM * (h + 1)]


def _gla(c, h0, wa_pad, b_alpha, norm_g, valid_rows):
    bsz, t, _ = c.shape
    tb = min(t, GLA_BLOCK_CHUNKS * GLA_CHUNK)
    wa_spec = pl.BlockSpec((LANES, N_HEADS * GLA_DK), lambda b, i: (0, 0))
    return pl.pallas_call(
        functools.partial(_gla_body, n_chunks=tb // GLA_CHUNK, valid_rows=valid_rows),
        out_shape=(jax.ShapeDtypeStruct((bsz, t, GROUP_WIDTH), f32),
                   jax.ShapeDtypeStruct((bsz, N_HEADS, GLA_DK, HEAD_DIM), f32)),
        grid=(bsz, t // tb),
        in_specs=[pl.BlockSpec((1, tb, GLA_COLS_PAD), lambda b, i: (b, i, 0)),
                  pl.BlockSpec((1, N_HEADS, GLA_DK, HEAD_DIM), lambda b, i: (b, 0, 0, 0)),
                  wa_spec, wa_spec,
                  pl.BlockSpec((1, N_HEADS * GLA_DK), lambda b, i: (0, 0)),
                  pl.BlockSpec((1, GROUP_WIDTH), lambda b, i: (0, 0))],
        out_specs=(pl.BlockSpec((1, tb, GROUP_WIDTH), lambda b, i: (b, i, 0)),
                   pl.BlockSpec((1, N_HEADS, GLA_DK, HEAD_DIM), lambda b, i: (b, 0, 0, 0))),
        scratch_shapes=[pltpu.VMEM((N_HEADS * GLA_DK, GROUP_WIDTH), f32)],
        compiler_params=_params("parallel", "arbitrary"),
        name="gla",
    )(c, h0, *wa_pad, b_alpha, norm_g)


def _gmlp_body(c_ref, lng_ref, lnb_ref, ws_ref, bias_ref, o_ref, *v_refs):
    gw = GROUP_WIDTH
    x = c_ref[...]
    ge = 0.5 * x * (1.0 + lax.erf(x * float(np.sqrt(0.5))))
    u, v = ge[:, :gw], ge[:, gw:]
    mean = jnp.mean(v, axis=-1, keepdims=True)
    d = v - mean
    var = jnp.mean(d * d, axis=-1, keepdims=True)
    vn = d * lax.rsqrt(var + LN_EPS) * lng_ref[...] + lnb_ref[...]
    n = GMLP_CHUNK
    causal = _iota((n, n), 0) >= _iota((n, n), 1)
    v_head = _iota((1, gw), 1) >> 6
    w_all = jnp.concatenate([jnp.where(causal, ws_ref[grp], 0.0).astype(bf16) for grp in range(N_HEADS)], axis=0)
    vb = vn.astype(bf16)
    for ci in range(x.shape[0] // n):
        rows = slice(ci * n, (ci + 1) * n)
        res = _dot(w_all, vb[rows])
        s = res[0:n]
        for grp in range(1, N_HEADS):
            s = jnp.where(v_head == grp, res[grp * n:(grp + 1) * n], s)
        o_ref[rows, :] = u[rows] * (s + bias_ref[...])
    if v_refs:
        v_refs[0][...] = vn


def _gmlp(c, ln_g, ln_b, ws, bias_rows, emit_v):
    n = c.shape[0]
    rows = min(n, GMLP_BLOCK_CHUNKS * GMLP_CHUNK)
    blk = pl.BlockSpec((rows, GROUP_WIDTH), lambda i: (i, 0))
    out_shape = [jax.ShapeDtypeStruct((n, GROUP_WIDTH), f32)]
    out_specs = [blk]
    if emit_v:
        out_shape.append(jax.ShapeDtypeStruct((n, GROUP_WIDTH), f32))
        out_specs.append(blk)
    return pl.pallas_call(
        _gmlp_body,
        out_shape=tuple(out_shape),
        grid=(n // rows,),
        in_specs=[pl.BlockSpec((rows, GMLP_COLS), lambda i: (i, 0)),
                  pl.BlockSpec((1, GROUP_WIDTH), lambda i: (0, 0)),
                  pl.BlockSpec((1, GROUP_WIDTH), lambda i: (0, 0)),
                  pl.BlockSpec((N_HEADS, GMLP_CHUNK, GMLP_CHUNK), lambda i: (0, 0, 0)),
                  pl.BlockSpec((GMLP_CHUNK, GROUP_WIDTH), lambda i: (0, 0))],
        out_specs=tuple(out_specs),
        compiler_params=_params("parallel"),
        name="gmlp",
    )(c, ln_g, ln_b, ws, bias_rows)


def _rwkv_body(c_ref, sp_ref, s0_ref, mu_ref, w0_ref, w2h_ref, w2l_ref, a0_ref, a2h_ref, a2l_ref, g2h_ref, g2l_ref,
               kk_ref, ka_ref, rk_ref, lg_ref, lb_ref, seg_ref, o_ref, sfin_ref,
               s_scr, carry_scr, r_scr, w_scr, k_scr, v_scr, a_scr, b_scr, y_scr, g_scr, bonus_scr, sr_scr,
               *, nb, n_steps):
    gw = GROUP_WIDTH
    tbr = c_ref.shape[1]
    tb = pl.program_id(1)
    seg_b = seg_ref[...]

    @pl.when(tb == 0)
    def _init():
        for b in range(nb):
            carry_scr[b] = sp_ref[b]
            for h in range(N_HEADS):
                s_scr[b, :, HEAD_DIM * h:HEAD_DIM * (h + 1)] = s0_ref[b, h]

    first_row = _iota((tbr, RWKV_COLS), 0) == 0
    mixed = []
    for b in range(nb):
        c = c_ref[b]
        prev = jnp.where(first_row, carry_scr[b], pltpu.roll(c, 1, 0))
        carry_scr[b] = c[tbr - 1:tbr, :]
        mixed.append(c + (prev - c) * mu_ref[...])
    xm = jnp.concatenate(mixed, axis=0)
    r = xm[:, 0:gw]
    k = xm[:, gw:2 * gw]
    v = xm[:, 2 * gw:3 * gw]
    wa_lr = xm[:, 3 * gw:3 * gw + LANES]
    g_lr = xm[:, 3 * gw + LANES:]
    w = -_softplus(-(w0_ref[...] + _dot_param(jnp.tanh(wa_lr), w2h_ref[...], w2l_ref[...]))) - 0.5
    a = jax.nn.sigmoid(a0_ref[...] + _dot_param(wa_lr, a2h_ref[...], a2l_ref[...]))
    kk = k * kk_ref[...]
    kk = kk / jnp.maximum(jnp.sqrt(_dot_exact(kk * kk, seg_b)), 1e-12)
    k2 = k * (1.0 + (a - 1.0) * ka_ref[...])
    per_batch = (nb, tbr, gw)
    r_scr[...] = r.reshape(per_batch)
    w_scr[...] = jnp.exp(-jnp.exp(w)).reshape(per_batch)
    k_scr[...] = k2.reshape(per_batch)
    v_scr[...] = v.reshape(per_batch)
    a_scr[...] = (-kk).reshape(per_batch)
    b_scr[...] = (kk * a).reshape(per_batch)
    g_scr[...] = _dot_param(jax.nn.sigmoid(g_lr), g2h_ref[...], g2l_ref[...]).reshape(per_batch)
    bonus_scr[...] = (_dot_exact(r * k2 * rk_ref[...], seg_b) * v).reshape(per_batch)
    y_scr[...] = jnp.zeros(per_batch, f32)

    eye = (_iota((HEAD_DIM, gw), 1) & (HEAD_DIM - 1)) == _iota((HEAD_DIM, gw), 0)

    hd = HEAD_DIM

    groups = [list(range(g0, g0 + RWKV_GROUP)) for g0 in range(0, nb, RWKV_GROUP)]

    def feedback_dot(row, batches):
        states, parts = [], []
        for b in batches:
            state = s_scr[b]
            sa_in = state * a_scr[b, row, :]
            parts += [sa_in.astype(bf16), jnp.where(eye, v_scr[b, row, :], 0.0).astype(bf16)]
            states.append(state)
        return states, _dot(jnp.concatenate(parts, axis=0), seg_b)

    def update(row, batches, states, res):
        for i, b in enumerate(batches):
            base = 2 * hd * i
            sa = res[base:base + hd]
            v_col = res[base + hd:base + 2 * hd]
            state = states[i] * w_scr[b, row, :] + sa * b_scr[b, row, :] + v_col * k_scr[b, row, :]
            s_scr[b] = state
            sr_scr[b] = state * r_scr[b, row, :]

    def readout(row, batches):
        res = _dot(jnp.concatenate([sr_scr[b].astype(bf16) for b in batches], axis=0), seg_b)
        for i, b in enumerate(batches):
            y = res[hd * i:hd * (i + 1)]
            y_scr[b, row, :] = jnp.sum(jnp.where(eye, y, 0.0), axis=0, keepdims=True)

    def step(t, carry):
        row = pl.ds(t, 1)
        prev_row = pl.ds(jnp.maximum(t - 1, 0), 1)
        fed = [feedback_dot(row, g) for g in groups]
        for g in groups:
            readout(prev_row, g)
        for g, f in zip(groups, fed):
            update(row, g, *f)
        return carry

    for b in range(nb):
        sr_scr[b] = jnp.zeros((hd, gw), f32)
    lax.fori_loop(0, n_steps, step, 0, unroll=RWKV_UNROLL)
    for g in groups:
        readout(pl.ds(n_steps - 1, 1), g)

    rows_all = (nb * tbr, gw)
    y = y_scr[...].reshape(rows_all)
    d = y - _dot_exact(y, seg_b) * (1.0 / HEAD_DIM)
    var = _dot_exact(d * d, seg_b) * (1.0 / HEAD_DIM)
    yn = d * lax.rsqrt(var + RWKV_LNX_EPS) * lg_ref[...] + lb_ref[...]
    o_ref[...] = ((yn + bonus_scr[...].reshape(rows_all)) * g_scr[...].reshape(rows_all)).reshape(per_batch)

    @pl.when(tb == pl.num_programs(1) - 1)
    def _fin():
        for b in range(nb):
            for h in range(N_HEADS):
                sfin_ref[b, h] = s_scr[b, :, HEAD_DIM * h:HEAD_DIM * (h + 1)]


def _rwkv(c, shift_prev, s0, p, seg_b, n_valid):
    bsz, t, _ = c.shape
    nb = RWKV_BATCH
    tbr = min(t, RWKV_TBLOCK)
    n_steps = min(n_valid, tbr)
    vec = lambda width: pl.BlockSpec((1, width), lambda b, i: (0, 0))
    mat = lambda rows: pl.BlockSpec((rows, GROUP_WIDTH), lambda b, i: (0, 0))
    rows_scr = pltpu.VMEM((nb, tbr, GROUP_WIDTH), f32)
    return pl.pallas_call(
        functools.partial(_rwkv_body, nb=nb, n_steps=n_steps),
        out_shape=(jax.ShapeDtypeStruct((bsz, t, GROUP_WIDTH), f32),
                   jax.ShapeDtypeStruct((bsz, N_HEADS, HEAD_DIM, HEAD_DIM), f32)),
        grid=(bsz // nb, t // tbr),
        in_specs=[pl.BlockSpec((nb, tbr, RWKV_COLS), lambda b, i: (b, i, 0)),
                  pl.BlockSpec((nb, 1, RWKV_COLS), lambda b, i: (b, 0, 0)),
                  pl.BlockSpec((nb, N_HEADS, HEAD_DIM, HEAD_DIM), lambda b, i: (b, 0, 0, 0)),
                  vec(RWKV_COLS), vec(GROUP_WIDTH), mat(LANES), mat(LANES), vec(GROUP_WIDTH), mat(LANES), mat(LANES),
                  mat(RWKV_G_RANK), mat(RWKV_G_RANK),
                  vec(GROUP_WIDTH), vec(GROUP_WIDTH), vec(GROUP_WIDTH), vec(GROUP_WIDTH), vec(GROUP_WIDTH),
                  mat(GROUP_WIDTH)],
        out_specs=(pl.BlockSpec((nb, tbr, GROUP_WIDTH), lambda b, i: (b, i, 0)),
                   pl.BlockSpec((nb, N_HEADS, HEAD_DIM, HEAD_DIM), lambda b, i: (b, 0, 0, 0))),
        scratch_shapes=[pltpu.VMEM((nb, HEAD_DIM, GROUP_WIDTH), f32), pltpu.VMEM((nb, 1, RWKV_COLS), f32)]
        + [rows_scr] * 9 + [pltpu.VMEM((nb, HEAD_DIM, GROUP_WIDTH), f32)],
        compiler_params=_params("parallel", "arbitrary"),
        name="rwkv",
    )(c, shift_prev, s0, p["mu"], p["w0"], *p["w2"], p["a0"], *p["a2"], *p["g2"], p["kk"], p["ka"], p["rk"],
      p["lnx_g"], p["lnx_b"], seg_b)


def _merge_rows(acc, mrun, den, o_new, m_new, d_new):
    m = jnp.maximum(mrun, m_new)
    a_old, a_new = jnp.exp(mrun - m), jnp.exp(m_new - m)
    return acc * a_old + o_new * a_new, m, den * a_old + d_new * a_new


def _dil_prompt_body(x_ref, o_ref, kv0_ref, kv1_ref, kv2_ref, acc_scr, m_scr, d_scr, *, t):
    grp = pl.program_id(1)
    blk = LANES
    n_pairs = GROUP_WIDTH // LANES
    rr, cc = _iota((blk, blk), 0), _iota((blk, blk), 1)
    first_head = _iota((1, LANES), 1) < HEAD_DIM
    neg_inf = float("-inf")
    kv_refs = (kv0_ref, kv1_ref, kv2_ref)

    def group(gi, window, dilation, first):
        n_sub = t // dilation // blk
        has_prev = n_sub > 1
        dist = (rr - cc).astype(f32) * float(dilation)
        valid = rr >= cc
        if has_prev:
            dist = jnp.concatenate([(rr + blk - cc).astype(f32) * float(dilation), dist], axis=1)
            valid = jnp.concatenate([cc >= rr, valid], axis=1)
            in_prev = _iota((1, 2 * blk), 1) < blk

        def keys_values(slab, rows, prows):
            cur = x_ref[0, slab, rows, :].astype(bf16)
            return jnp.concatenate([x_ref[0, slab, prows, :].astype(bf16), cur], axis=0) if has_prev else cur

        def scores(idx):
            res, sub = idx // n_sub, idx % n_sub
            start = res + dilation * blk * sub
            pstart = jnp.maximum(start - dilation * blk, 0)
            rows = pl.ds(start, blk, stride=dilation) if dilation > 1 else pl.ds(start, blk)
            prows = pl.ds(pstart, blk, stride=dilation) if dilation > 1 else pl.ds(pstart, blk)
            raw = []
            for pair in range(n_pairs):
                q = x_ref[0, pair, rows, :] * (HEAD_DIM ** -0.5)
                keys = keys_values(2 + pair, rows, prows)
                for hh in range(2):
                    lanes = first_head if hh == 0 else jnp.logical_not(first_head)
                    raw.append(_dot_tr(jnp.where(lanes, q, 0.0).astype(bf16), keys))
            return rows, prows, sub, raw

        def finish(rows, prows, sub, raw):
            if has_prev:
                gate = jnp.where(in_prev, jnp.where(sub > 0, 0.0, neg_inf), 0.0)
            for pair in range(n_pairs):
                vals = keys_values(4 + pair, rows, prows)
                vals = jnp.concatenate([vals, jnp.ones(vals.shape, bf16)], axis=1)
                o_pair = m_pair = d_pair = None
                for hh in range(2):
                    slope = _alibi_slope(gi * N_HEADS + 2 * pair + hh)
                    lanes = first_head if hh == 0 else jnp.logical_not(first_head)
                    s = jnp.where(valid, raw[2 * pair + hh] - slope * dist, neg_inf)
                    if has_prev:
                        s = s + gate
                    mx = jnp.max(s, axis=-1, keepdims=True)
                    pv = _dot(jnp.exp(s - mx).astype(bf16), vals)
                    o_h, d_h = pv[:, :LANES], pv[:, LANES:]
                    m_h = jnp.broadcast_to(mx, (blk, LANES))
                    if hh == 0:
                        o_pair, m_pair, d_pair = o_h, m_h, d_h
                    else:
                        o_pair = jnp.where(lanes, o_h, o_pair)
                        m_pair = jnp.where(lanes, m_h, m_pair)
                        d_pair = jnp.where(lanes, d_h, d_pair)
                if not first:
                    o_pair, m_pair, d_pair = _merge_rows(acc_scr[pair, rows, :], m_scr[pair, rows, :],
                                                         d_scr[pair, rows, :], o_pair, m_pair, d_pair)
                acc_scr[pair, rows, :] = o_pair
                m_scr[pair, rows, :] = m_pair
                d_scr[pair, rows, :] = d_pair

        def combos(i, carry):
            ctx = [scores(DIL_COMBOS * i + j) for j in range(DIL_COMBOS)]
            for c in ctx:
                finish(*c)
            return carry

        lax.fori_loop(0, t // blk // DIL_COMBOS, combos, 0)

        w = min(window, t)
        for j in range(2 * n_pairs):
            for c in range(w // blk):
                tile = x_ref[0, n_pairs + j, t - w + c * blk:t - w + (c + 1) * blk, :]
                kv_refs[gi][0, j * LANES:(j + 1) * LANES, c * blk:(c + 1) * blk] = tile.T

    for step, gi in enumerate(reversed(range(len(DIL_PAIRS)))):
        pl.when(grp == step)(functools.partial(group, gi, *DIL_PAIRS[gi], step == 0))

    @pl.when(grp == len(DIL_PAIRS) - 1)
    def _fin():
        o_ref[...] = acc_scr[...] / d_scr[...]


def _dil_prompt(x, bsz, t):
    n_pairs = GROUP_WIDTH // LANES
    scr = pltpu.VMEM((n_pairs, t, LANES), f32)
    kv_rows = [min(w, t) for w, _ in DIL_PAIRS]
    return pl.pallas_call(
        functools.partial(_dil_prompt_body, t=t),
        out_shape=(jax.ShapeDtypeStruct((n_pairs, bsz * t, LANES), f32),)
        + tuple(jax.ShapeDtypeStruct((bsz, 2 * GROUP_WIDTH, r), f32) for r in kv_rows),
        grid=(bsz, len(DIL_PAIRS)),
        in_specs=[pl.BlockSpec((1, DIL_SLABS, t, LANES), lambda b, g: (len(DIL_PAIRS) - 1 - g, 0, b, 0))],
        out_specs=(pl.BlockSpec((n_pairs, t, LANES), lambda b, g: (0, b, 0)),)
        + tuple(pl.BlockSpec((1, 2 * GROUP_WIDTH, r), lambda b, g: (b, 0, 0)) for r in kv_rows),
        scratch_shapes=[scr, scr, scr],
        compiler_params=_params("parallel", "arbitrary"),
        name="dil_prompt",
    )(x)


def _dil_sample_body(x_ref, c0_ref, c1_ref, c2_ref, o_ref, *, t_new):
    caches = (c0_ref, c1_ref, c2_ref)
    nq = SUBLANES
    per = nq // t_new
    first_head = _iota((1, LANES), 1) < HEAD_DIM
    neg_inf = float("-inf")
    q_idx = _iota((nq, 1), 0) & (t_new - 1)
    q_bat = _iota((nq, 1), 0) >> 2
    rn, cn = _iota((nq, nq), 0), _iota((nq, nq), 1)
    dn = (rn & (t_new - 1)) - (cn & (t_new - 1))
    same = (rn >> 2) == (cn >> 2)
    for pair in range(GROUP_WIDTH // LANES):
        acc = mrun = den = None
        for gi, (window, dilation) in enumerate(DIL_PAIRS):
            q = x_ref[gi, pair] * (HEAD_DIM ** -0.5)
            kn = x_ref[gi, 2 + pair].astype(bf16)
            vn = x_ref[gi, 4 + pair].astype(bf16)
            dist_c = window + q_idx - _iota((nq, window), 1)
            ok_c = jnp.logical_and((dist_c & (dilation - 1)) == 0, dist_c <= window)
            ok_n = jnp.logical_and(jnp.logical_and(same, dn >= 0), (dn & (dilation - 1)) == 0)
            q2 = jnp.concatenate([jnp.where(first_head, q, 0.0), jnp.where(first_head, 0.0, q)], axis=0).astype(bf16)
            slope2 = jnp.where(_iota((2 * nq, 1), 0) < nq, _alibi_slope(gi * N_HEADS + 2 * pair),
                               _alibi_slope(gi * N_HEADS + 2 * pair + 1))
            dist_c2 = jnp.concatenate([dist_c, dist_c], axis=0).astype(f32)
            ok_c2 = jnp.concatenate([ok_c, ok_c], axis=0)
            dn2 = jnp.concatenate([dn, dn], axis=0).astype(f32)
            ok_n2 = jnp.concatenate([ok_n, ok_n], axis=0)
            s_new = jnp.where(ok_n2, _dot_tr(q2, kn) - slope2 * dn2, neg_inf)
            mx_new = jnp.max(s_new, axis=-1, keepdims=True)
            o_pair = m_pair = d_pair = None
            for s in range(per):
                kt = caches[gi][0, s, pair * LANES:(pair + 1) * LANES, :].astype(bf16)
                vt = caches[gi][0, s, GROUP_WIDTH + pair * LANES:GROUP_WIDTH + (pair + 1) * LANES, :].astype(bf16)
                s_c = jnp.where(ok_c2, _dot(q2, kt) - slope2 * dist_c2, neg_inf)
                mx = jnp.maximum(jnp.max(s_c, axis=-1, keepdims=True), mx_new)
                p_c, p_n = jnp.exp(s_c - mx), jnp.exp(s_new - mx)
                o2 = _dot_tr(p_c.astype(bf16), vt) + _dot(p_n.astype(bf16), vn)
                d2 = jnp.sum(p_c, axis=-1, keepdims=True) + jnp.sum(p_n, axis=-1, keepdims=True)
                o_s = jnp.where(first_head, o2[:nq], o2[nq:])
                m_s = jnp.where(first_head, jnp.broadcast_to(mx[:nq], (nq, LANES)), jnp.broadcast_to(mx[nq:], (nq, LANES)))
                d_s = jnp.where(first_head, jnp.broadcast_to(d2[:nq], (nq, LANES)), jnp.broadcast_to(d2[nq:], (nq, LANES)))
                if s == 0:
                    o_pair, m_pair, d_pair = o_s, m_s, d_s
                else:
                    mine = q_bat == s
                    o_pair = jnp.where(mine, o_s, o_pair)
                    m_pair = jnp.where(mine, m_s, m_pair)
                    d_pair = jnp.where(mine, d_s, d_pair)
            if gi == 0:
                acc, mrun, den = o_pair, m_pair, d_pair
            else:
                acc, mrun, den = _merge_rows(acc, mrun, den, o_pair, m_pair, d_pair)
        o_ref[pair] = acc / den


def _dil_sample(x, caches, layer, bsz, t_new):
    n_pairs = GROUP_WIDTH // LANES
    per = SUBLANES // t_new
    return pl.pallas_call(
        functools.partial(_dil_sample_body, t_new=t_new),
        out_shape=jax.ShapeDtypeStruct((n_pairs, bsz * t_new, LANES), f32),
        grid=(bsz // per,),
        in_specs=[pl.BlockSpec((len(DIL_PAIRS), DIL_SLABS, SUBLANES, LANES), lambda i: (0, 0, i, 0))]
        + [pl.BlockSpec((1, per, 2 * GROUP_WIDTH, w), lambda i: (layer, i, 0, 0)) for w, _ in DIL_PAIRS],
        out_specs=pl.BlockSpec((n_pairs, SUBLANES, LANES), lambda i: (0, i, 0)),
        compiler_params=_params("parallel"),
        name="dil_sample",
    )(x, *caches)


def _row(v):
    return v.reshape(1, -1)


def _prep_layer(l, norm_gains, w_ff_gate, w_ff_up, w_ff_down, w_in, w_out, gla_w_alpha2, gla_b_alpha, gla_norm,
                gmlp_ln_g, gmlp_ln_b, gmlp_ws, gmlp_bs, rwkv_mu, rwkv_w0, rwkv_w2, rwkv_a0, rwkv_a2, rwkv_g2,
                rwkv_kk, rwkv_ka, rwkv_rk, rwkv_lnx_g, rwkv_lnx_b):
    win = w_in[l]
    o1, o2, o3 = GLA_COLS, GLA_COLS + GMLP_COLS, GLA_COLS + GMLP_COLS + RWKV_COLS
    hk = N_HEADS * GLA_DK
    a_lo, a_hi = 2 * hk + GROUP_WIDTH, 2 * hk + GROUP_WIDTH + GLA_RANK
    w_gla = win[:, :o1]
    w_gla = jnp.concatenate([w_gla[:, :a_lo], w_gla[:, a_hi:], w_gla[:, a_lo:a_hi],
                             jnp.zeros((D_MODEL, GLA_COLS_PAD - GLA_COLS), f32)], axis=1)
    zeros_r = jnp.zeros((RWKV_W_RANK, GROUP_WIDTH), f32)
    return dict(
        norms=[_row(norm_gains[l, i]) for i in range(norm_gains.shape[1])],
        w_gla=w_gla.astype(bf16), w_gmlp=win[:, o1:o2].astype(bf16), w_rwkv=win[:, o2:o3].astype(bf16),
        w_dil=win[:, o3:].astype(bf16),
        gla_wa=_split_param(jnp.concatenate([gla_w_alpha2[l], jnp.zeros((LANES - GLA_RANK, hk), f32)], axis=0)),
        gla_ba=_row(gla_b_alpha[l]), gla_norm=_row(gla_norm[l]),
        gmlp_ln_g=_row(gmlp_ln_g[l]), gmlp_ln_b=_row(gmlp_ln_b[l]), gmlp_ws=gmlp_ws[l], gmlp_bs=gmlp_bs[l],
        rwkv=dict(mu=_row(rwkv_mu[l]), w0=_row(rwkv_w0[l]),
                  w2=_split_param(jnp.concatenate([rwkv_w2[l], zeros_r], axis=0)), a0=_row(rwkv_a0[l]),
                  a2=_split_param(jnp.concatenate([zeros_r, rwkv_a2[l]], axis=0)), g2=_split_param(rwkv_g2[l]),
                  kk=_row(rwkv_kk[l]), ka=_row(rwkv_ka[l]), rk=_row(rwkv_rk[l]),
                  lnx_g=_row(rwkv_lnx_g[l]), lnx_b=_row(rwkv_lnx_b[l])),
    )


def _kv_rows(c_dil, bsz, t, rows):
    out = []
    for gi in range(len(DIL_PAIRS)):
        kv = c_dil[gi, 2:].reshape(2, GROUP_WIDTH // LANES, bsz, t, LANES)[:, :, :, t - rows[gi]:]
        kv = jnp.transpose(kv, (2, 3, 0, 1, 4))
        out.append(kv.reshape(bsz, rows[gi], 2, N_HEADS, HEAD_DIM))
    return out


def _channel_major_kv(kv):
    bsz, _, rows = kv.shape
    return jnp.transpose(kv.reshape(bsz, 2, N_HEADS, HEAD_DIM, rows), (0, 4, 1, 2, 3))


def _trunk_layer(x, p, ffw, seg_b, bsz, t, gla_h0, rwkv_s0, shift0, caches, layer):
    ng = p["norms"]
    n = bsz * t
    h = _ffn(x, ng[0], ng[1], ffw["gate"], ffw["up"], ffw["down"], layer, 0)
    c_gla, c_gmlp, c_rwkv, c_dil = _proj(h, ng[2], p["w_gla"], p["w_gmlp"], p["w_rwkv"], p["w_dil"])
    is_prompt = caches is None

    t_gla = max(t, GLA_CHUNK)
    cg = c_gla.reshape(bsz, t, GLA_COLS_PAD)
    if t_gla != t:
        cg = jnp.pad(cg, ((0, 0), (0, t_gla - t), (0, 0)))
    o_gla, gla_s = _gla(cg, gla_h0, p["gla_wa"], p["gla_ba"], p["gla_norm"], min(t, GLA_CHUNK))
    o_gla = o_gla[:, :t].reshape(n, GROUP_WIDTH)

    if t >= GMLP_CHUNK:
        ws = p["gmlp_ws"]
        bias_rows = jnp.repeat(p["gmlp_bs"].T, HEAD_DIM, axis=1)
    else:
        per = GMLP_CHUNK // t
        eye = jnp.eye(per, dtype=f32)
        ws = jnp.einsum("ab,gij->gaibj", eye, p["gmlp_ws"][:, :t, :t]).reshape(N_HEADS, GMLP_CHUNK, GMLP_CHUNK)
        bias_rows = jnp.tile(jnp.repeat(p["gmlp_bs"][:, :t].T, HEAD_DIM, axis=1), (per, 1))
    gm = _gmlp(c_gmlp, p["gmlp_ln_g"], p["gmlp_ln_b"], ws, bias_rows, emit_v=not is_prompt)
    o_gmlp = gm[0]
    gmlp_v = None if is_prompt else gm[1]

    t_rw = max(t, SUBLANES)
    cr = c_rwkv.reshape(bsz, t, RWKV_COLS)
    shift = cr[:, -1]
    if t_rw != t:
        cr = jnp.pad(cr, ((0, 0), (0, t_rw - t), (0, 0)))
    o_rwkv, rwkv_s = _rwkv(cr, shift0.reshape(bsz, 1, RWKV_COLS), rwkv_s0, p["rwkv"], seg_b, t)
    o_rwkv = o_rwkv[:, :t].reshape(n, GROUP_WIDTH)

    if is_prompt:
        o_dil, *kv_cm = _dil_prompt(c_dil, bsz, t)
        kv_new = [_channel_major_kv(kv) for kv in kv_cm]
    else:
        o_dil = _dil_sample(c_dil, caches, layer, bsz, t)
        kv_new = _kv_rows(c_dil, bsz, t, [min(w, t) for w, _ in DIL_PAIRS])

    y = _ffn(h, ng[4], ng[5], ffw["gate"], ffw["up"], ffw["down"], layer, 1,
             mix=(o_gla, o_gmlp, o_rwkv, o_dil, ffw["out"], ng[3]))
    return y, gla_s, rwkv_s, shift, gmlp_v, kv_new


def kernel(x_prompt, x_sample, cache_win128, cache_win512, cache_win2048, state_gla, state_rwkv, state_shift,
           norm_gains, w_ff_gate, w_ff_up, w_ff_down, w_in, w_out, gla_w_alpha2, gla_b_alpha, gla_norm,
           gmlp_ln_g, gmlp_ln_b, gmlp_ws, gmlp_bs, rwkv_mu, rwkv_w0, rwkv_w2, rwkv_a0, rwkv_a2, rwkv_g2,
           rwkv_kk, rwkv_ka, rwkv_rk, rwkv_lnx_g, rwkv_lnx_b):
    bp, tp, _ = x_prompt.shape
    bs, ts, _ = x_sample.shape
    depth = norm_gains.shape[0]
    assert tp % GMLP_CHUNK == 0 and tp % (DIL_PAIRS[-1][1] * LANES) == 0 and bp % RWKV_BATCH == 0
    assert ts == 4 and bs % RWKV_BATCH == 0 and (bs * ts) % GMLP_CHUNK == 0
    caches = [jnp.transpose(c, (0, 1, 3, 4, 5, 2)).reshape(depth, bs, 2 * GROUP_WIDTH, c.shape[2])
              for c in (cache_win128, cache_win512, cache_win2048)]
    lanes = np.arange(GROUP_WIDTH)
    seg_b = jnp.asarray((lanes[:, None] // HEAD_DIM) == (lanes[None, :] // HEAD_DIM), dtype=bf16)
    ffw = dict(gate=w_ff_gate.astype(bf16), up=w_ff_up.astype(bf16), down=w_ff_down.astype(bf16),
               out=w_out.astype(bf16))
    gla0 = jnp.zeros((bp, N_HEADS, GLA_DK, HEAD_DIM), f32)
    rwkv0 =jnp.zeros((bp, N_HEADS, HEAD_DIM, HEAD_DIM), f32)
    shift0 = jnp.zeros((bp, RWKV_COLS), f32)
    yp = x_prompt.reshape(bp * tp, D_MODEL)
    ys = x_sample.reshape(bs * ts, D_MODEL)
    pg, pr, psh, pw = [], [], [], [[], [], []]
    sg, sr, ssh, sw, sv = [], [], [], [[], [], []], []
    for l in range(depth):
        p = _prep_layer(l, norm_gains, w_ff_gate, w_ff_up, w_ff_down, w_in, w_out, gla_w_alpha2, gla_b_alpha,
                        gla_norm, gmlp_ln_g, gmlp_ln_b, gmlp_ws, gmlp_bs, rwkv_mu, rwkv_w0, rwkv_w2, rwkv_a0,
                        rwkv_a2, rwkv_g2, rwkv_kk, rwkv_ka, rwkv_rk, rwkv_lnx_g, rwkv_lnx_b)
        yp, g_p, r_p, sh_p, _, kv_p = _trunk_layer(yp, p, ffw, seg_b, bp, tp, gla0, rwkv0, shift0, None, l)
        ys, g_s, r_s, sh_s, v_s, kv_s = _trunk_layer(ys, p, ffw, seg_b, bs, ts, state_gla[l], state_rwkv[l],
                                                     state_shift[l], caches, l)
        pg.append(g_p)
        pr.append(r_p)
        psh.append(sh_p)
        sg.append(g_s)
        sr.append(r_s)
        ssh.append(sh_s)
        sv.append(v_s.reshape(bs, ts, GROUP_WIDTH))
        for gi in range(len(DIL_PAIRS)):
            pw[gi].append(kv_p[gi])
            sw[gi].append(kv_s[gi])
    return (yp.reshape(bp, tp, D_MODEL), ys.reshape(bs, ts, D_MODEL),
            jnp.stack(pg), jnp.stack(pr), jnp.stack(psh),
            jnp.stack(pw[0]), jnp.stack(pw[1]), jnp.stack(pw[2]),
            jnp.stack(sg), jnp.stack(sr), jnp.stack(ssh),
            jnp.stack(sw[0]), jnp.stack(sw[1]), jnp.stack(sw[2]),
            jnp.stack(sv))
```

```python
import functools

import jax
import jax.numpy as jnp
import numpy as np
from jax import lax
from jax.experimental import pallas as pl
from jax.experimental.pallas import tpu as pltpu

f32 = jnp.float32
bf16 = jnp.bfloat16

D_MODEL = 1024
HEAD_DIM = 64
N_HEADS = 4
GROUP_WIDTH = N_HEADS * HEAD_DIM
D_FF = 2816
EPS = 1e-6
GLA_DK = 32
GLA_RANK = 16
GLA_TAU = 16.0
GLA_CHUNK = 64
GMLP_CHUNK = 128
LN_EPS = 1e-5
RWKV_W_RANK = 64
RWKV_A_RANK = 64
RWKV_G_RANK = 128
RWKV_LNX_EPS = 64e-5
DIL_PAIRS = ((128, 1), (512, 4), (2048, 16))
N_DIL_HEADS = len(DIL_PAIRS) * N_HEADS
ALIBI_MAX_EXP = 8.0
GLA_COLS = 2 * N_HEADS * GLA_DK + GROUP_WIDTH + GLA_RANK + GROUP_WIDTH
GMLP_COLS = 2 * GROUP_WIDTH
RWKV_COLS = 3 * GROUP_WIDTH + RWKV_W_RANK + RWKV_A_RANK + RWKV_G_RANK
DIL_COLS = len(DIL_PAIRS) * 3 * GROUP_WIDTH

LANES = 128
SUBLANES = 8
VMEM_LIMIT_BYTES = 56 * 1024 * 1024

GLA_COLS_PAD = 896
DIL_SLABS = DIL_COLS // LANES // len(DIL_PAIRS)
FFN_CHUNK = 256
ROW_TILE = 512
FFN_ROW_TILE = 512
FFN_SUBTILE = 512
GMLP_BLOCK_CHUNKS = 4
DIL_COMBOS = 2
GLA_SHORT_BATCHES = 4
GLA_BLOCK_CHUNKS = 8
RWKV_BATCH = 8
RWKV_TBLOCK = 128
RWKV_GROUP = 2
RWKV_UNROLL = 4


def _alibi_slope(head):
    return float(2.0 ** (-ALIBI_MAX_EXP * (head + 1.0) / N_DIL_HEADS))


def _iota(shape, dim):
    return lax.broadcasted_iota(jnp.int32, shape, dim)


def _dot(a, b, precision=None):
    return jnp.dot(a, b, preferred_element_type=f32, precision=precision)


def _dot_tr(a, b, precision=None):
    return lax.dot_general(a, b, (((1,), (1,)), ((), ())), preferred_element_type=f32, precision=precision)


def _dot_tl(a, b, precision=None):
    return lax.dot_general(a, b, (((0,), (0,)), ((), ())), preferred_element_type=f32, precision=precision)


def _rms_rows(x, g):
    ms = jnp.mean(x * x, axis=-1, keepdims=True)
    return x * lax.rsqrt(ms + EPS) * g


def _softplus(x):
    return jnp.maximum(x, 0.0) + jnp.log1p(jnp.exp(-jnp.abs(x)))


def _split(x, n):
    parts = []
    for _ in range(n - 1):
        p = x.astype(bf16)
        parts.append(p)
        x = x - p.astype(f32)
    parts.append(x.astype(bf16))
    return parts


def _split_param(w):
    hi = w.astype(bf16)
    return hi, (w - hi.astype(f32)).astype(bf16)


def _dot_exact(x, m_b, n=2):
    return functools.reduce(jnp.add, [_dot(p, m_b) for p in _split(x, n)])


def _dot_param(x, w_hi, w_lo):
    x_hi, x_lo = _split(x, 2)
    return _dot(x_hi, w_hi) + _dot(x_lo, w_hi) + _dot(x_hi, w_lo)


def _params(*semantics):
    return pltpu.CompilerParams(dimension_semantics=semantics, vmem_limit_bytes=VMEM_LIMIT_BYTES)


def _resident(shape):
    zeros = (0,) * len(shape)
    return pl.BlockSpec(shape, lambda *_: zeros, pipeline_mode=pl.Buffered(1))


def _ffn_body(*refs, with_mix):
    if with_mix:
        (h_ref, ogla_ref, ogmlp_ref, orwkv_ref, odil_ref, wout_ref, gmix_ref,
         gpre_ref, gpost_ref, wg_ref, wu_ref, wd_ref, o_ref) = refs
        x_ref = h_ref
    else:
        x_ref, gpre_ref, gpost_ref, wg_ref, wu_ref, wd_ref, o_ref = refs
    tm = x_ref.shape[0]
    sub = min(FFN_SUBTILE, tm)
    for s0 in range(0, tm, sub):
        rows = slice(s0, s0 + sub)
        x = x_ref[rows, :]
        if with_mix:
            gw = GROUP_WIDTH
            mix = _dot(ogla_ref[rows, :].astype(bf16), wout_ref[0, 0:gw, :])
            mix += _dot(ogmlp_ref[rows, :].astype(bf16), wout_ref[0, gw:2 * gw, :])
            mix += _dot(orwkv_ref[rows, :].astype(bf16), wout_ref[0, 2 * gw:3 * gw, :])
            for p in range(gw // LANES):
                mix += _dot(odil_ref[p, rows, :].astype(bf16),
                            wout_ref[0, 3 * gw + p * LANES:3 * gw + (p + 1) * LANES, :])
            x = x + _rms_rows(mix, gmix_ref[...])
        xn = _rms_rows(x, gpre_ref[...]).astype(bf16)
        acc = None
        for c in range(D_FF // FFN_CHUNK):
            lo, hi = c * FFN_CHUNK, (c + 1) * FFN_CHUNK
            g = _dot(xn, wg_ref[0, 0, :, lo:hi])
            u = _dot(xn, wu_ref[0, 0, :, lo:hi])
            act = (g * jax.nn.sigmoid(g) * u).astype(bf16)
            part = _dot(act, wd_ref[0, 0, lo:hi, :])
            acc = part if acc is None else acc + part
        o_ref[rows, :] = x + 0.5 * _rms_rows(acc, gpost_ref[...])


def _ffn(x, g_pre, g_post, wg, wu, wd, layer, idx, mix=None):
    n = x.shape[0]
    tm = min(FFN_ROW_TILE, n)
    row = pl.BlockSpec((tm, D_MODEL), lambda i: (i, 0))
    once = dict(pipeline_mode=pl.Buffered(1))
    w_specs = [pl.BlockSpec((1, 1, D_MODEL, D_FF), lambda i: (layer, idx, 0, 0), **once),
               pl.BlockSpec((1, 1, D_MODEL, D_FF), lambda i: (layer, idx, 0, 0), **once),
               pl.BlockSpec((1, 1, D_FF, D_MODEL), lambda i: (layer, idx, 0, 0), **once)]
    in_specs, args = [row], [x]
    if mix is not None:
        o_gla, o_gmlp, o_rwkv, o_dil, w_out, g_mix = mix
        grp = pl.BlockSpec((tm, GROUP_WIDTH), lambda i: (i, 0))
        in_specs += [grp, grp, grp, pl.BlockSpec((GROUP_WIDTH // LANES, tm, LANES), lambda i: (0, i, 0)),
                     pl.BlockSpec((1, D_MODEL, D_MODEL), lambda i: (layer, 0, 0), **once), _resident((1, D_MODEL))]
        args += [o_gla, o_gmlp, o_rwkv, o_dil, w_out, g_mix]
    return pl.pallas_call(
        functools.partial(_ffn_body, with_mix=mix is not None),
        out_shape=jax.ShapeDtypeStruct((n, D_MODEL), f32),
        grid=(n // tm,),
        in_specs=in_specs + [_resident((1, D_MODEL)), _resident((1, D_MODEL))] + w_specs,
        out_specs=row,
        compiler_params=_params("parallel"),
        name="ffn_mix" if mix is not None else "ffn",
    )(*args, g_pre, g_post, wg, wu, wd)


def _proj_body(h_ref, g_ref, wgla_ref, wgmlp_ref, wrwkv_ref, wdil_ref, ogla_ref, ogmlp_ref, orwkv_ref, odil_ref):
    xn = _rms_rows(h_ref[...], g_ref[...]).astype(bf16)
    ogla_ref[...] = _dot(xn, wgla_ref[...])
    ogmlp_ref[...] = _dot(xn, wgmlp_ref[...])
    orwkv_ref[...] = _dot(xn, wrwkv_ref[...])
    res = _dot(xn, wdil_ref[...])
    for s in range(DIL_COLS // LANES):
        odil_ref[s // DIL_SLABS, s % DIL_SLABS] = res[:, s * LANES:(s + 1) * LANES]


def _proj(h, g, w_gla, w_gmlp, w_rwkv, w_dil):
    n = h.shape[0]
    tm = min(ROW_TILE, n)
    n_grp = len(DIL_PAIRS)
    return pl.pallas_call(
        _proj_body,
        out_shape=(jax.ShapeDtypeStruct((n, GLA_COLS_PAD), f32),
                   jax.ShapeDtypeStruct((n, GMLP_COLS), f32),
                   jax.ShapeDtypeStruct((n, RWKV_COLS), f32),
                   jax.ShapeDtypeStruct((n_grp, DIL_SLABS, n, LANES), f32)),
        grid=(n // tm,),
        in_specs=[pl.BlockSpec((tm, D_MODEL), lambda i: (i, 0)), _resident((1, D_MODEL)),
                  _resident((D_MODEL, GLA_COLS_PAD)), _resident((D_MODEL, GMLP_COLS)),
                  _resident((D_MODEL, RWKV_COLS)), _resident((D_MODEL, DIL_COLS))],
        out_specs=(pl.BlockSpec((tm, GLA_COLS_PAD), lambda i: (i, 0)),
                   pl.BlockSpec((tm, GMLP_COLS), lambda i: (i, 0)),
                   pl.BlockSpec((tm, RWKV_COLS), lambda i: (i, 0)),
                   pl.BlockSpec((n_grp, DIL_SLABS, tm, LANES), lambda i: (0, 0, i, 0))),
        compiler_params=_params("parallel"),
        name="proj",
    )(h, g, w_gla, w_gmlp, w_rwkv, w_dil)


def _gla_body(c_ref, h0_ref, wa_hi_ref, wa_lo_ref, ba_ref, ng_ref, o_ref, sfin_ref, s_scr, *, n_chunks, valid_rows,
              n_batch):
    C = GLA_CHUNK
    hk, hv = N_HEADS * GLA_DK, GROUP_WIDTH
    tb = pl.program_id(1)

    rr, cc = _iota((C, C), 0), _iota((C, C), 1)
    causal = rr >= cc
    tril = causal.astype(bf16)
    k_head = _iota((1, hk), 1) >> 5
    v_head = _iota((1, hv), 1) >> 6
    diag_blocks = (_iota((hk, hv), 0) >> 5) == (_iota((hk, hv), 1) >> 6)
    seg_b = ((_iota((hv, hv), 0) >> 6) == (_iota((hv, hv), 1) >> 6)).astype(bf16)
    ones_cv = jnp.ones((C, hv), bf16)

    @pl.when(tb == 0)
    def _init():
        for bi in range(n_batch):
            s_scr[bi] = jnp.zeros((hk, hv), f32)
            for h in range(N_HEADS):
                s_scr[bi, GLA_DK * h:GLA_DK * (h + 1), HEAD_DIM * h:HEAD_DIM * (h + 1)] = h0_ref[bi, h]

    stashes = []
    for bi in range(n_batch):
        z = _dot_param(c_ref[bi, :, 2 * hk + 2 * hv:], wa_hi_ref[...], wa_lo_ref[...]) + ba_ref[...]
        log_a_all = -_softplus(-z) * (1.0 / GLA_TAU)
        if valid_rows < C:
            log_a_all = jnp.where(_iota(log_a_all.shape, 0) < valid_rows, log_a_all, 0.0)
        stash = []
        for ci in range(n_chunks):
            x = c_ref[bi, ci * C:(ci + 1) * C, :]
            q = x[:, 0:hk] * (GLA_DK ** -0.5)
            k = x[:, hk:2 * hk]
            v = x[:, 2 * hk:2 * hk + hv]
            la_parts = _split(log_a_all[ci * C:(ci + 1) * C], 3)
            b = functools.reduce(jnp.add, [_dot(tril, p) for p in la_parts])
            b_last = b[C - 1:C, :]
            qe = q * jnp.exp(b)
            keb = (k * jnp.exp(-b)).astype(bf16)
            kd = k * jnp.exp(b_last - b)
            o_intra = None
            for h in range(N_HEADS):
                att = _dot_tr(jnp.where(k_head == h, qe, 0.0).astype(bf16), keb)
                att = jnp.where(causal, att, 0.0)
                part = _dot(att.astype(bf16), jnp.where(v_head == h, v, 0.0).astype(bf16))
                o_intra = part if o_intra is None else o_intra + part
            incr = jnp.where(diag_blocks, _dot_tl(kd.astype(bf16), v.astype(bf16)), 0.0)
            chunk_log_decay = functools.reduce(jnp.add, [_dot_tl(p, ones_cv) for p in la_parts])
            stash.append((qe.astype(bf16), o_intra, incr, jnp.exp(chunk_log_decay)))
        stashes.append(stash)

    outs_all = []
    for bi, stash in enumerate(stashes):
        state = s_scr[bi]
        outs = []
        for qe_b, o_intra, incr, decay in stash:
            outs.append(o_intra + _dot(qe_b, state.astype(bf16)))
            state = decay * state + incr
        s_scr[bi] = state
        outs_all.append(outs)

    for bi, outs in enumerate(outs_all):
        for ci, o in enumerate(outs):
            g = c_ref[bi, ci * C:(ci + 1) * C, 2 * hk + hv:2 * hk + 2 * hv]
            ms = _dot_exact(o * o, seg_b) * (1.0 / HEAD_DIM)
            o_ref[bi, ci * C:(ci + 1) * C, :] = o * lax.rsqrt(ms + EPS) * ng_ref[...] * (g * jax.nn.sigmoid(g))

    @pl.when(tb == pl.num_programs(1) - 1)
    def _fin():
        for bi in range(n_batch):
            for h in range(N_HEADS):
                sfin_ref[bi, h] = s_scr[bi, GLA_DK * h:GLA_DK * (h + 1), HEAD_DIM * h:HEAD_DIM * (h + 1)]


def _gla(c, h0, wa_pad, b_alpha, norm_g, valid_rows):
    bsz, t, _ = c.shape
    tb = min(t, GLA_BLOCK_CHUNKS * GLA_CHUNK)
    nb = GLA_SHORT_BATCHES if t == GLA_CHUNK and bsz % GLA_SHORT_BATCHES == 0 else 1
    wa_spec = pl.BlockSpec((LANES, N_HEADS * GLA_DK), lambda b, i: (0, 0))
    return pl.pallas_call(
        functools.partial(_gla_body, n_chunks=tb // GLA_CHUNK, valid_rows=valid_rows, n_batch=nb),
        out_shape=(jax.ShapeDtypeStruct((bsz, t, GROUP_WIDTH), f32),
                   jax.ShapeDtypeStruct((bsz, N_HEADS, GLA_DK, HEAD_DIM), f32)),
        grid=(bsz // nb, t // tb),
        in_specs=[pl.BlockSpec((nb, tb, GLA_COLS_PAD), lambda b, i: (b, i, 0)),
                  pl.BlockSpec((nb, N_HEADS, GLA_DK, HEAD_DIM), lambda b, i: (b, 0, 0, 0)),
                  wa_spec, wa_spec,
                  pl.BlockSpec((1, N_HEADS * GLA_DK), lambda b, i: (0, 0)),
                  pl.BlockSpec((1, GROUP_WIDTH), lambda b, i: (0, 0))],
        out_specs=(pl.BlockSpec((nb, tb, GROUP_WIDTH), lambda b, i: (b, i, 0)),
                   pl.BlockSpec((nb, N_HEADS, GLA_DK, HEAD_DIM), lambda b, i: (b, 0, 0, 0))),
        scratch_shapes=[pltpu.VMEM((nb, N_HEADS * GLA_DK, GROUP_WIDTH), f32)],
        compiler_params=_params("parallel", "arbitrary"),
        name="gla",
    )(c, h0, *wa_pad, b_alpha, norm_g)


def _gmlp_body(c_ref, lng_ref, lnb_ref, ws_ref, bias_ref, o_ref, *v_refs):
    gw = GROUP_WIDTH
    x = c_ref[...]
    ge = 0.5 * x * (1.0 + lax.erf(x * float(np.sqrt(0.5))))
    u, v = ge[:, :gw], ge[:, gw:]
    mean = jnp.mean(v, axis=-1, keepdims=True)
    d = v - mean
    var = jnp.mean(d * d, axis=-1, keepdims=True)
    vn = d * lax.rsqrt(var + LN_EPS) * lng_ref[...] + lnb_ref[...]
    n = GMLP_CHUNK
    causal = _iota((n, n), 0) >= _iota((n, n), 1)
    v_head = _iota((1, gw), 1) >> 6
    w_all = jnp.concatenate([jnp.where(causal, ws_ref[grp], 0.0).astype(bf16) for grp in range(N_HEADS)], axis=0)
    vb = vn.astype(bf16)
    for ci in range(x.shape[0] // n):
        rows = slice(ci * n, (ci + 1) * n)
        res = _dot(w_all, vb[rows])
        s = res[0:n]
        for grp in range(1, N_HEADS):
            s = jnp.where(v_head == grp, res[grp * n:(grp + 1) * n], s)
        o_ref[rows, :] = u[rows] * (s + bias_ref[...])
    if v_refs:
        v_refs[0][...] = vn


def _gmlp(c, ln_g, ln_b, ws, bias_rows, emit_v):
    n = c.shape[0]
    rows = min(n, GMLP_BLOCK_CHUNKS * GMLP_CHUNK)
    blk = pl.BlockSpec((rows, GROUP_WIDTH), lambda i: (i, 0))
    out_shape = [jax.ShapeDtypeStruct((n, GROUP_WIDTH), f32)]
    out_specs = [blk]
    if emit_v:
        out_shape.append(jax.ShapeDtypeStruct((n, GROUP_WIDTH), f32))
        out_specs.append(blk)
    return pl.pallas_call(
        _gmlp_body,
        out_shape=tuple(out_shape),
        grid=(n // rows,),
        in_specs=[pl.BlockSpec((rows, GMLP_COLS), lambda i: (i, 0)),
                  pl.BlockSpec((1, GROUP_WIDTH), lambda i: (0, 0)),
                  pl.BlockSpec((1, GROUP_WIDTH), lambda i: (0, 0)),
                  pl.BlockSpec((N_HEADS, GMLP_CHUNK, GMLP_CHUNK), lambda i: (0, 0, 0)),
                  pl.BlockSpec((GMLP_CHUNK, GROUP_WIDTH), lambda i: (0, 0))],
        out_specs=tuple(out_specs),
        compiler_params=_params("parallel"),
        name="gmlp",
    )(c, ln_g, ln_b, ws, bias_rows)


def _rwkv_body(c_ref, sp_ref, s0_ref, mu_ref, w0_ref, w2h_ref, w2l_ref, a0_ref, a2h_ref, a2l_ref, g2h_ref, g2l_ref,
               kk_ref, ka_ref, rk_ref, lg_ref, lb_ref, seg_ref, o_ref, sfin_ref,
               s_scr, carry_scr, r_scr, w_scr, k_scr, v_scr, a_scr, b_scr, y_scr, g_scr, bonus_scr, sr_scr,
               *, nb, n_steps):
    gw = GROUP_WIDTH
    tbr = c_ref.shape[1]
    tb = pl.program_id(1)
    seg_b = seg_ref[...]

    @pl.when(tb == 0)
    def _init():
        for b in range(nb):
            carry_scr[b] = sp_ref[b]
            for h in range(N_HEADS):
                s_scr[b, :, HEAD_DIM * h:HEAD_DIM * (h + 1)] = s0_ref[b, h]

    first_row = _iota((tbr, RWKV_COLS), 0) == 0
    mixed = []
    for b in range(nb):
        c = c_ref[b]
        prev = jnp.where(first_row, carry_scr[b], pltpu.roll(c, 1, 0))
        carry_scr[b] = c[tbr - 1:tbr, :]
        mixed.append(c + (prev - c) * mu_ref[...])
    xm = jnp.concatenate(mixed, axis=0)
    r = xm[:, 0:gw]
    k = xm[:, gw:2 * gw]
    v = xm[:, 2 * gw:3 * gw]
    wa_lr = xm[:, 3 * gw:3 * gw + LANES]
    g_lr = xm[:, 3 * gw + LANES:]
    w = -_softplus(-(w0_ref[...] + _dot_param(jnp.tanh(wa_lr), w2h_ref[...], w2l_ref[...]))) - 0.5
    a = jax.nn.sigmoid(a0_ref[...] + _dot_param(wa_lr, a2h_ref[...], a2l_ref[...]))
    kk = k * kk_ref[...]
    kk = kk / jnp.maximum(jnp.sqrt(_dot_exact(kk * kk, seg_b)), 1e-12)
    k2 = k * (1.0 + (a - 1.0) * ka_ref[...])
    per_batch = (nb, tbr, gw)
    r_scr[...] = r.reshape(per_batch)
    w_scr[...] = jnp.exp(-jnp.exp(w)).reshape(per_batch)
    k_scr[...] = k2.reshape(per_batch)
    v_scr[...] = v.reshape(per_batch)
    a_scr[...] = (-kk).reshape(per_batch)
    b_scr[...] = (kk * a).reshape(per_batch)
    g_scr[...] = _dot_param(jax.nn.sigmoid(g_lr), g2h_ref[...], g2l_ref[...]).reshape(per_batch)
    bonus_scr[...] = (_dot_exact(r * k2 * rk_ref[...], seg_b) * v).reshape(per_batch)
    y_scr[...] = jnp.zeros(per_batch, f32)

    eye = (_iota((HEAD_DIM, gw), 1) & (HEAD_DIM - 1)) == _iota((HEAD_DIM, gw), 0)

    hd = HEAD_DIM

    groups = [list(range(g0, g0 + RWKV_GROUP)) for g0 in range(0, nb, RWKV_GROUP)]

    def feedback_dot(row, batches):
        states, parts = [], []
        for b in batches:
            state = s_scr[b]
            sa_in = state * a_scr[b, row, :]
            parts += [sa_in.astype(bf16), jnp.where(eye, v_scr[b, row, :], 0.0).astype(bf16)]
            states.append(state)
        return states, _dot(jnp.concatenate(parts, axis=0), seg_b)

    def update(row, batches, states, res):
        for i, b in enumerate(batches):
            base = 2 * hd * i
            sa = res[base:base + hd]
            v_col = res[base + hd:base + 2 * hd]
            state = states[i] * w_scr[b, row, :] + sa * b_scr[b, row, :] + v_col * k_scr[b, row, :]
            s_scr[b] = state
            sr_scr[b] = state * r_scr[b, row, :]

    def readout(row, batches):
        res = _dot(jnp.concatenate([sr_scr[b].astype(bf16) for b in batches], axis=0), seg_b)
        for i, b in enumerate(batches):
            y = res[hd * i:hd * (i + 1)]
            y_scr[b, row, :] = jnp.sum(jnp.where(eye, y, 0.0), axis=0, keepdims=True)

    def step(t, carry):
        row = pl.ds(t, 1)
        prev_row = pl.ds(jnp.maximum(t - 1, 0), 1)
        fed = [feedback_dot(row, g) for g in groups]
        for g in groups:
            readout(prev_row, g)
        for g, f in zip(groups, fed):
            update(row, g, *f)
        return carry

    for b in range(nb):
        sr_scr[b] = jnp.zeros((hd, gw), f32)
    lax.fori_loop(0, n_steps, step, 0, unroll=RWKV_UNROLL)
    for g in groups:
        readout(pl.ds(n_steps - 1, 1), g)

    rows_all = (nb * tbr, gw)
    y = y_scr[...].reshape(rows_all)
    d = y - _dot_exact(y, seg_b) * (1.0 / HEAD_DIM)
    var = _dot_exact(d * d, seg_b) * (1.0 / HEAD_DIM)
    yn = d * lax.rsqrt(var + RWKV_LNX_EPS) * lg_ref[...] + lb_ref[...]
    o_ref[...] = ((yn + bonus_scr[...].reshape(rows_all)) * g_scr[...].reshape(rows_all)).reshape(per_batch)

    @pl.when(tb == pl.num_programs(1) - 1)
    def _fin():
        for b in range(nb):
            for h in range(N_HEADS):
                sfin_ref[b, h] = s_scr[b, :, HEAD_DIM * h:HEAD_DIM * (h + 1)]


def _rwkv(c, shift_prev, s0, p, seg_b, n_valid):
    bsz, t, _ = c.shape
    nb = RWKV_BATCH
    tbr = min(t, RWKV_TBLOCK)
    n_steps = min(n_valid, tbr)
    vec = lambda width: pl.BlockSpec((1, width), lambda b, i: (0, 0))
    mat = lambda rows: pl.BlockSpec((rows, GROUP_WIDTH), lambda b, i: (0, 0))
    rows_scr = pltpu.VMEM((nb, tbr, GROUP_WIDTH), f32)
    return pl.pallas_call(
        functools.partial(_rwkv_body, nb=nb, n_steps=n_steps),
        out_shape=(jax.ShapeDtypeStruct((bsz, t, GROUP_WIDTH), f32),
                   jax.ShapeDtypeStruct((bsz, N_HEADS, HEAD_DIM, HEAD_DIM), f32)),
        grid=(bsz // nb, t // tbr),
        in_specs=[pl.BlockSpec((nb, tbr, RWKV_COLS), lambda b, i: (b, i, 0)),
                  pl.BlockSpec((nb, 1, RWKV_COLS), lambda b, i: (b, 0, 0)),
                  pl.BlockSpec((nb, N_HEADS, HEAD_DIM, HEAD_DIM), lambda b, i: (b, 0, 0, 0)),
                  vec(RWKV_COLS), vec(GROUP_WIDTH), mat(LANES), mat(LANES), vec(GROUP_WIDTH), mat(LANES), mat(LANES),
                  mat(RWKV_G_RANK), mat(RWKV_G_RANK),
                  vec(GROUP_WIDTH), vec(GROUP_WIDTH), vec(GROUP_WIDTH), vec(GROUP_WIDTH), vec(GROUP_WIDTH),
                  mat(GROUP_WIDTH)],
        out_specs=(pl.BlockSpec((nb, tbr, GROUP_WIDTH), lambda b, i: (b, i, 0)),
                   pl.BlockSpec((nb, N_HEADS, HEAD_DIM, HEAD_DIM), lambda b, i: (b, 0, 0, 0))),
        scratch_shapes=[pltpu.VMEM((nb, HEAD_DIM, GROUP_WIDTH), f32), pltpu.VMEM((nb, 1, RWKV_COLS), f32)]
        + [rows_scr] * 9 + [pltpu.VMEM((nb, HEAD_DIM, GROUP_WIDTH), f32)],
        compiler_params=_params("parallel", "arbitrary"),
        name="rwkv",
    )(c, shift_prev, s0, p["mu"], p["w0"], *p["w2"], p["a0"], *p["a2"], *p["g2"], p["kk"], p["ka"], p["rk"],
      p["lnx_g"], p["lnx_b"], seg_b)


def _merge_rows(acc, mrun, den, o_new, m_new, d_new):
    m = jnp.maximum(mrun, m_new)
    a_old, a_new = jnp.exp(mrun - m), jnp.exp(m_new - m)
    return acc * a_old + o_new * a_new, m, den * a_old + d_new * a_new


def _dil_prompt_body(x_ref, o_ref, kv0_ref, kv1_ref, kv2_ref, acc_scr, m_scr, d_scr, *, t):
    grp = pl.program_id(1)
    blk = LANES
    n_pairs = GROUP_WIDTH // LANES
    rr, cc = _iota((blk, blk), 0), _iota((blk, blk), 1)
    first_head = _iota((1, LANES), 1) < HEAD_DIM
    neg_inf = float("-inf")
    kv_refs = (kv0_ref, kv1_ref, kv2_ref)

    def group(gi, window, dilation, first):
        n_sub = t // dilation // blk
        has_prev = n_sub > 1
        dist = (rr - cc).astype(f32) * float(dilation)
        valid = rr >= cc
        if has_prev:
            dist = jnp.concatenate([(rr + blk - cc).astype(f32) * float(dilation), dist], axis=1)
            valid = jnp.concatenate([cc >= rr, valid], axis=1)
            in_prev = _iota((1, 2 * blk), 1) < blk

        def keys_values(slab, rows, prows):
            cur = x_ref[0, slab, rows, :].astype(bf16)
            return jnp.concatenate([x_ref[0, slab, prows, :].astype(bf16), cur], axis=0) if has_prev else cur

        def scores(idx):
            res, sub = idx // n_sub, idx % n_sub
            start = res + dilation * blk * sub
            pstart = jnp.maximum(start - dilation * blk, 0)
            rows = pl.ds(start, blk, stride=dilation) if dilation > 1 else pl.ds(start, blk)
            prows = pl.ds(pstart, blk, stride=dilation) if dilation > 1 else pl.ds(pstart, blk)
            raw = []
            for pair in range(n_pairs):
                q = x_ref[0, pair, rows, :] * (HEAD_DIM ** -0.5)
                keys = keys_values(2 + pair, rows, prows)
                for hh in range(2):
                    lanes = first_head if hh == 0 else jnp.logical_not(first_head)
                    raw.append(_dot_tr(jnp.where(lanes, q, 0.0).astype(bf16), keys))
            return rows, prows, sub, raw

        def finish(rows, prows, sub, raw):
            if has_prev:
                gate = jnp.where(in_prev, jnp.where(sub > 0, 0.0, neg_inf), 0.0)
            for pair in range(n_pairs):
                vals = keys_values(4 + pair, rows, prows)
                vals = jnp.concatenate([vals, jnp.ones(vals.shape, bf16)], axis=1)
                o_pair = m_pair = d_pair = None
                for hh in range(2):
                    slope = _alibi_slope(gi * N_HEADS + 2 * pair + hh)
                    lanes = first_head if hh == 0 else jnp.logical_not(first_head)
                    s = jnp.where(valid, raw[2 * pair + hh] - slope * dist, neg_inf)
                    if has_prev:
                        s = s + gate
                    mx = jnp.max(s, axis=-1, keepdims=True)
                    pv = _dot(jnp.exp(s - mx).astype(bf16), vals)
                    o_h, d_h = pv[:, :LANES], pv[:, LANES:]
                    m_h = jnp.broadcast_to(mx, (blk, LANES))
                    if hh == 0:
                        o_pair, m_pair, d_pair = o_h, m_h, d_h
                    else:
                        o_pair = jnp.where(lanes, o_h, o_pair)
                        m_pair = jnp.where(lanes, m_h, m_pair)
                        d_pair = jnp.where(lanes, d_h, d_pair)
                if not first:
                    o_pair, m_pair, d_pair = _merge_rows(acc_scr[pair, rows, :], m_scr[pair, rows, :],
                                                         d_scr[pair, rows, :], o_pair, m_pair, d_pair)
                acc_scr[pair, rows, :] = o_pair
                m_scr[pair, rows, :] = m_pair
                d_scr[pair, rows, :] = d_pair

        def combos(i, carry):
            ctx = [scores(DIL_COMBOS * i + j) for j in range(DIL_COMBOS)]
            for c in ctx:
                finish(*c)
            return carry

        lax.fori_loop(0, t // blk // DIL_COMBOS, combos, 0)

        w = min(window, t)
        for j in range(2 * n_pairs):
            for c in range(w // blk):
                tile = x_ref[0, n_pairs + j, t - w + c * blk:t - w + (c + 1) * blk, :]
                kv_refs[gi][0, j * LANES:(j + 1) * LANES, c * blk:(c + 1) * blk] = tile.T

    for step, gi in enumerate(reversed(range(len(DIL_PAIRS)))):
        pl.when(grp == step)(functools.partial(group, gi, *DIL_PAIRS[gi], step == 0))

    @pl.when(grp == len(DIL_PAIRS) - 1)
    def _fin():
        o_ref[...] = acc_scr[...] / d_scr[...]


def _dil_prompt(x, bsz, t):
    n_pairs = GROUP_WIDTH // LANES
    scr = pltpu.VMEM((n_pairs, t, LANES), f32)
    kv_rows = [min(w, t) for w, _ in DIL_PAIRS]
    return pl.pallas_call(
        functools.partial(_dil_prompt_body, t=t),
        out_shape=(jax.ShapeDtypeStruct((n_pairs, bsz * t, LANES), f32),)
        + tuple(jax.ShapeDtypeStruct((bsz, 2 * GROUP_WIDTH, r), f32) for r in kv_rows),
        grid=(bsz, len(DIL_PAIRS)),
        in_specs=[pl.BlockSpec((1, DIL_SLABS, t, LANES), lambda b, g: (len(DIL_PAIRS) - 1 - g, 0, b, 0))],
        out_specs=(pl.BlockSpec((n_pairs, t, LANES), lambda b, g: (0, b, 0)),)
        + tuple(pl.BlockSpec((1, 2 * GROUP_WIDTH, r), lambda b, g: (b, 0, 0)) for r in kv_rows),
        scratch_shapes=[scr, scr, scr],
        compiler_params=_params("parallel", "arbitrary"),
        name="dil_prompt",
    )(x)


def _dil_sample_body(x_ref, c0_ref, c1_ref, c2_ref, o_ref, *, t_new):
    caches = (c0_ref, c1_ref, c2_ref)
    nq = SUBLANES
    per = nq // t_new
    first_head = _iota((1, LANES), 1) < HEAD_DIM
    neg_inf = float("-inf")
    q_idx = _iota((nq, 1), 0) & (t_new - 1)
    q_bat = _iota((nq, 1), 0) >> 2
    rn, cn = _iota((nq, nq), 0), _iota((nq, nq), 1)
    dn = (rn & (t_new - 1)) - (cn & (t_new - 1))
    same = (rn >> 2) == (cn >> 2)
    for pair in range(GROUP_WIDTH // LANES):
        acc = mrun = den = None
        for gi, (window, dilation) in enumerate(DIL_PAIRS):
            q = x_ref[gi, pair] * (HEAD_DIM ** -0.5)
            kn = x_ref[gi, 2 + pair].astype(bf16)
            vn = x_ref[gi, 4 + pair].astype(bf16)
            dist_c = window + q_idx - _iota((nq, window), 1)
            ok_c = jnp.logical_and((dist_c & (dilation - 1)) == 0, dist_c <= window)
            ok_n = jnp.logical_and(jnp.logical_and(same, dn >= 0), (dn & (dilation - 1)) == 0)
            q2 = jnp.concatenate([jnp.where(first_head, q, 0.0), jnp.where(first_head, 0.0, q)], axis=0).astype(bf16)
            slope2 = jnp.where(_iota((2 * nq, 1), 0) < nq, _alibi_slope(gi * N_HEADS + 2 * pair),
                               _alibi_slope(gi * N_HEADS + 2 * pair + 1))
            dist_c2 = jnp.concatenate([dist_c, dist_c], axis=0).astype(f32)
            ok_c2 = jnp.concatenate([ok_c, ok_c], axis=0)
            dn2 = jnp.concatenate([dn, dn], axis=0).astype(f32)
            ok_n2 = jnp.concatenate([ok_n, ok_n], axis=0)
            s_new = jnp.where(ok_n2, _dot_tr(q2, kn) - slope2 * dn2, neg_inf)
            mx_new = jnp.max(s_new, axis=-1, keepdims=True)
            o_pair = m_pair = d_pair = None
            for s in range(per):
                kt = caches[gi][0, s, pair * LANES:(pair + 1) * LANES, :].astype(bf16)
                vt = caches[gi][0, s, GROUP_WIDTH + pair * LANES:GROUP_WIDTH + (pair + 1) * LANES, :].astype(bf16)
                s_c = jnp.where(ok_c2, _dot(q2, kt) - slope2 * dist_c2, neg_inf)
                mx = jnp.maximum(jnp.max(s_c, axis=-1, keepdims=True), mx_new)
                p_c, p_n = jnp.exp(s_c - mx), jnp.exp(s_new - mx)
                o2 = _dot_tr(p_c.astype(bf16), vt) + _dot(p_n.astype(bf16), vn)
                d2 = jnp.sum(p_c, axis=-1, keepdims=True) + jnp.sum(p_n, axis=-1, keepdims=True)
                o_s = jnp.where(first_head, o2[:nq], o2[nq:])
                m_s = jnp.where(first_head, jnp.broadcast_to(mx[:nq], (nq, LANES)), jnp.broadcast_to(mx[nq:], (nq, LANES)))
                d_s = jnp.where(first_head, jnp.broadcast_to(d2[:nq], (nq, LANES)), jnp.broadcast_to(d2[nq:], (nq, LANES)))
                if s == 0:
                    o_pair, m_pair, d_pair = o_s, m_s, d_s
                else:
                    mine = q_bat == s
                    o_pair = jnp.where(mine, o_s, o_pair)
                    m_pair = jnp.where(mine, m_s, m_pair)
                    d_pair = jnp.where(mine, d_s, d_pair)
            if gi == 0:
                acc, mrun, den = o_pair, m_pair, d_pair
            else:
                acc, mrun, den = _merge_rows(acc, mrun, den, o_pair, m_pair, d_pair)
        o_ref[pair] = acc / den


def _dil_sample(x, caches, layer, bsz, t_new):
    n_pairs = GROUP_WIDTH // LANES
    per = SUBLANES // t_new
    return pl.pallas_call(
        functools.partial(_dil_sample_body, t_new=t_new),
        out_shape=jax.ShapeDtypeStruct((n_pairs, bsz * t_new, LANES), f32),
        grid=(bsz // per,),
        in_specs=[pl.BlockSpec((len(DIL_PAIRS), DIL_SLABS, SUBLANES, LANES), lambda i: (0, 0, i, 0))]
        + [pl.BlockSpec((1, per, 2 * GROUP_WIDTH, w), lambda i: (layer, i, 0, 0)) for w, _ in DIL_PAIRS],
        out_specs=pl.BlockSpec((n_pairs, SUBLANES, LANES), lambda i: (0, i, 0)),
        compiler_params=_params("parallel"),
        name="dil_sample",
    )(x, *caches)


def _row(v):
    return v.reshape(1, -1)


def _prep_layer(l, norm_gains, w_ff_gate, w_ff_up, w_ff_down, w_in, w_out, gla_w_alpha2, gla_b_alpha, gla_norm,
                gmlp_ln_g, gmlp_ln_b, gmlp_ws, gmlp_bs, rwkv_mu, rwkv_w0, rwkv_w2, rwkv_a0, rwkv_a2, rwkv_g2,
                rwkv_kk, rwkv_ka, rwkv_rk, rwkv_lnx_g, rwkv_lnx_b):
    win = w_in[l]
    o1, o2, o3 = GLA_COLS, GLA_COLS + GMLP_COLS, GLA_COLS + GMLP_COLS + RWKV_COLS
    hk = N_HEADS * GLA_DK
    a_lo, a_hi = 2 * hk + GROUP_WIDTH, 2 * hk + GROUP_WIDTH + GLA_RANK
    w_gla = win[:, :o1]
    w_gla = jnp.concatenate([w_gla[:, :a_lo], w_gla[:, a_hi:], w_gla[:, a_lo:a_hi],
                             jnp.zeros((D_MODEL, GLA_COLS_PAD - GLA_COLS), f32)], axis=1)
    zeros_r = jnp.zeros((RWKV_W_RANK, GROUP_WIDTH), f32)
    return dict(
        norms=[_row(norm_gains[l, i]) for i in range(norm_gains.shape[1])],
        w_gla=w_gla.astype(bf16), w_gmlp=win[:, o1:o2].astype(bf16), w_rwkv=win[:, o2:o3].astype(bf16),
        w_dil=win[:, o3:].astype(bf16),
        gla_wa=_split_param(jnp.concatenate([gla_w_alpha2[l], jnp.zeros((LANES - GLA_RANK, hk), f32)], axis=0)),
        gla_ba=_row(gla_b_alpha[l]), gla_norm=_row(gla_norm[l]),
        gmlp_ln_g=_row(gmlp_ln_g[l]), gmlp_ln_b=_row(gmlp_ln_b[l]), gmlp_ws=gmlp_ws[l], gmlp_bs=gmlp_bs[l],
        rwkv=dict(mu=_row(rwkv_mu[l]), w0=_row(rwkv_w0[l]),
                  w2=_split_param(jnp.concatenate([rwkv_w2[l], zeros_r], axis=0)), a0=_row(rwkv_a0[l]),
                  a2=_split_param(jnp.concatenate([zeros_r, rwkv_a2[l]], axis=0)), g2=_split_param(rwkv_g2[l]),
                  kk=_row(rwkv_kk[l]), ka=_row(rwkv_ka[l]), rk=_row(rwkv_rk[l]),
                  lnx_g=_row(rwkv_lnx_g[l]), lnx_b=_row(rwkv_lnx_b[l])),
    )


def _kv_rows(c_dil, bsz, t, rows):
    out = []
    for gi in range(len(DIL_PAIRS)):
        kv = c_dil[gi, 2:].reshape(2, GROUP_WIDTH // LANES, bsz, t, LANES)[:, :, :, t - rows[gi]:]
        kv = jnp.transpose(kv, (2, 3, 0, 1, 4))
        out.append(kv.reshape(bsz, rows[gi], 2, N_HEADS, HEAD_DIM))
    return out


def _channel_major_kv(kv):
    bsz, _, rows = kv.shape
    return jnp.transpose(kv.reshape(bsz, 2, N_HEADS, HEAD_DIM, rows), (0, 4, 1, 2, 3))


def _trunk_layer(x, p, ffw, seg_b, bsz, t, gla_h0, rwkv_s0, shift0, caches, layer):
    ng = p["norms"]
    n = bsz * t
    h = _ffn(x, ng[0], ng[1], ffw["gate"], ffw["up"], ffw["down"], layer, 0)
    c_gla, c_gmlp, c_rwkv, c_dil = _proj(h, ng[2], p["w_gla"], p["w_gmlp"], p["w_rwkv"], p["w_dil"])
    is_prompt = caches is None

    t_gla = max(t, GLA_CHUNK)
    cg = c_gla.reshape(bsz, t, GLA_COLS_PAD)
    if t_gla != t:
        cg = jnp.pad(cg, ((0, 0), (0, t_gla - t), (0, 0)))
    o_gla, gla_s = _gla(cg, gla_h0, p["gla_wa"], p["gla_ba"], p["gla_norm"], min(t, GLA_CHUNK))
    o_gla = o_gla[:, :t].reshape(n, GROUP_WIDTH)

    if t >= GMLP_CHUNK:
        ws = p["gmlp_ws"]
        bias_rows = jnp.repeat(p["gmlp_bs"].T, HEAD_DIM, axis=1)
    else:
        per = GMLP_CHUNK // t
        eye = jnp.eye(per, dtype=f32)
        ws = jnp.einsum("ab,gij->gaibj", eye, p["gmlp_ws"][:, :t, :t]).reshape(N_HEADS, GMLP_CHUNK, GMLP_CHUNK)
        bias_rows = jnp.tile(jnp.repeat(p["gmlp_bs"][:, :t].T, HEAD_DIM, axis=1), (per, 1))
    gm = _gmlp(c_gmlp, p["gmlp_ln_g"], p["gmlp_ln_b"], ws, bias_rows, emit_v=not is_prompt)
    o_gmlp = gm[0]
    gmlp_v = None if is_prompt else gm[1]

    t_rw = max(t, SUBLANES)
    cr = c_rwkv.reshape(bsz, t, RWKV_COLS)
    shift = cr[:, -1]
    if t_rw != t:
        cr = jnp.pad(cr, ((0, 0), (0, t_rw - t), (0, 0)))
    o_rwkv, rwkv_s = _rwkv(cr, shift0.reshape(bsz, 1, RWKV_COLS), rwkv_s0, p["rwkv"], seg_b, t)
    o_rwkv = o_rwkv[:, :t].reshape(n, GROUP_WIDTH)

    if is_prompt:
        o_dil, *kv_cm = _dil_prompt(c_dil, bsz, t)
        kv_new = [_channel_major_kv(kv) for kv in kv_cm]
    else:
        o_dil = _dil_sample(c_dil, caches, layer, bsz, t)
        kv_new = _kv_rows(c_dil, bsz, t, [min(w, t) for w, _ in DIL_PAIRS])

    y = _ffn(h, ng[4], ng[5], ffw["gate"], ffw["up"], ffw["down"], layer, 1,
             mix=(o_gla, o_gmlp, o_rwkv, o_dil, ffw["out"], ng[3]))
    return y, gla_s, rwkv_s, shift, gmlp_v, kv_new


def kernel(x_prompt, x_sample, cache_win128, cache_win512, cache_win2048, state_gla, state_rwkv, state_shift,
           norm_gains, w_ff_gate, w_ff_up, w_ff_down, w_in, w_out, gla_w_alpha2, gla_b_alpha, gla_norm,
           gmlp_ln_g, gmlp_ln_b, gmlp_ws, gmlp_bs, rwkv_mu, rwkv_w0, rwkv_w2, rwkv_a0, rwkv_a2, rwkv_g2,
           rwkv_kk, rwkv_ka, rwkv_rk, rwkv_lnx_g, rwkv_lnx_b):
    bp, tp, _ = x_prompt.shape
    bs, ts, _ = x_sample.shape
    depth = norm_gains.shape[0]
    assert tp % GMLP_CHUNK == 0 and tp % (DIL_PAIRS[-1][1] * LANES) == 0 and bp % RWKV_BATCH == 0
    assert ts == 4 and bs % RWKV_BATCH == 0 and (bs * ts) % GMLP_CHUNK == 0
    caches = [jnp.transpose(c, (0, 1, 3, 4, 5, 2)).reshape(depth, bs, 2 * GROUP_WIDTH, c.shape[2])
              for c in (cache_win128, cache_win512, cache_win2048)]
    lanes = np.arange(GROUP_WIDTH)
    seg_b = jnp.asarray((lanes[:, None] // HEAD_DIM) == (lanes[None, :] // HEAD_DIM), dtype=bf16)
    ffw = dict(gate=w_ff_gate.astype(bf16), up=w_ff_up.astype(bf16), down=w_ff_down.astype(bf16),
               out=w_out.astype(bf16))
    gla0 = jnp.zeros((bp, N_HEADS, GLA_DK, HEAD_DIM), f32)
    rwkv0 =jnp.zeros((bp, N_HEADS, HEAD_DIM, HEAD_DIM), f32)
    shift0 = jnp.zeros((bp, RWKV_COLS), f32)
    yp = x_prompt.reshape(bp * tp, D_MODEL)
    ys = x_sample.reshape(bs * ts, D_MODEL)
    pg, pr, psh, pw = [], [], [], [[], [], []]
    sg, sr, ssh, sw, sv = [], [], [], [[], [], []], []
    for l in range(depth):
        p = _prep_layer(l, norm_gains, w_ff_gate, w_ff_up, w_ff_down, w_in, w_out, gla_w_alpha2, gla_b_alpha,
                        gla_norm, gmlp_ln_g, gmlp_ln_b, gmlp_ws, gmlp_bs, rwkv_mu, rwkv_w0, rwkv_w2, rwkv_a0,
                        rwkv_a2, rwkv_g2, rwkv_kk, rwkv_ka, rwkv_rk, rwkv_lnx_g, rwkv_lnx_b)
        yp, g_p, r_p, sh_p, _, kv_p = _trunk_layer(yp, p, ffw, seg_b, bp, tp, gla0, rwkv0, shift0, None, l)
        ys, g_s, r_s, sh_s, v_s, kv_s = _trunk_layer(ys, p, ffw, seg_b, bs, ts, state_gla[l], state_rwkv[l],
                                                     state_shift[l], caches, l)
        pg.append(g_p)
        pr.append(r_p)
        psh.append(sh_p)
        sg.append(g_s)
        sr.append(r_s)
        ssh.append(sh_s)
        sv.append(v_s.reshape(bs, ts, GROUP_WIDTH))
        for gi in range(len(DIL_PAIRS)):
            pw[gi].append(kv_p[gi])
            sw[gi].append(kv_s[gi])
    return (yp.reshape(bp, tp, D_MODEL), ys.reshape(bs, ts, D_MODEL),
            jnp.stack(pg), jnp.stack(pr), jnp.stack(psh),
            jnp.stack(pw[0]), jnp.stack(pw[1]), jnp.stack(pw[2]),
            jnp.stack(sg), jnp.stack(sr), jnp.stack(ssh),
            jnp.stack(sw[0]), jnp.stack(sw[1]), jnp.stack(sw[2]),
            jnp.stack(sv))
```

```python
import functools

import jax
import jax.numpy as jnp
import numpy as np
from jax import lax
from jax.experimental import pallas as pl
from jax.experimental.pallas import tpu as pltpu

f32 = jnp.float32
bf16 = jnp.bfloat16

D_MODEL = 1024
HEAD_DIM = 64
N_HEADS = 4
GROUP_WIDTH = N_HEADS * HEAD_DIM
D_FF = 2816
EPS = 1e-6
GLA_DK = 32
GLA_RANK = 16
GLA_TAU = 16.0
GLA_CHUNK = 64
GMLP_CHUNK = 128
LN_EPS = 1e-5
RWKV_W_RANK = 64
RWKV_A_RANK = 64
RWKV_G_RANK = 128
RWKV_LNX_EPS = 64e-5
DIL_PAIRS = ((128, 1), (512, 4), (2048, 16))
N_DIL_HEADS = len(DIL_PAIRS) * N_HEADS
ALIBI_MAX_EXP = 8.0
GLA_COLS = 2 * N_HEADS * GLA_DK + GROUP_WIDTH + GLA_RANK + GROUP_WIDTH
GMLP_COLS = 2 * GROUP_WIDTH
RWKV_COLS = 3 * GROUP_WIDTH + RWKV_W_RANK + RWKV_A_RANK + RWKV_G_RANK
DIL_COLS = len(DIL_PAIRS) * 3 * GROUP_WIDTH

LANES = 128
SUBLANES = 8
VMEM_LIMIT_BYTES = 56 * 1024 * 1024

GLA_COLS_PAD = 896
DIL_SLABS = DIL_COLS // LANES // len(DIL_PAIRS)
FFN_CHUNK = 256
ROW_TILE = 512
FFN_ROW_TILE = 512
FFN_SUBTILE = 512
GMLP_BLOCK_CHUNKS = 4
DIL_COMBOS = 4
GLA_SHORT_BATCHES = 4
GLA_BLOCK_CHUNKS = 8
RWKV_BATCH = 8
RWKV_TBLOCK = 128
RWKV_GROUP = 2
RWKV_UNROLL = 4


def _alibi_slope(head):
    return float(2.0 ** (-ALIBI_MAX_EXP * (head + 1.0) / N_DIL_HEADS))


def _iota(shape, dim):
    return lax.broadcasted_iota(jnp.int32, shape, dim)


def _dot(a, b, precision=None):
    return jnp.dot(a, b, preferred_element_type=f32, precision=precision)


def _dot_tr(a, b, precision=None):
    return lax.dot_general(a, b, (((1,), (1,)), ((), ())), preferred_element_type=f32, precision=precision)


def _dot_tl(a, b, precision=None):
    return lax.dot_general(a, b, (((0,), (0,)), ((), ())), preferred_element_type=f32, precision=precision)


def _rms_rows(x, g):
    ms = jnp.mean(x * x, axis=-1, keepdims=True)
    return x * lax.rsqrt(ms + EPS) * g


def _softplus(x):
    return jnp.maximum(x, 0.0) + jnp.log1p(jnp.exp(-jnp.abs(x)))


def _split(x, n):
    parts = []
    for _ in range(n - 1):
        p = x.astype(bf16)
        parts.append(p)
        x = x - p.astype(f32)
    parts.append(x.astype(bf16))
    return parts


def _split_param(w):
    hi = w.astype(bf16)
    return hi, (w - hi.astype(f32)).astype(bf16)


def _dot_exact(x, m_b, n=2):
    return functools.reduce(jnp.add, [_dot(p, m_b) for p in _split(x, n)])


def _dot_param(x, w_hi, w_lo):
    x_hi, x_lo = _split(x, 2)
    return _dot(x_hi, w_hi) + _dot(x_lo, w_hi) + _dot(x_hi, w_lo)


def _params(*semantics):
    return pltpu.CompilerParams(dimension_semantics=semantics, vmem_limit_bytes=VMEM_LIMIT_BYTES)


def _resident(shape):
    zeros = (0,) * len(shape)
    return pl.BlockSpec(shape, lambda *_: zeros, pipeline_mode=pl.Buffered(1))


def _ffn_body(*refs, with_mix):
    if with_mix:
        (h_ref, ogla_ref, ogmlp_ref, orwkv_ref, odil_ref, wout_ref, gmix_ref,
         gpre_ref, gpost_ref, wg_ref, wu_ref, wd_ref, o_ref) = refs
        x_ref = h_ref
    else:
        x_ref, gpre_ref, gpost_ref, wg_ref, wu_ref, wd_ref, o_ref = refs
    tm = x_ref.shape[0]
    sub = min(FFN_SUBTILE, tm)
    for s0 in range(0, tm, sub):
        rows = slice(s0, s0 + sub)
        x = x_ref[rows, :]
        if with_mix:
            gw = GROUP_WIDTH
            mix = _dot(ogla_ref[rows, :].astype(bf16), wout_ref[0, 0:gw, :])
            mix += _dot(ogmlp_ref[rows, :].astype(bf16), wout_ref[0, gw:2 * gw, :])
            mix += _dot(orwkv_ref[rows, :].astype(bf16), wout_ref[0, 2 * gw:3 * gw, :])
            for p in range(gw // LANES):
                mix += _dot(odil_ref[p, rows, :].astype(bf16),
                            wout_ref[0, 3 * gw + p * LANES:3 * gw + (p + 1) * LANES, :])
            x = x + _rms_rows(mix, gmix_ref[...])
        xn = _rms_rows(x, gpre_ref[...]).astype(bf16)
        acc = None
        for c in range(D_FF // FFN_CHUNK):
            lo, hi = c * FFN_CHUNK, (c + 1) * FFN_CHUNK
            g = _dot(xn, wg_ref[0, 0, :, lo:hi])
            u = _dot(xn, wu_ref[0, 0, :, lo:hi])
            act = (g * jax.nn.sigmoid(g) * u).astype(bf16)
            part = _dot(act, wd_ref[0, 0, lo:hi, :])
            acc = part if acc is None else acc + part
        o_ref[rows, :] = x + 0.5 * _rms_rows(acc, gpost_ref[...])


def _ffn(x, g_pre, g_post, wg, wu, wd, layer, idx, mix=None):
    n = x.shape[0]
    tm = min(FFN_ROW_TILE, n)
    row = pl.BlockSpec((tm, D_MODEL), lambda i: (i, 0))
    once = dict(pipeline_mode=pl.Buffered(1))
    w_specs = [pl.BlockSpec((1, 1, D_MODEL, D_FF), lambda i: (layer, idx, 0, 0), **once),
               pl.BlockSpec((1, 1, D_MODEL, D_FF), lambda i: (layer, idx, 0, 0), **once),
               pl.BlockSpec((1, 1, D_FF, D_MODEL), lambda i: (layer, idx, 0, 0), **once)]
    in_specs, args = [row], [x]
    if mix is not None:
        o_gla, o_gmlp, o_rwkv, o_dil, w_out, g_mix = mix
        grp = pl.BlockSpec((tm, GROUP_WIDTH), lambda i: (i, 0))
        in_specs += [grp, grp, grp, pl.BlockSpec((GROUP_WIDTH // LANES, tm, LANES), lambda i: (0, i, 0)),
                     pl.BlockSpec((1, D_MODEL, D_MODEL), lambda i: (layer, 0, 0), **once), _resident((1, D_MODEL))]
        args += [o_gla, o_gmlp, o_rwkv, o_dil, w_out, g_mix]
    return pl.pallas_call(
        functools.partial(_ffn_body, with_mix=mix is not None),
        out_shape=jax.ShapeDtypeStruct((n, D_MODEL), f32),
        grid=(n // tm,),
        in_specs=in_specs + [_resident((1, D_MODEL)), _resident((1, D_MODEL))] + w_specs,
        out_specs=row,
        compiler_params=_params("parallel"),
        name="ffn_mix" if mix is not None else "ffn",
    )(*args, g_pre, g_post, wg, wu, wd)


def _proj_body(h_ref, g_ref, wgla_ref, wgmlp_ref, wrwkv_ref, wdil_ref, ogla_ref, ogmlp_ref, orwkv_ref, odil_ref):
    xn = _rms_rows(h_ref[...], g_ref[...]).astype(bf16)
    ogla_ref[...] = _dot(xn, wgla_ref[...])
    ogmlp_ref[...] = _dot(xn, wgmlp_ref[...])
    orwkv_ref[...] = _dot(xn, wrwkv_ref[...])
    res = _dot(xn, wdil_ref[...])
    for s in range(DIL_COLS // LANES):
        odil_ref[s // DIL_SLABS, s % DIL_SLABS] = res[:, s * LANES:(s + 1) * LANES]


def _proj(h, g, w_gla, w_gmlp, w_rwkv, w_dil):
    n = h.shape[0]
    tm = min(ROW_TILE, n)
    n_grp = len(DIL_PAIRS)
    return pl.pallas_call(
        _proj_body,
        out_shape=(jax.ShapeDtypeStruct((n, GLA_COLS_PAD), f32),
                   jax.ShapeDtypeStruct((n, GMLP_COLS), f32),
                   jax.ShapeDtypeStruct((n, RWKV_COLS), f32),
                   jax.ShapeDtypeStruct((n_grp, DIL_SLABS, n, LANES), f32)),
        grid=(n // tm,),
        in_specs=[pl.BlockSpec((tm, D_MODEL), lambda i: (i, 0)), _resident((1, D_MODEL)),
                  _resident((D_MODEL, GLA_COLS_PAD)), _resident((D_MODEL, GMLP_COLS)),
                  _resident((D_MODEL, RWKV_COLS)), _resident((D_MODEL, DIL_COLS))],
        out_specs=(pl.BlockSpec((tm, GLA_COLS_PAD), lambda i: (i, 0)),
                   pl.BlockSpec((tm, GMLP_COLS), lambda i: (i, 0)),
                   pl.BlockSpec((tm, RWKV_COLS), lambda i: (i, 0)),
                   pl.BlockSpec((n_grp, DIL_SLABS, tm, LANES), lambda i: (0, 0, i, 0))),
        compiler_params=_params("parallel"),
        name="proj",
    )(h, g, w_gla, w_gmlp, w_rwkv, w_dil)


def _gla_body(c_ref, h0_ref, wa_hi_ref, wa_lo_ref, ba_ref, ng_ref, o_ref, sfin_ref, s_scr, *, n_chunks, valid_rows,
              n_batch):
    C = GLA_CHUNK
    hk, hv = N_HEADS * GLA_DK, GROUP_WIDTH
    tb = pl.program_id(1)

    rr, cc = _iota((C, C), 0), _iota((C, C), 1)
    causal = rr >= cc
    tril = causal.astype(bf16)
    k_head = _iota((1, hk), 1) >> 5
    v_head = _iota((1, hv), 1) >> 6
    diag_blocks = (_iota((hk, hv), 0) >> 5) == (_iota((hk, hv), 1) >> 6)
    seg_b = ((_iota((hv, hv), 0) >> 6) == (_iota((hv, hv), 1) >> 6)).astype(bf16)
    ones_cv = jnp.ones((C, hv), bf16)

    @pl.when(tb == 0)
    def _init():
        for bi in range(n_batch):
            s_scr[bi] = jnp.zeros((hk, hv), f32)
            for h in range(N_HEADS):
                s_scr[bi, GLA_DK * h:GLA_DK * (h + 1), HEAD_DIM * h:HEAD_DIM * (h + 1)] = h0_ref[bi, h]

    stashes = []
    for bi in range(n_batch):
        z = _dot_param(c_ref[bi, :, 2 * hk + 2 * hv:], wa_hi_ref[...], wa_lo_ref[...]) + ba_ref[...]
        log_a_all = -_softplus(-z) * (1.0 / GLA_TAU)
        if valid_rows < C:
            log_a_all = jnp.where(_iota(log_a_all.shape, 0) < valid_rows, log_a_all, 0.0)
        stash = []
        for ci in range(n_chunks):
            x = c_ref[bi, ci * C:(ci + 1) * C, :]
            q = x[:, 0:hk] * (GLA_DK ** -0.5)
            k = x[:, hk:2 * hk]
            v = x[:, 2 * hk:2 * hk + hv]
            la_parts = _split(log_a_all[ci * C:(ci + 1) * C], 3)
            b = functools.reduce(jnp.add, [_dot(tril, p) for p in la_parts])
            b_last = b[C - 1:C, :]
            qe = q * jnp.exp(b)
            keb = (k * jnp.exp(-b)).astype(bf16)
            kd = k * jnp.exp(b_last - b)
            o_intra = None
            for h in range(N_HEADS):
                att = _dot_tr(jnp.where(k_head == h, qe, 0.0).astype(bf16), keb)
                att = jnp.where(causal, att, 0.0)
                part = _dot(att.astype(bf16), jnp.where(v_head == h, v, 0.0).astype(bf16))
                o_intra = part if o_intra is None else o_intra + part
            incr = jnp.where(diag_blocks, _dot_tl(kd.astype(bf16), v.astype(bf16)), 0.0)
            chunk_log_decay = functools.reduce(jnp.add, [_dot_tl(p, ones_cv) for p in la_parts])
            stash.append((qe.astype(bf16), o_intra, incr, jnp.exp(chunk_log_decay)))
        stashes.append(stash)

    outs_all = []
    for bi, stash in enumerate(stashes):
        state = s_scr[bi]
        outs = []
        for qe_b, o_intra, incr, decay in stash:
            outs.append(o_intra + _dot(qe_b, state.astype(bf16)))
            state = decay * state + incr
        s_scr[bi] = state
        outs_all.append(outs)

    for bi, outs in enumerate(outs_all):
        for ci, o in enumerate(outs):
            g = c_ref[bi, ci * C:(ci + 1) * C, 2 * hk + hv:2 * hk + 2 * hv]
            ms = _dot_exact(o * o, seg_b) * (1.0 / HEAD_DIM)
            o_ref[bi, ci * C:(ci + 1) * C, :] = o * lax.rsqrt(ms + EPS) * ng_ref[...] * (g * jax.nn.sigmoid(g))

    @pl.when(tb == pl.num_programs(1) - 1)
    def _fin():
        for bi in range(n_batch):
            for h in range(N_HEADS):
                sfin_ref[bi, h] = s_scr[bi, GLA_DK * h:GLA_DK * (h + 1), HEAD_DIM * h:HEAD_DIM * (h + 1)]


def _gla(c, h0, wa_pad, b_alpha, norm_g, valid_rows):
    bsz, t, _ = c.shape
    tb = min(t, GLA_BLOCK_CHUNKS * GLA_CHUNK)
    nb = GLA_SHORT_BATCHES if t == GLA_CHUNK and bsz % GLA_SHORT_BATCHES == 0 else 1
    wa_spec = pl.BlockSpec((LANES, N_HEADS * GLA_DK), lambda b, i: (0, 0))
    return pl.pallas_call(
        functools.partial(_gla_body, n_chunks=tb // GLA_CHUNK, valid_rows=valid_rows, n_batch=nb),
        out_shape=(jax.ShapeDtypeStruct((bsz, t, GROUP_WIDTH), f32),
                   jax.ShapeDtypeStruct((bsz, N_HEADS, GLA_DK, HEAD_DIM), f32)),
        grid=(bsz // nb, t // tb),
        in_specs=[pl.BlockSpec((nb, tb, GLA_COLS_PAD), lambda b, i: (b, i, 0)),
                  pl.BlockSpec((nb, N_HEADS, GLA_DK, HEAD_DIM), lambda b, i: (b, 0, 0, 0)),
                  wa_spec, wa_spec,
                  pl.BlockSpec((1, N_HEADS * GLA_DK), lambda b, i: (0, 0)),
                  pl.BlockSpec((1, GROUP_WIDTH), lambda b, i: (0, 0))],
        out_specs=(pl.BlockSpec((nb, tb, GROUP_WIDTH), lambda b, i: (b, i, 0)),
                   pl.BlockSpec((nb, N_HEADS, GLA_DK, HEAD_DIM), lambda b, i: (b, 0, 0, 0))),
        scratch_shapes=[pltpu.VMEM((nb, N_HEADS * GLA_DK, GROUP_WIDTH), f32)],
        compiler_params=_params("parallel", "arbitrary"),
        name="gla",
    )(c, h0, *wa_pad, b_alpha, norm_g)


def _gmlp_body(c_ref, lng_ref, lnb_ref, ws_ref, bias_ref, o_ref, *v_refs):
    gw = GROUP_WIDTH
    x = c_ref[...]
    ge = 0.5 * x * (1.0 + lax.erf(x * float(np.sqrt(0.5))))
    u, v = ge[:, :gw], ge[:, gw:]
    mean = jnp.mean(v, axis=-1, keepdims=True)
    d = v - mean
    var = jnp.mean(d * d, axis=-1, keepdims=True)
    vn = d * lax.rsqrt(var + LN_EPS) * lng_ref[...] + lnb_ref[...]
    n = GMLP_CHUNK
    causal = _iota((n, n), 0) >= _iota((n, n), 1)
    v_head = _iota((1, gw), 1) >> 6
    w_all = jnp.concatenate([jnp.where(causal, ws_ref[grp], 0.0).astype(bf16) for grp in range(N_HEADS)], axis=0)
    vb = vn.astype(bf16)
    for ci in range(x.shape[0] // n):
        rows = slice(ci * n, (ci + 1) * n)
        res = _dot(w_all, vb[rows])
        s = res[0:n]
        for grp in range(1, N_HEADS):
            s = jnp.where(v_head == grp, res[grp * n:(grp + 1) * n], s)
        o_ref[rows, :] = u[rows] * (s + bias_ref[...])
    if v_refs:
        v_refs[0][...] = vn


def _gmlp(c, ln_g, ln_b, ws, bias_rows, emit_v):
    n = c.shape[0]
    rows = min(n, GMLP_BLOCK_CHUNKS * GMLP_CHUNK)
    blk = pl.BlockSpec((rows, GROUP_WIDTH), lambda i: (i, 0))
    out_shape = [jax.ShapeDtypeStruct((n, GROUP_WIDTH), f32)]
    out_specs = [blk]
    if emit_v:
        out_shape.append(jax.ShapeDtypeStruct((n, GROUP_WIDTH), f32))
        out_specs.append(blk)
    return pl.pallas_call(
        _gmlp_body,
        out_shape=tuple(out_shape),
        grid=(n // rows,),
        in_specs=[pl.BlockSpec((rows, GMLP_COLS), lambda i: (i, 0)),
                  pl.BlockSpec((1, GROUP_WIDTH), lambda i: (0, 0)),
                  pl.BlockSpec((1, GROUP_WIDTH), lambda i: (0, 0)),
                  pl.BlockSpec((N_HEADS, GMLP_CHUNK, GMLP_CHUNK), lambda i: (0, 0, 0)),
                  pl.BlockSpec((GMLP_CHUNK, GROUP_WIDTH), lambda i: (0, 0))],
        out_specs=tuple(out_specs),
        compiler_params=_params("parallel"),
        name="gmlp",
    )(c, ln_g, ln_b, ws, bias_rows)


def _rwkv_body(c_ref, sp_ref, s0_ref, mu_ref, w0_ref, w2h_ref, w2l_ref, a0_ref, a2h_ref, a2l_ref, g2h_ref, g2l_ref,
               kk_ref, ka_ref, rk_ref, lg_ref, lb_ref, seg_ref, o_ref, sfin_ref,
               s_scr, carry_scr, r_scr, w_scr, k_scr, v_scr, a_scr, b_scr, y_scr, g_scr, bonus_scr, sr_scr,
               *, nb, n_steps):
    gw = GROUP_WIDTH
    tbr = c_ref.shape[1]
    tb = pl.program_id(1)
    seg_b = seg_ref[...]

    @pl.when(tb == 0)
    def _init():
        for b in range(nb):
            carry_scr[b] = sp_ref[b]
            for h in range(N_HEADS):
                s_scr[b, :, HEAD_DIM * h:HEAD_DIM * (h + 1)] = s0_ref[b, h]

    first_row = _iota((tbr, RWKV_COLS), 0) == 0
    mixed = []
    for b in range(nb):
        c = c_ref[b]
        prev = jnp.where(first_row, carry_scr[b], pltpu.roll(c, 1, 0))
        carry_scr[b] = c[tbr - 1:tbr, :]
        mixed.append(c + (prev - c) * mu_ref[...])
    xm = jnp.concatenate(mixed, axis=0)
    r = xm[:, 0:gw]
    k = xm[:, gw:2 * gw]
    v = xm[:, 2 * gw:3 * gw]
    wa_lr = xm[:, 3 * gw:3 * gw + LANES]
    g_lr = xm[:, 3 * gw + LANES:]
    w = -_softplus(-(w0_ref[...] + _dot_param(jnp.tanh(wa_lr), w2h_ref[...], w2l_ref[...]))) - 0.5
    a = jax.nn.sigmoid(a0_ref[...] + _dot_param(wa_lr, a2h_ref[...], a2l_ref[...]))
    kk = k * kk_ref[...]
    kk = kk / jnp.maximum(jnp.sqrt(_dot_exact(kk * kk, seg_b)), 1e-12)
    k2 = k * (1.0 + (a - 1.0) * ka_ref[...])
    per_batch = (nb, tbr, gw)
    r_scr[...] = r.reshape(per_batch)
    w_scr[...] = jnp.exp(-jnp.exp(w)).reshape(per_batch)
    k_scr[...] = k2.reshape(per_batch)
    v_scr[...] = v.reshape(per_batch)
    a_scr[...] = (-kk).reshape(per_batch)
    b_scr[...] = (kk * a).reshape(per_batch)
    g_scr[...] = _dot_param(jax.nn.sigmoid(g_lr), g2h_ref[...], g2l_ref[...]).reshape(per_batch)
    bonus_scr[...] = (_dot_exact(r * k2 * rk_ref[...], seg_b) * v).reshape(per_batch)
    y_scr[...] = jnp.zeros(per_batch, f32)

    eye = (_iota((HEAD_DIM, gw), 1) & (HEAD_DIM - 1)) == _iota((HEAD_DIM, gw), 0)

    hd = HEAD_DIM

    groups = [list(range(g0, g0 + RWKV_GROUP)) for g0 in range(0, nb, RWKV_GROUP)]

    def feedback_dot(row, batches):
        states, parts = [], []
        for b in batches:
            state = s_scr[b]
            sa_in = state * a_scr[b, row, :]
            parts += [sa_in.astype(bf16), jnp.where(eye, v_scr[b, row, :], 0.0).astype(bf16)]
            states.append(state)
        return states, _dot(jnp.concatenate(parts, axis=0), seg_b)

    def update(row, batches, states, res):
        for i, b in enumerate(batches):
            base = 2 * hd * i
            sa = res[base:base + hd]
            v_col = res[base + hd:base + 2 * hd]
            state = states[i] * w_scr[b, row, :] + sa * b_scr[b, row, :] + v_col * k_scr[b, row, :]
            s_scr[b] = state
            sr_scr[b] = state * r_scr[b, row, :]

    def readout(row, batches):
        res = _dot(jnp.concatenate([sr_scr[b].astype(bf16) for b in batches], axis=0), seg_b)
        for i, b in enumerate(batches):
            y = res[hd * i:hd * (i + 1)]
            y_scr[b, row, :] = jnp.sum(jnp.where(eye, y, 0.0), axis=0, keepdims=True)

    def step(t, carry):
        row = pl.ds(t, 1)
        prev_row = pl.ds(jnp.maximum(t - 1, 0), 1)
        fed = [feedback_dot(row, g) for g in groups]
        for g in groups:
            readout(prev_row, g)
        for g, f in zip(groups, fed):
            update(row, g, *f)
        return carry

    for b in range(nb):
        sr_scr[b] = jnp.zeros((hd, gw), f32)
    lax.fori_loop(0, n_steps, step, 0, unroll=RWKV_UNROLL)
    for g in groups:
        readout(pl.ds(n_steps - 1, 1), g)

    rows_all = (nb * tbr, gw)
    y = y_scr[...].reshape(rows_all)
    d = y - _dot_exact(y, seg_b) * (1.0 / HEAD_DIM)
    var = _dot_exact(d * d, seg_b) * (1.0 / HEAD_DIM)
    yn = d * lax.rsqrt(var + RWKV_LNX_EPS) * lg_ref[...] + lb_ref[...]
    o_ref[...] = ((yn + bonus_scr[...].reshape(rows_all)) * g_scr[...].reshape(rows_all)).reshape(per_batch)

    @pl.when(tb == pl.num_programs(1) - 1)
    def _fin():
        for b in range(nb):
            for h in range(N_HEADS):
                sfin_ref[b, h] = s_scr[b, :, HEAD_DIM * h:HEAD_DIM * (h + 1)]


def _rwkv(c, shift_prev, s0, p, seg_b, n_valid):
    bsz, t, _ = c.shape
    nb = RWKV_BATCH
    tbr = min(t, RWKV_TBLOCK)
    n_steps = min(n_valid, tbr)
    vec = lambda width: pl.BlockSpec((1, width), lambda b, i: (0, 0))
    mat = lambda rows: pl.BlockSpec((rows, GROUP_WIDTH), lambda b, i: (0, 0))
    rows_scr = pltpu.VMEM((nb, tbr, GROUP_WIDTH), f32)
    return pl.pallas_call(
        functools.partial(_rwkv_body, nb=nb, n_steps=n_steps),
        out_shape=(jax.ShapeDtypeStruct((bsz, t, GROUP_WIDTH), f32),
                   jax.ShapeDtypeStruct((bsz, N_HEADS, HEAD_DIM, HEAD_DIM), f32)),
        grid=(bsz // nb, t // tbr),
        in_specs=[pl.BlockSpec((nb, tbr, RWKV_COLS), lambda b, i: (b, i, 0)),
                  pl.BlockSpec((nb, 1, RWKV_COLS), lambda b, i: (b, 0, 0)),
                  pl.BlockSpec((nb, N_HEADS, HEAD_DIM, HEAD_DIM), lambda b, i: (b, 0, 0, 0)),
                  vec(RWKV_COLS), vec(GROUP_WIDTH), mat(LANES), mat(LANES), vec(GROUP_WIDTH), mat(LANES), mat(LANES),
                  mat(RWKV_G_RANK), mat(RWKV_G_RANK),
                  vec(GROUP_WIDTH), vec(GROUP_WIDTH), vec(GROUP_WIDTH), vec(GROUP_WIDTH), vec(GROUP_WIDTH),
                  mat(GROUP_WIDTH)],
        out_specs=(pl.BlockSpec((nb, tbr, GROUP_WIDTH), lambda b, i: (b, i, 0)),
                   pl.BlockSpec((nb, N_HEADS, HEAD_DIM, HEAD_DIM), lambda b, i: (b, 0, 0, 0))),
        scratch_shapes=[pltpu.VMEM((nb, HEAD_DIM, GROUP_WIDTH), f32), pltpu.VMEM((nb, 1, RWKV_COLS), f32)]
        + [rows_scr] * 9 + [pltpu.VMEM((nb, HEAD_DIM, GROUP_WIDTH), f32)],
        compiler_params=_params("parallel", "arbitrary"),
        name="rwkv",
    )(c, shift_prev, s0, p["mu"], p["w0"], *p["w2"], p["a0"], *p["a2"], *p["g2"], p["kk"], p["ka"], p["rk"],
      p["lnx_g"], p["lnx_b"], seg_b)


def _merge_rows(acc, mrun, den, o_new, m_new, d_new):
    m = jnp.maximum(mrun, m_new)
    a_old, a_new = jnp.exp(mrun - m), jnp.exp(m_new - m)
    return acc * a_old + o_new * a_new, m, den * a_old + d_new * a_new


def _dil_prompt_body(x_ref, o_ref, kv0_ref, kv1_ref, kv2_ref, acc_scr, m_scr, d_scr, *, t):
    grp = pl.program_id(1)
    blk = LANES
    n_pairs = GROUP_WIDTH // LANES
    rr, cc = _iota((blk, blk), 0), _iota((blk, blk), 1)
    first_head = _iota((1, LANES), 1) < HEAD_DIM
    neg_inf = float("-inf")
    kv_refs = (kv0_ref, kv1_ref, kv2_ref)

    def group(gi, window, dilation, first):
        n_sub = t // dilation // blk
        has_prev = n_sub > 1
        dist = (rr - cc).astype(f32) * float(dilation)
        valid = rr >= cc
        if has_prev:
            dist = jnp.concatenate([(rr + blk - cc).astype(f32) * float(dilation), dist], axis=1)
            valid = jnp.concatenate([cc >= rr, valid], axis=1)
            in_prev = _iota((1, 2 * blk), 1) < blk

        def keys_values(slab, rows, prows):
            cur = x_ref[0, slab, rows, :].astype(bf16)
            return jnp.concatenate([x_ref[0, slab, prows, :].astype(bf16), cur], axis=0) if has_prev else cur

        def scores(idx):
            res, sub = idx // n_sub, idx % n_sub
            start = res + dilation * blk * sub
            pstart = jnp.maximum(start - dilation * blk, 0)
            rows = pl.ds(start, blk, stride=dilation) if dilation > 1 else pl.ds(start, blk)
            prows = pl.ds(pstart, blk, stride=dilation) if dilation > 1 else pl.ds(pstart, blk)
            raw = []
            for pair in range(n_pairs):
                q = x_ref[0, pair, rows, :] * (HEAD_DIM ** -0.5)
                keys = keys_values(2 + pair, rows, prows)
                for hh in range(2):
                    lanes = first_head if hh == 0 else jnp.logical_not(first_head)
                    raw.append(_dot_tr(jnp.where(lanes, q, 0.0).astype(bf16), keys))
            return rows, prows, sub, raw

        def finish(rows, prows, sub, raw):
            if has_prev:
                gate = jnp.where(in_prev, jnp.where(sub > 0, 0.0, neg_inf), 0.0)
            for pair in range(n_pairs):
                vals = keys_values(4 + pair, rows, prows)
                vals = jnp.concatenate([vals, jnp.ones(vals.shape, bf16)], axis=1)
                o_pair = m_pair = d_pair = None
                for hh in range(2):
                    slope = _alibi_slope(gi * N_HEADS + 2 * pair + hh)
                    lanes = first_head if hh == 0 else jnp.logical_not(first_head)
                    s = jnp.where(valid, raw[2 * pair + hh] - slope * dist, neg_inf)
                    if has_prev:
                        s = s + gate
                    mx = jnp.max(s, axis=-1, keepdims=True)
                    pv = _dot(jnp.exp(s - mx).astype(bf16), vals)
                    o_h, d_h = pv[:, :LANES], pv[:, LANES:]
                    m_h = jnp.broadcast_to(mx, (blk, LANES))
                    if hh == 0:
                        o_pair, m_pair, d_pair = o_h, m_h, d_h
                    else:
                        o_pair = jnp.where(lanes, o_h, o_pair)
                        m_pair = jnp.where(lanes, m_h, m_pair)
                        d_pair = jnp.where(lanes, d_h, d_pair)
                if not first:
                    o_pair, m_pair, d_pair = _merge_rows(acc_scr[pair, rows, :], m_scr[pair, rows, :],
                                                         d_scr[pair, rows, :], o_pair, m_pair, d_pair)
                acc_scr[pair, rows, :] = o_pair
                m_scr[pair, rows, :] = m_pair
                d_scr[pair, rows, :] = d_pair

        def combos(i, carry):
            ctx = [scores(DIL_COMBOS * i + j) for j in range(DIL_COMBOS)]
            for c in ctx:
                finish(*c)
            return carry

        lax.fori_loop(0, t // blk // DIL_COMBOS, combos, 0)

        w = min(window, t)
        for j in range(2 * n_pairs):
            for c in range(w // blk):
                tile = x_ref[0, n_pairs + j, t - w + c * blk:t - w + (c + 1) * blk, :]
                kv_refs[gi][0, j * LANES:(j + 1) * LANES, c * blk:(c + 1) * blk] = tile.T

    for step, gi in enumerate(reversed(range(len(DIL_PAIRS)))):
        pl.when(grp == step)(functools.partial(group, gi, *DIL_PAIRS[gi], step == 0))

    @pl.when(grp == len(DIL_PAIRS) - 1)
    def _fin():
        o_ref[...] = acc_scr[...] / d_scr[...]


def _dil_prompt(x, bsz, t):
    n_pairs = GROUP_WIDTH // LANES
    scr = pltpu.VMEM((n_pairs, t, LANES), f32)
    kv_rows = [min(w, t) for w, _ in DIL_PAIRS]
    return pl.pallas_call(
        functools.partial(_dil_prompt_body, t=t),
        out_shape=(jax.ShapeDtypeStruct((n_pairs, bsz * t, LANES), f32),)
        + tuple(jax.ShapeDtypeStruct((bsz, 2 * GROUP_WIDTH, r), f32) for r in kv_rows),
        grid=(bsz, len(DIL_PAIRS)),
        in_specs=[pl.BlockSpec((1, DIL_SLABS, t, LANES), lambda b, g: (len(DIL_PAIRS) - 1 - g, 0, b, 0))],
        out_specs=(pl.BlockSpec((n_pairs, t, LANES), lambda b, g: (0, b, 0)),)
        + tuple(pl.BlockSpec((1, 2 * GROUP_WIDTH, r), lambda b, g: (b, 0, 0)) for r in kv_rows),
        scratch_shapes=[scr, scr, scr],
        compiler_params=_params("parallel", "arbitrary"),
        name="dil_prompt",
    )(x)


def _dil_sample_body(x_ref, c0_ref, c1_ref, c2_ref, o_ref, *, t_new):
    caches = (c0_ref, c1_ref, c2_ref)
    nq = SUBLANES
    per = nq // t_new
    first_head = _iota((1, LANES), 1) < HEAD_DIM
    neg_inf = float("-inf")
    q_idx = _iota((nq, 1), 0) & (t_new - 1)
    q_bat = _iota((nq, 1), 0) >> 2
    rn, cn = _iota((nq, nq), 0), _iota((nq, nq), 1)
    dn = (rn & (t_new - 1)) - (cn & (t_new - 1))
    same = (rn >> 2) == (cn >> 2)
    for pair in range(GROUP_WIDTH // LANES):
        acc = mrun = den = None
        for gi, (window, dilation) in enumerate(DIL_PAIRS):
            q = x_ref[gi, pair] * (HEAD_DIM ** -0.5)
            kn = x_ref[gi, 2 + pair].astype(bf16)
            vn = x_ref[gi, 4 + pair].astype(bf16)
            dist_c = window + q_idx - _iota((nq, window), 1)
            ok_c = jnp.logical_and((dist_c & (dilation - 1)) == 0, dist_c <= window)
            ok_n = jnp.logical_and(jnp.logical_and(same, dn >= 0), (dn & (dilation - 1)) == 0)
            q2 = jnp.concatenate([jnp.where(first_head, q, 0.0), jnp.where(first_head, 0.0, q)], axis=0).astype(bf16)
            slope2 = jnp.where(_iota((2 * nq, 1), 0) < nq, _alibi_slope(gi * N_HEADS + 2 * pair),
                               _alibi_slope(gi * N_HEADS + 2 * pair + 1))
            dist_c2 = jnp.concatenate([dist_c, dist_c], axis=0).astype(f32)
            ok_c2 = jnp.concatenate([ok_c, ok_c], axis=0)
            dn2 = jnp.concatenate([dn, dn], axis=0).astype(f32)
            ok_n2 = jnp.concatenate([ok_n, ok_n], axis=0)
            s_new = jnp.where(ok_n2, _dot_tr(q2, kn) - slope2 * dn2, neg_inf)
            mx_new = jnp.max(s_new, axis=-1, keepdims=True)
            o_pair = m_pair = d_pair = None
            for s in range(per):
                kt = caches[gi][0, s, pair * LANES:(pair + 1) * LANES, :].astype(bf16)
                vt = caches[gi][0, s, GROUP_WIDTH + pair * LANES:GROUP_WIDTH + (pair + 1) * LANES, :].astype(bf16)
                s_c = jnp.where(ok_c2, _dot(q2, kt) - slope2 * dist_c2, neg_inf)
                mx = jnp.maximum(jnp.max(s_c, axis=-1, keepdims=True), mx_new)
                p_c, p_n = jnp.exp(s_c - mx), jnp.exp(s_new - mx)
                o2 = _dot_tr(p_c.astype(bf16), vt) + _dot(p_n.astype(bf16), vn)
                d2 = jnp.sum(p_c, axis=-1, keepdims=True) + jnp.sum(p_n, axis=-1, keepdims=True)
                o_s = jnp.where(first_head, o2[:nq], o2[nq:])
                m_s = jnp.where(first_head, jnp.broadcast_to(mx[:nq], (nq, LANES)), jnp.broadcast_to(mx[nq:], (nq, LANES)))
                d_s = jnp.where(first_head, jnp.broadcast_to(d2[:nq], (nq, LANES)), jnp.broadcast_to(d2[nq:], (nq, LANES)))
                if s == 0:
                    o_pair, m_pair, d_pair = o_s, m_s, d_s
                else:
                    mine = q_bat == s
                    o_pair = jnp.where(mine, o_s, o_pair)
                    m_pair = jnp.where(mine, m_s, m_pair)
                    d_pair = jnp.where(mine, d_s, d_pair)
            if gi == 0:
                acc, mrun, den = o_pair, m_pair, d_pair
            else:
                acc, mrun, den = _merge_rows(acc, mrun, den, o_pair, m_pair, d_pair)
        o_ref[pair] = acc / den


def _dil_sample(x, caches, layer, bsz, t_new):
    n_pairs = GROUP_WIDTH // LANES
    per = SUBLANES // t_new
    return pl.pallas_call(
        functools.partial(_dil_sample_body, t_new=t_new),
        out_shape=jax.ShapeDtypeStruct((n_pairs, bsz * t_new, LANES), f32),
        grid=(bsz // per,),
        in_specs=[pl.BlockSpec((len(DIL_PAIRS), DIL_SLABS, SUBLANES, LANES), lambda i: (0, 0, i, 0))]
        + [pl.BlockSpec((1, per, 2 * GROUP_WIDTH, w), lambda i: (layer, i, 0, 0)) for w, _ in DIL_PAIRS],
        out_specs=pl.BlockSpec((n_pairs, SUBLANES, LANES), lambda i: (0, i, 0)),
        compiler_params=_params("parallel"),
        name="dil_sample",
    )(x, *caches)


def _row(v):
    return v.reshape(1, -1)


def _prep_layer(l, norm_gains, w_ff_gate, w_ff_up, w_ff_down, w_in, w_out, gla_w_alpha2, gla_b_alpha, gla_norm,
                gmlp_ln_g, gmlp_ln_b, gmlp_ws, gmlp_bs, rwkv_mu, rwkv_w0, rwkv_w2, rwkv_a0, rwkv_a2, rwkv_g2,
                rwkv_kk, rwkv_ka, rwkv_rk, rwkv_lnx_g, rwkv_lnx_b):
    win = w_in[l]
    o1, o2, o3 = GLA_COLS, GLA_COLS + GMLP_COLS, GLA_COLS + GMLP_COLS + RWKV_COLS
    hk = N_HEADS * GLA_DK
    a_lo, a_hi = 2 * hk + GROUP_WIDTH, 2 * hk + GROUP_WIDTH + GLA_RANK
    w_gla = win[:, :o1]
    w_gla = jnp.concatenate([w_gla[:, :a_lo], w_gla[:, a_hi:], w_gla[:, a_lo:a_hi],
                             jnp.zeros((D_MODEL, GLA_COLS_PAD - GLA_COLS), f32)], axis=1)
    zeros_r = jnp.zeros((RWKV_W_RANK, GROUP_WIDTH), f32)
    return dict(
        norms=[_row(norm_gains[l, i]) for i in range(norm_gains.shape[1])],
        w_gla=w_gla.astype(bf16), w_gmlp=win[:, o1:o2].astype(bf16), w_rwkv=win[:, o2:o3].astype(bf16),
        w_dil=win[:, o3:].astype(bf16),
        gla_wa=_split_param(jnp.concatenate([gla_w_alpha2[l], jnp.zeros((LANES - GLA_RANK, hk), f32)], axis=0)),
        gla_ba=_row(gla_b_alpha[l]), gla_norm=_row(gla_norm[l]),
        gmlp_ln_g=_row(gmlp_ln_g[l]), gmlp_ln_b=_row(gmlp_ln_b[l]), gmlp_ws=gmlp_ws[l], gmlp_bs=gmlp_bs[l],
        rwkv=dict(mu=_row(rwkv_mu[l]), w0=_row(rwkv_w0[l]),
                  w2=_split_param(jnp.concatenate([rwkv_w2[l], zeros_r], axis=0)), a0=_row(rwkv_a0[l]),
                  a2=_split_param(jnp.concatenate([zeros_r, rwkv_a2[l]], axis=0)), g2=_split_param(rwkv_g2[l]),
                  kk=_row(rwkv_kk[l]), ka=_row(rwkv_ka[l]), rk=_row(rwkv_rk[l]),
                  lnx_g=_row(rwkv_lnx_g[l]), lnx_b=_row(rwkv_lnx_b[l])),
    )


def _kv_rows(c_dil, bsz, t, rows):
    out = []
    for gi in range(len(DIL_PAIRS)):
        kv = c_dil[gi, 2:].reshape(2, GROUP_WIDTH // LANES, bsz, t, LANES)[:, :, :, t - rows[gi]:]
        kv = jnp.transpose(kv, (2, 3, 0, 1, 4))
        out.append(kv.reshape(bsz, rows[gi], 2, N_HEADS, HEAD_DIM))
    return out


def _channel_major_kv(kv):
    bsz, _, rows = kv.shape
    return jnp.transpose(kv.reshape(bsz, 2, N_HEADS, HEAD_DIM, rows), (0, 4, 1, 2, 3))


def _trunk_layer(x, p, ffw, seg_b, bsz, t, gla_h0, rwkv_s0, shift0, caches, layer):
    ng = p["norms"]
    n = bsz * t
    h = _ffn(x, ng[0], ng[1], ffw["gate"], ffw["up"], ffw["down"], layer, 0)
    c_gla, c_gmlp, c_rwkv, c_dil = _proj(h, ng[2], p["w_gla"], p["w_gmlp"], p["w_rwkv"], p["w_dil"])
    is_prompt = caches is None

    t_gla = max(t, GLA_CHUNK)
    cg = c_gla.reshape(bsz, t, GLA_COLS_PAD)
    if t_gla != t:
        cg = jnp.pad(cg, ((0, 0), (0, t_gla - t), (0, 0)))
    o_gla, gla_s = _gla(cg, gla_h0, p["gla_wa"], p["gla_ba"], p["gla_norm"], min(t, GLA_CHUNK))
    o_gla = o_gla[:, :t].reshape(n, GROUP_WIDTH)

    if t >= GMLP_CHUNK:
        ws = p["gmlp_ws"]
        bias_rows = jnp.repeat(p["gmlp_bs"].T, HEAD_DIM, axis=1)
    else:
        per = GMLP_CHUNK // t
        eye = jnp.eye(per, dtype=f32)
        ws = jnp.einsum("ab,gij->gaibj", eye, p["gmlp_ws"][:, :t, :t]).reshape(N_HEADS, GMLP_CHUNK, GMLP_CHUNK)
        bias_rows = jnp.tile(jnp.repeat(p["gmlp_bs"][:, :t].T, HEAD_DIM, axis=1), (per, 1))
    gm = _gmlp(c_gmlp, p["gmlp_ln_g"], p["gmlp_ln_b"], ws, bias_rows, emit_v=not is_prompt)
    o_gmlp = gm[0]
    gmlp_v = None if is_prompt else gm[1]

    t_rw = max(t, SUBLANES)
    cr = c_rwkv.reshape(bsz, t, RWKV_COLS)
    shift = cr[:, -1]
    if t_rw != t:
        cr = jnp.pad(cr, ((0, 0), (0, t_rw - t), (0, 0)))
    o_rwkv, rwkv_s = _rwkv(cr, shift0.reshape(bsz, 1, RWKV_COLS), rwkv_s0, p["rwkv"], seg_b, t)
    o_rwkv = o_rwkv[:, :t].reshape(n, GROUP_WIDTH)

    if is_prompt:
        o_dil, *kv_cm = _dil_prompt(c_dil, bsz, t)
        kv_new = [_channel_major_kv(kv) for kv in kv_cm]
    else:
        o_dil = _dil_sample(c_dil, caches, layer, bsz, t)
        kv_new = _kv_rows(c_dil, bsz, t, [min(w, t) for w, _ in DIL_PAIRS])

    y = _ffn(h, ng[4], ng[5], ffw["gate"], ffw["up"], ffw["down"], layer, 1,
             mix=(o_gla, o_gmlp, o_rwkv, o_dil, ffw["out"], ng[3]))
    return y, gla_s, rwkv_s, shift, gmlp_v, kv_new


def kernel(x_prompt, x_sample, cache_win128, cache_win512, cache_win2048, state_gla, state_rwkv, state_shift,
           norm_gains, w_ff_gate, w_ff_up, w_ff_down, w_in, w_out, gla_w_alpha2, gla_b_alpha, gla_norm,
           gmlp_ln_g, gmlp_ln_b, gmlp_ws, gmlp_bs, rwkv_mu, rwkv_w0, rwkv_w2, rwkv_a0, rwkv_a2, rwkv_g2,
           rwkv_kk, rwkv_ka, rwkv_rk, rwkv_lnx_g, rwkv_lnx_b):
    bp, tp, _ = x_prompt.shape
    bs, ts, _ = x_sample.shape
    depth = norm_gains.shape[0]
    assert tp % GMLP_CHUNK == 0 and tp % (DIL_PAIRS[-1][1] * LANES) == 0 and bp % RWKV_BATCH == 0
    assert ts == 4 and bs % RWKV_BATCH == 0 and (bs * ts) % GMLP_CHUNK == 0
    caches = [jnp.transpose(c, (0, 1, 3, 4, 5, 2)).reshape(depth, bs, 2 * GROUP_WIDTH, c.shape[2])
              for c in (cache_win128, cache_win512, cache_win2048)]
    lanes = np.arange(GROUP_WIDTH)
    seg_b = jnp.asarray((lanes[:, None] // HEAD_DIM) == (lanes[None, :] // HEAD_DIM), dtype=bf16)
    ffw = dict(gate=w_ff_gate.astype(bf16), up=w_ff_up.astype(bf16), down=w_ff_down.astype(bf16),
               out=w_out.astype(bf16))
    gla0 = jnp.zeros((bp, N_HEADS, GLA_DK, HEAD_DIM), f32)
    rwkv0 =jnp.zeros((bp, N_HEADS, HEAD_DIM, HEAD_DIM), f32)
    shift0 = jnp.zeros((bp, RWKV_COLS), f32)
    yp = x_prompt.reshape(bp * tp, D_MODEL)
    ys = x_sample.reshape(bs * ts, D_MODEL)
    pg, pr, psh, pw = [], [], [], [[], [], []]
    sg, sr, ssh, sw, sv = [], [], [], [[], [], []], []
    for l in range(depth):
        p = _prep_layer(l, norm_gains, w_ff_gate, w_ff_up, w_ff_down, w_in, w_out, gla_w_alpha2, gla_b_alpha,
                        gla_norm, gmlp_ln_g, gmlp_ln_b, gmlp_ws, gmlp_bs, rwkv_mu, rwkv_w0, rwkv_w2, rwkv_a0,
                        rwkv_a2, rwkv_g2, rwkv_kk, rwkv_ka, rwkv_rk, rwkv_lnx_g, rwkv_lnx_b)
        yp, g_p, r_p, sh_p, _, kv_p = _trunk_layer(yp, p, ffw, seg_b, bp, tp, gla0, rwkv0, shift0, None, l)
        ys, g_s, r_s, sh_s, v_s, kv_s = _trunk_layer(ys, p, ffw, seg_b, bs, ts, state_gla[l], state_rwkv[l],
                                                     state_shift[l], caches, l)
        pg.append(g_p)
        pr.append(r_p)
        psh.append(sh_p)
        sg.append(g_s)
        sr.append(r_s)
        ssh.append(sh_s)
        sv.append(v_s.reshape(bs, ts, GROUP_WIDTH))
        for gi in range(len(DIL_PAIRS)):
            pw[gi].append(kv_p[gi])
            sw[gi].append(kv_s[gi])
    return (yp.reshape(bp, tp, D_MODEL), ys.reshape(bs, ts, D_MODEL),
            jnp.stack(pg), jnp.stack(pr), jnp.stack(psh),
            jnp.stack(pw[0]), jnp.stack(pw[1]), jnp.stack(pw[2]),
            jnp.stack(sg), jnp.stack(sr), jnp.stack(ssh),
            jnp.stack(sw[0]), jnp.stack(sw[1]), jnp.stack(sw[2]),
            jnp.stack(sv))
```

```python
import functools

import jax
import jax.numpy as jnp
import numpy as np
from jax import lax
from jax.experimental import pallas as pl
from jax.experimental.pallas import tpu as pltpu

f32 = jnp.float32
bf16 = jnp.bfloat16

D_MODEL = 1024
HEAD_DIM = 64
N_HEADS = 4
GROUP_WIDTH = N_HEADS * HEAD_DIM
D_FF = 2816
EPS = 1e-6
GLA_DK = 32
GLA_RANK = 16
GLA_TAU = 16.0
GLA_CHUNK = 64
GMLP_CHUNK = 128
LN_EPS = 1e-5
RWKV_W_RANK = 64
RWKV_A_RANK = 64
RWKV_G_RANK = 128
RWKV_LNX_EPS = 64e-5
DIL_PAIRS = ((128, 1), (512, 4), (2048, 16))
N_DIL_HEADS = len(DIL_PAIRS) * N_HEADS
ALIBI_MAX_EXP = 8.0
GLA_COLS = 2 * N_HEADS * GLA_DK + GROUP_WIDTH + GLA_RANK + GROUP_WIDTH
GMLP_COLS = 2 * GROUP_WIDTH
RWKV_COLS = 3 * GROUP_WIDTH + RWKV_W_RANK + RWKV_A_RANK + RWKV_G_RANK
DIL_COLS = len(DIL_PAIRS) * 3 * GROUP_WIDTH

LANES = 128
SUBLANES = 8
VMEM_LIMIT_BYTES = 56 * 1024 * 1024

GLA_COLS_PAD = 896
DIL_SLABS = DIL_COLS // LANES // len(DIL_PAIRS)
FFN_CHUNK = 256
ROW_TILE = 512
FFN_ROW_TILE = 1024
FFN_SUBTILE = 512
GMLP_BLOCK_CHUNKS = 4
DIL_COMBOS = 4
GLA_SHORT_BATCHES = 4
GLA_BLOCK_CHUNKS = 8
RWKV_BATCH = 8
RWKV_TBLOCK = 128
RWKV_GROUP = 2
RWKV_UNROLL = 4


def _alibi_slope(head):
    return float(2.0 ** (-ALIBI_MAX_EXP * (head + 1.0) / N_DIL_HEADS))


def _iota(shape, dim):
    return lax.broadcasted_iota(jnp.int32, shape, dim)


def _dot(a, b, precision=None):
    return jnp.dot(a, b, preferred_element_type=f32, precision=precision)


def _dot_tr(a, b, precision=None):
    return lax.dot_general(a, b, (((1,), (1,)), ((), ())), preferred_element_type=f32, precision=precision)


def _dot_tl(a, b, precision=None):
    return lax.dot_general(a, b, (((0,), (0,)), ((), ())), preferred_element_type=f32, precision=precision)


def _rms_rows(x, g):
    ms = jnp.mean(x * x, axis=-1, keepdims=True)
    return x * lax.rsqrt(ms + EPS) * g


def _softplus(x):
    return jnp.maximum(x, 0.0) + jnp.log1p(jnp.exp(-jnp.abs(x)))


def _split(x, n):
    parts = []
    for _ in range(n - 1):
        p = x.astype(bf16)
        parts.append(p)
        x = x - p.astype(f32)
    parts.append(x.astype(bf16))
    return parts


def _split_param(w):
    hi = w.astype(bf16)
    return hi, (w - hi.astype(f32)).astype(bf16)


def _dot_exact(x, m_b, n=2):
    return functools.reduce(jnp.add, [_dot(p, m_b) for p in _split(x, n)])


def _dot_param(x, w_hi, w_lo):
    x_hi, x_lo = _split(x, 2)
    return _dot(x_hi, w_hi) + _dot(x_lo, w_hi) + _dot(x_hi, w_lo)


def _params(*semantics):
    return pltpu.CompilerParams(dimension_semantics=semantics, vmem_limit_bytes=VMEM_LIMIT_BYTES)


def _resident(shape):
    zeros = (0,) * len(shape)
    return pl.BlockSpec(shape, lambda *_: zeros, pipeline_mode=pl.Buffered(1))


def _ffn_body(*refs, with_mix):
    if with_mix:
        (h_ref, ogla_ref, ogmlp_ref, orwkv_ref, odil_ref, wout_ref, gmix_ref,
         gpre_ref, gpost_ref, wg_ref, wu_ref, wd_ref, o_ref) = refs
        x_ref = h_ref
    else:
        x_ref, gpre_ref, gpost_ref, wg_ref, wu_ref, wd_ref, o_ref = refs
    tm = x_ref.shape[0]
    sub = min(FFN_SUBTILE, tm)
    for s0 in range(0, tm, sub):
        rows = slice(s0, s0 + sub)
        x = x_ref[rows, :]
        if with_mix:
            gw = GROUP_WIDTH
            mix = _dot(ogla_ref[rows, :].astype(bf16), wout_ref[0, 0:gw, :])
            mix += _dot(ogmlp_ref[rows, :].astype(bf16), wout_ref[0, gw:2 * gw, :])
            mix += _dot(orwkv_ref[rows, :].astype(bf16), wout_ref[0, 2 * gw:3 * gw, :])
            for p in range(gw // LANES):
                mix += _dot(odil_ref[p, rows, :].astype(bf16),
                            wout_ref[0, 3 * gw + p * LANES:3 * gw + (p + 1) * LANES, :])
            x = x + _rms_rows(mix, gmix_ref[...])
        xn = _rms_rows(x, gpre_ref[...]).astype(bf16)
        acc = None
        for c in range(D_FF // FFN_CHUNK):
            lo, hi = c * FFN_CHUNK, (c + 1) * FFN_CHUNK
            g = _dot(xn, wg_ref[0, 0, :, lo:hi])
            u = _dot(xn, wu_ref[0, 0, :, lo:hi])
            act = (g * jax.nn.sigmoid(g) * u).astype(bf16)
            part = _dot(act, wd_ref[0, 0, lo:hi, :])
            acc = part if acc is None else acc + part
        o_ref[rows, :] = x + 0.5 * _rms_rows(acc, gpost_ref[...])


def _ffn(x, g_pre, g_post, wg, wu, wd, layer, idx, mix=None):
    n = x.shape[0]
    tm = min(FFN_ROW_TILE, n)
    row = pl.BlockSpec((tm, D_MODEL), lambda i: (i, 0))
    once = dict(pipeline_mode=pl.Buffered(1))
    w_specs = [pl.BlockSpec((1, 1, D_MODEL, D_FF), lambda i: (layer, idx, 0, 0), **once),
               pl.BlockSpec((1, 1, D_MODEL, D_FF), lambda i: (layer, idx, 0, 0), **once),
               pl.BlockSpec((1, 1, D_FF, D_MODEL), lambda i: (layer, idx, 0, 0), **once)]
    in_specs, args = [row], [x]
    if mix is not None:
        o_gla, o_gmlp, o_rwkv, o_dil, w_out, g_mix = mix
        grp = pl.BlockSpec((tm, GROUP_WIDTH), lambda i: (i, 0))
        in_specs += [grp, grp, grp, pl.BlockSpec((GROUP_WIDTH // LANES, tm, LANES), lambda i: (0, i, 0)),
                     pl.BlockSpec((1, D_MODEL, D_MODEL), lambda i: (layer, 0, 0), **once), _resident((1, D_MODEL))]
        args += [o_gla, o_gmlp, o_rwkv, o_dil, w_out, g_mix]
    return pl.pallas_call(
        functools.partial(_ffn_body, with_mix=mix is not None),
        out_shape=jax.ShapeDtypeStruct((n, D_MODEL), f32),
        grid=(n // tm,),
        in_specs=in_specs + [_resident((1, D_MODEL)), _resident((1, D_MODEL))] + w_specs,
        out_specs=row,
        compiler_params=_params("parallel"),
        name="ffn_mix" if mix is not None else "ffn",
    )(*args, g_pre, g_post, wg, wu, wd)


def _proj_body(h_ref, g_ref, wgla_ref, wgmlp_ref, wrwkv_ref, wdil_ref, ogla_ref, ogmlp_ref, orwkv_ref, odil_ref):
    xn = _rms_rows(h_ref[...], g_ref[...]).astype(bf16)
    ogla_ref[...] = _dot(xn, wgla_ref[...])
    ogmlp_ref[...] = _dot(xn, wgmlp_ref[...])
    orwkv_ref[...] = _dot(xn, wrwkv_ref[...])
    res = _dot(xn, wdil_ref[...])
    for s in range(DIL_COLS // LANES):
        odil_ref[s // DIL_SLABS, s % DIL_SLABS] = res[:, s * LANES:(s + 1) * LANES]


def _proj(h, g, w_gla, w_gmlp, w_rwkv, w_dil):
    n = h.shape[0]
    tm = min(ROW_TILE, n)
    n_grp = len(DIL_PAIRS)
    return pl.pallas_call(
        _proj_body,
        out_shape=(jax.ShapeDtypeStruct((n, GLA_COLS_PAD), f32),
                   jax.ShapeDtypeStruct((n, GMLP_COLS), f32),
                   jax.ShapeDtypeStruct((n, RWKV_COLS), f32),
                   jax.ShapeDtypeStruct((n_grp, DIL_SLABS, n, LANES), f32)),
        grid=(n // tm,),
        in_specs=[pl.BlockSpec((tm, D_MODEL), lambda i: (i, 0)), _resident((1, D_MODEL)),
                  _resident((D_MODEL, GLA_COLS_PAD)), _resident((D_MODEL, GMLP_COLS)),
                  _resident((D_MODEL, RWKV_COLS)), _resident((D_MODEL, DIL_COLS))],
        out_specs=(pl.BlockSpec((tm, GLA_COLS_PAD), lambda i: (i, 0)),
                   pl.BlockSpec((tm, GMLP_COLS), lambda i: (i, 0)),
                   pl.BlockSpec((tm, RWKV_COLS), lambda i: (i, 0)),
                   pl.BlockSpec((n_grp, DIL_SLABS, tm, LANES), lambda i: (0, 0, i, 0))),
        compiler_params=_params("parallel"),
        name="proj",
    )(h, g, w_gla, w_gmlp, w_rwkv, w_dil)


def _gla_body(c_ref, h0_ref, wa_hi_ref, wa_lo_ref, ba_ref, ng_ref, o_ref, sfin_ref, s_scr, *, n_chunks, valid_rows,
              n_batch):
    C = GLA_CHUNK
    hk, hv = N_HEADS * GLA_DK, GROUP_WIDTH
    tb = pl.program_id(1)

    rr, cc = _iota((C, C), 0), _iota((C, C), 1)
    causal = rr >= cc
    tril = causal.astype(bf16)
    k_head = _iota((1, hk), 1) >> 5
    v_head = _iota((1, hv), 1) >> 6
    diag_blocks = (_iota((hk, hv), 0) >> 5) == (_iota((hk, hv), 1) >> 6)
    seg_b = ((_iota((hv, hv), 0) >> 6) == (_iota((hv, hv), 1) >> 6)).astype(bf16)
    ones_cv = jnp.ones((C, hv), bf16)

    @pl.when(tb == 0)
    def _init():
        for bi in range(n_batch):
            s_scr[bi] = jnp.zeros((hk, hv), f32)
            for h in range(N_HEADS):
                s_scr[bi, GLA_DK * h:GLA_DK * (h + 1), HEAD_DIM * h:HEAD_DIM * (h + 1)] = h0_ref[bi, h]

    stashes = []
    for bi in range(n_batch):
        z = _dot_param(c_ref[bi, :, 2 * hk + 2 * hv:], wa_hi_ref[...], wa_lo_ref[...]) + ba_ref[...]
        log_a_all = -_softplus(-z) * (1.0 / GLA_TAU)
        if valid_rows < C:
            log_a_all = jnp.where(_iota(log_a_all.shape, 0) < valid_rows, log_a_all, 0.0)
        stash = []
        for ci in range(n_chunks):
            x = c_ref[bi, ci * C:(ci + 1) * C, :]
            q = x[:, 0:hk] * (GLA_DK ** -0.5)
            k = x[:, hk:2 * hk]
            v = x[:, 2 * hk:2 * hk + hv]
            la_parts = _split(log_a_all[ci * C:(ci + 1) * C], 3)
            b = functools.reduce(jnp.add, [_dot(tril, p) for p in la_parts])
            b_last = b[C - 1:C, :]
            qe = q * jnp.exp(b)
            keb = (k * jnp.exp(-b)).astype(bf16)
            kd = k * jnp.exp(b_last - b)
            o_intra = None
            for h in range(N_HEADS):
                att = _dot_tr(jnp.where(k_head == h, qe, 0.0).astype(bf16), keb)
                att = jnp.where(causal, att, 0.0)
                part = _dot(att.astype(bf16), jnp.where(v_head == h, v, 0.0).astype(bf16))
                o_intra = part if o_intra is None else o_intra + part
            incr = jnp.where(diag_blocks, _dot_tl(kd.astype(bf16), v.astype(bf16)), 0.0)
            chunk_log_decay = functools.reduce(jnp.add, [_dot_tl(p, ones_cv) for p in la_parts])
            stash.append((qe.astype(bf16), o_intra, incr, jnp.exp(chunk_log_decay)))
        stashes.append(stash)

    outs_all = []
    for bi, stash in enumerate(stashes):
        state = s_scr[bi]
        outs = []
        for qe_b, o_intra, incr, decay in stash:
            outs.append(o_intra + _dot(qe_b, state.astype(bf16)))
            state = decay * state + incr
        s_scr[bi] = state
        outs_all.append(outs)

    for bi, outs in enumerate(outs_all):
        for ci, o in enumerate(outs):
            g = c_ref[bi, ci * C:(ci + 1) * C, 2 * hk + hv:2 * hk + 2 * hv]
            ms = _dot_exact(o * o, seg_b) * (1.0 / HEAD_DIM)
            o_ref[bi, ci * C:(ci + 1) * C, :] = o * lax.rsqrt(ms + EPS) * ng_ref[...] * (g * jax.nn.sigmoid(g))

    @pl.when(tb == pl.num_programs(1) - 1)
    def _fin():
        for bi in range(n_batch):
            for h in range(N_HEADS):
                sfin_ref[bi, h] = s_scr[bi, GLA_DK * h:GLA_DK * (h + 1), HEAD_DIM * h:HEAD_DIM * (h + 1)]


def _gla(c, h0, wa_pad, b_alpha, norm_g, valid_rows):
    bsz, t, _ = c.shape
    tb = min(t, GLA_BLOCK_CHUNKS * GLA_CHUNK)
    nb = GLA_SHORT_BATCHES if t == GLA_CHUNK and bsz % GLA_SHORT_BATCHES == 0 else 1
    wa_spec = pl.BlockSpec((LANES, N_HEADS * GLA_DK), lambda b, i: (0, 0))
    return pl.pallas_call(
        functools.partial(_gla_body, n_chunks=tb // GLA_CHUNK, valid_rows=valid_rows, n_batch=nb),
        out_shape=(jax.ShapeDtypeStruct((bsz, t, GROUP_WIDTH), f32),
                   jax.ShapeDtypeStruct((bsz, N_HEADS, GLA_DK, HEAD_DIM), f32)),
        grid=(bsz // nb, t // tb),
        in_specs=[pl.BlockSpec((nb, tb, GLA_COLS_PAD), lambda b, i: (b, i, 0)),
                  pl.BlockSpec((nb, N_HEADS, GLA_DK, HEAD_DIM), lambda b, i: (b, 0, 0, 0)),
                  wa_spec, wa_spec,
                  pl.BlockSpec((1, N_HEADS * GLA_DK), lambda b, i: (0, 0)),
                  pl.BlockSpec((1, GROUP_WIDTH), lambda b, i: (0, 0))],
        out_specs=(pl.BlockSpec((nb, tb, GROUP_WIDTH), lambda b, i: (b, i, 0)),
                   pl.BlockSpec((nb, N_HEADS, GLA_DK, HEAD_DIM), lambda b, i: (b, 0, 0, 0))),
        scratch_shapes=[pltpu.VMEM((nb, N_HEADS * GLA_DK, GROUP_WIDTH), f32)],
        compiler_params=_params("parallel", "arbitrary"),
        name="gla",
    )(c, h0, *wa_pad, b_alpha, norm_g)


def _gmlp_body(c_ref, lng_ref, lnb_ref, ws_ref, bias_ref, o_ref, *v_refs):
    gw = GROUP_WIDTH
    x = c_ref[...]
    ge = 0.5 * x * (1.0 + lax.erf(x * float(np.sqrt(0.5))))
    u, v = ge[:, :gw], ge[:, gw:]
    mean = jnp.mean(v, axis=-1, keepdims=True)
    d = v - mean
    var = jnp.mean(d * d, axis=-1, keepdims=True)
    vn = d * lax.rsqrt(var + LN_EPS) * lng_ref[...] + lnb_ref[...]
    n = GMLP_CHUNK
    causal = _iota((n, n), 0) >= _iota((n, n), 1)
    v_head = _iota((1, gw), 1) >> 6
    w_all = jnp.concatenate([jnp.where(causal, ws_ref[grp], 0.0).astype(bf16) for grp in range(N_HEADS)], axis=0)
    vb = vn.astype(bf16)
    for ci in range(x.shape[0] // n):
        rows = slice(ci * n, (ci + 1) * n)
        res = _dot(w_all, vb[rows])
        s = res[0:n]
        for grp in range(1, N_HEADS):
            s = jnp.where(v_head == grp, res[grp * n:(grp + 1) * n], s)
        o_ref[rows, :] = u[rows] * (s + bias_ref[...])
    if v_refs:
        v_refs[0][...] = vn


def _gmlp(c, ln_g, ln_b, ws, bias_rows, emit_v):
    n = c.shape[0]
    rows = min(n, GMLP_BLOCK_CHUNKS * GMLP_CHUNK)
    blk = pl.BlockSpec((rows, GROUP_WIDTH), lambda i: (i, 0))
    out_shape = [jax.ShapeDtypeStruct((n, GROUP_WIDTH), f32)]
    out_specs = [blk]
    if emit_v:
        out_shape.append(jax.ShapeDtypeStruct((n, GROUP_WIDTH), f32))
        out_specs.append(blk)
    return pl.pallas_call(
        _gmlp_body,
        out_shape=tuple(out_shape),
        grid=(n // rows,),
        in_specs=[pl.BlockSpec((rows, GMLP_COLS), lambda i: (i, 0)),
                  pl.BlockSpec((1, GROUP_WIDTH), lambda i: (0, 0)),
                  pl.BlockSpec((1, GROUP_WIDTH), lambda i: (0, 0)),
                  pl.BlockSpec((N_HEADS, GMLP_CHUNK, GMLP_CHUNK), lambda i: (0, 0, 0)),
                  pl.BlockSpec((GMLP_CHUNK, GROUP_WIDTH), lambda i: (0, 0))],
        out_specs=tuple(out_specs),
        compiler_params=_params("parallel"),
        name="gmlp",
    )(c, ln_g, ln_b, ws, bias_rows)


def _rwkv_body(c_ref, sp_ref, s0_ref, mu_ref, w0_ref, w2h_ref, w2l_ref, a0_ref, a2h_ref, a2l_ref, g2h_ref, g2l_ref,
               kk_ref, ka_ref, rk_ref, lg_ref, lb_ref, seg_ref, o_ref, sfin_ref,
               s_scr, carry_scr, r_scr, w_scr, k_scr, v_scr, a_scr, b_scr, y_scr, g_scr, bonus_scr, sr_scr,
               *, nb, n_steps):
    gw = GROUP_WIDTH
    tbr = c_ref.shape[1]
    tb = pl.program_id(1)
    seg_b = seg_ref[...]

    @pl.when(tb == 0)
    def _init():
        for b in range(nb):
            carry_scr[b] = sp_ref[b]
            for h in range(N_HEADS):
                s_scr[b, :, HEAD_DIM * h:HEAD_DIM * (h + 1)] = s0_ref[b, h]

    first_row = _iota((tbr, RWKV_COLS), 0) == 0
    mixed = []
    for b in range(nb):
        c = c_ref[b]
        prev = jnp.where(first_row, carry_scr[b], pltpu.roll(c, 1, 0))
        carry_scr[b] = c[tbr - 1:tbr, :]
        mixed.append(c + (prev - c) * mu_ref[...])
    xm = jnp.concatenate(mixed, axis=0)
    r = xm[:, 0:gw]
    k = xm[:, gw:2 * gw]
    v = xm[:, 2 * gw:3 * gw]
    wa_lr = xm[:, 3 * gw:3 * gw + LANES]
    g_lr = xm[:, 3 * gw + LANES:]
    w = -_softplus(-(w0_ref[...] + _dot_param(jnp.tanh(wa_lr), w2h_ref[...], w2l_ref[...]))) - 0.5
    a = jax.nn.sigmoid(a0_ref[...] + _dot_param(wa_lr, a2h_ref[...], a2l_ref[...]))
    kk = k * kk_ref[...]
    kk = kk / jnp.maximum(jnp.sqrt(_dot_exact(kk * kk, seg_b)), 1e-12)
    k2 = k * (1.0 + (a - 1.0) * ka_ref[...])
    per_batch = (nb, tbr, gw)
    r_scr[...] = r.reshape(per_batch)
    w_scr[...] = jnp.exp(-jnp.exp(w)).reshape(per_batch)
    k_scr[...] = k2.reshape(per_batch)
    v_scr[...] = v.reshape(per_batch)
    a_scr[...] = (-kk).reshape(per_batch)
    b_scr[...] = (kk * a).reshape(per_batch)
    g_scr[...] = _dot_param(jax.nn.sigmoid(g_lr), g2h_ref[...], g2l_ref[...]).reshape(per_batch)
    bonus_scr[...] = (_dot_exact(r * k2 * rk_ref[...], seg_b) * v).reshape(per_batch)
    y_scr[...] = jnp.zeros(per_batch, f32)

    eye = (_iota((HEAD_DIM, gw), 1) & (HEAD_DIM - 1)) == _iota((HEAD_DIM, gw), 0)

    hd = HEAD_DIM

    groups = [list(range(g0, g0 + RWKV_GROUP)) for g0 in range(0, nb, RWKV_GROUP)]

    def feedback_dot(row, batches):
        states, parts = [], []
        for b in batches:
            state = s_scr[b]
            sa_in = state * a_scr[b, row, :]
            parts += [sa_in.astype(bf16), jnp.where(eye, v_scr[b, row, :], 0.0).astype(bf16)]
            states.append(state)
        return states, _dot(jnp.concatenate(parts, axis=0), seg_b)

    def update(row, batches, states, res):
        for i, b in enumerate(batches):
            base = 2 * hd * i
            sa = res[base:base + hd]
            v_col = res[base + hd:base + 2 * hd]
            state = states[i] * w_scr[b, row, :] + sa * b_scr[b, row, :] + v_col * k_scr[b, row, :]
            s_scr[b] = state
            sr_scr[b] = state * r_scr[b, row, :]

    def readout(row, batches):
        res = _dot(jnp.concatenate([sr_scr[b].astype(bf16) for b in batches], axis=0), seg_b)
        for i, b in enumerate(batches):
            y = res[hd * i:hd * (i + 1)]
            y_scr[b, row, :] = jnp.sum(jnp.where(eye, y, 0.0), axis=0, keepdims=True)

    def step(t, carry):
        row = pl.ds(t, 1)
        prev_row = pl.ds(jnp.maximum(t - 1, 0), 1)
        fed = [feedback_dot(row, g) for g in groups]
        for g in groups:
            readout(prev_row, g)
        for g, f in zip(groups, fed):
            update(row, g, *f)
        return carry

    for b in range(nb):
        sr_scr[b] = jnp.zeros((hd, gw), f32)
    lax.fori_loop(0, n_steps, step, 0, unroll=RWKV_UNROLL)
    for g in groups:
        readout(pl.ds(n_steps - 1, 1), g)

    rows_all = (nb * tbr, gw)
    y = y_scr[...].reshape(rows_all)
    d = y - _dot_exact(y, seg_b) * (1.0 / HEAD_DIM)
    var = _dot_exact(d * d, seg_b) * (1.0 / HEAD_DIM)
    yn = d * lax.rsqrt(var + RWKV_LNX_EPS) * lg_ref[...] + lb_ref[...]
    o_ref[...] = ((yn + bonus_scr[...].reshape(rows_all)) * g_scr[...].reshape(rows_all)).reshape(per_batch)

    @pl.when(tb == pl.num_programs(1) - 1)
    def _fin():
        for b in range(nb):
            for h in range(N_HEADS):
                sfin_ref[b, h] = s_scr[b, :, HEAD_DIM * h:HEAD_DIM * (h + 1)]


def _rwkv(c, shift_prev, s0, p, seg_b, n_valid):
    bsz, t, _ = c.shape
    nb = RWKV_BATCH
    tbr = min(t, RWKV_TBLOCK)
    n_steps = min(n_valid, tbr)
    vec = lambda width: pl.BlockSpec((1, width), lambda b, i: (0, 0))
    mat = lambda rows: pl.BlockSpec((rows, GROUP_WIDTH), lambda b, i: (0, 0))
    rows_scr = pltpu.VMEM((nb, tbr, GROUP_WIDTH), f32)
    return pl.pallas_call(
        functools.partial(_rwkv_body, nb=nb, n_steps=n_steps),
        out_shape=(jax.ShapeDtypeStruct((bsz, t, GROUP_WIDTH), f32),
                   jax.ShapeDtypeStruct((bsz, N_HEADS, HEAD_DIM, HEAD_DIM), f32)),
        grid=(bsz // nb, t // tbr),
        in_specs=[pl.BlockSpec((nb, tbr, RWKV_COLS), lambda b, i: (b, i, 0)),
                  pl.BlockSpec((nb, 1, RWKV_COLS), lambda b, i: (b, 0, 0)),
                  pl.BlockSpec((nb, N_HEADS, HEAD_DIM, HEAD_DIM), lambda b, i: (b, 0, 0, 0)),
                  vec(RWKV_COLS), vec(GROUP_WIDTH), mat(LANES), mat(LANES), vec(GROUP_WIDTH), mat(LANES), mat(LANES),
                  mat(RWKV_G_RANK), mat(RWKV_G_RANK),
                  vec(GROUP_WIDTH), vec(GROUP_WIDTH), vec(GROUP_WIDTH), vec(GROUP_WIDTH), vec(GROUP_WIDTH),
                  mat(GROUP_WIDTH)],
        out_specs=(pl.BlockSpec((nb, tbr, GROUP_WIDTH), lambda b, i: (b, i, 0)),
                   pl.BlockSpec((nb, N_HEADS, HEAD_DIM, HEAD_DIM), lambda b, i: (b, 0, 0, 0))),
        scratch_shapes=[pltpu.VMEM((nb, HEAD_DIM, GROUP_WIDTH), f32), pltpu.VMEM((nb, 1, RWKV_COLS), f32)]
        + [rows_scr] * 9 + [pltpu.VMEM((nb, HEAD_DIM, GROUP_WIDTH), f32)],
        compiler_params=_params("parallel", "arbitrary"),
        name="rwkv",
    )(c, shift_prev, s0, p["mu"], p["w0"], *p["w2"], p["a0"], *p["a2"], *p["g2"], p["kk"], p["ka"], p["rk"],
      p["lnx_g"], p["lnx_b"], seg_b)


def _merge_rows(acc, mrun, den, o_new, m_new, d_new):
    m = jnp.maximum(mrun, m_new)
    a_old, a_new = jnp.exp(mrun - m), jnp.exp(m_new - m)
    return acc * a_old + o_new * a_new, m, den * a_old + d_new * a_new


def _dil_prompt_body(x_ref, o_ref, kv0_ref, kv1_ref, kv2_ref, acc_scr, m_scr, d_scr, *, t):
    grp = pl.program_id(1)
    blk = LANES
    n_pairs = GROUP_WIDTH // LANES
    rr, cc = _iota((blk, blk), 0), _iota((blk, blk), 1)
    first_head = _iota((1, LANES), 1) < HEAD_DIM
    neg_inf = float("-inf")
    kv_refs = (kv0_ref, kv1_ref, kv2_ref)

    def group(gi, window, dilation, first):
        n_sub = t // dilation // blk
        has_prev = n_sub > 1
        dist = (rr - cc).astype(f32) * float(dilation)
        valid = rr >= cc
        if has_prev:
            dist = jnp.concatenate([(rr + blk - cc).astype(f32) * float(dilation), dist], axis=1)
            valid = jnp.concatenate([cc >= rr, valid], axis=1)
            in_prev = _iota((1, 2 * blk), 1) < blk

        def keys_values(slab, rows, prows):
            cur = x_ref[0, slab, rows, :].astype(bf16)
            return jnp.concatenate([x_ref[0, slab, prows, :].astype(bf16), cur], axis=0) if has_prev else cur

        def scores(idx):
            res, sub = idx // n_sub, idx % n_sub
            start = res + dilation * blk * sub
            pstart = jnp.maximum(start - dilation * blk, 0)
            rows = pl.ds(start, blk, stride=dilation) if dilation > 1 else pl.ds(start, blk)
            prows = pl.ds(pstart, blk, stride=dilation) if dilation > 1 else pl.ds(pstart, blk)
            raw = []
            for pair in range(n_pairs):
                q = x_ref[0, pair, rows, :] * (HEAD_DIM ** -0.5)
                keys = keys_values(2 + pair, rows, prows)
                for hh in range(2):
                    lanes = first_head if hh == 0 else jnp.logical_not(first_head)
                    raw.append(_dot_tr(jnp.where(lanes, q, 0.0).astype(bf16), keys))
            return rows, prows, sub, raw

        def finish(rows, prows, sub, raw):
            if has_prev:
                gate = jnp.where(in_prev, jnp.where(sub > 0, 0.0, neg_inf), 0.0)
            for pair in range(n_pairs):
                vals = keys_values(4 + pair, rows, prows)
                vals = jnp.concatenate([vals, jnp.ones(vals.shape, bf16)], axis=1)
                o_pair = m_pair = d_pair = None
                for hh in range(2):
                    slope = _alibi_slope(gi * N_HEADS + 2 * pair + hh)
                    lanes = first_head if hh == 0 else jnp.logical_not(first_head)
                    s = jnp.where(valid, raw[2 * pair + hh] - slope * dist, neg_inf)
                    if has_prev:
                        s = s + gate
                    mx = jnp.max(s, axis=-1, keepdims=True)
                    pv = _dot(jnp.exp(s - mx).astype(bf16), vals)
                    o_h, d_h = pv[:, :LANES], pv[:, LANES:]
                    m_h = jnp.broadcast_to(mx, (blk, LANES))
                    if hh == 0:
                        o_pair, m_pair, d_pair = o_h, m_h, d_h
                    else:
                        o_pair = jnp.where(lanes, o_h, o_pair)
                        m_pair = jnp.where(lanes, m_h, m_pair)
                        d_pair = jnp.where(lanes, d_h, d_pair)
                if not first:
                    o_pair, m_pair, d_pair = _merge_rows(acc_scr[pair, rows, :], m_scr[pair, rows, :],
                                                         d_scr[pair, rows, :], o_pair, m_pair, d_pair)
                acc_scr[pair, rows, :] = o_pair
                m_scr[pair, rows, :] = m_pair
                d_scr[pair, rows, :] = d_pair

        def combos(i, carry):
            ctx = [scores(DIL_COMBOS * i + j) for j in range(DIL_COMBOS)]
            for c in ctx:
                finish(*c)
            return carry

        lax.fori_loop(0, t // blk // DIL_COMBOS, combos, 0)

        w = min(window, t)
        for j in range(2 * n_pairs):
            for c in range(w // blk):
                tile = x_ref[0, n_pairs + j, t - w + c * blk:t - w + (c + 1) * blk, :]
                kv_refs[gi][0, j * LANES:(j + 1) * LANES, c * blk:(c + 1) * blk] = tile.T

    for step, gi in enumerate(reversed(range(len(DIL_PAIRS)))):
        pl.when(grp == step)(functools.partial(group, gi, *DIL_PAIRS[gi], step == 0))

    @pl.when(grp == len(DIL_PAIRS) - 1)
    def _fin():
        o_ref[...] = acc_scr[...] / d_scr[...]


def _dil_prompt(x, bsz, t):
    n_pairs = GROUP_WIDTH // LANES
    scr = pltpu.VMEM((n_pairs, t, LANES), f32)
    kv_rows = [min(w, t) for w, _ in DIL_PAIRS]
    return pl.pallas_call(
        functools.partial(_dil_prompt_body, t=t),
        out_shape=(jax.ShapeDtypeStruct((n_pairs, bsz * t, LANES), f32),)
        + tuple(jax.ShapeDtypeStruct((bsz, 2 * GROUP_WIDTH, r), f32) for r in kv_rows),
        grid=(bsz, len(DIL_PAIRS)),
        in_specs=[pl.BlockSpec((1, DIL_SLABS, t, LANES), lambda b, g: (len(DIL_PAIRS) - 1 - g, 0, b, 0))],
        out_specs=(pl.BlockSpec((n_pairs, t, LANES), lambda b, g: (0, b, 0)),)
        + tuple(pl.BlockSpec((1, 2 * GROUP_WIDTH, r), lambda b, g: (b, 0, 0)) for r in kv_rows),
        scratch_shapes=[scr, scr, scr],
        compiler_params=_params("parallel", "arbitrary"),
        name="dil_prompt",
    )(x)


def _dil_sample_body(x_ref, c0_ref, c1_ref, c2_ref, o_ref, *, t_new):
    caches = (c0_ref, c1_ref, c2_ref)
    nq = SUBLANES
    per = nq // t_new
    first_head = _iota((1, LANES), 1) < HEAD_DIM
    neg_inf = float("-inf")
    q_idx = _iota((nq, 1), 0) & (t_new - 1)
    q_bat = _iota((nq, 1), 0) >> 2
    rn, cn = _iota((nq, nq), 0), _iota((nq, nq), 1)
    dn = (rn & (t_new - 1)) - (cn & (t_new - 1))
    same = (rn >> 2) == (cn >> 2)
    for pair in range(GROUP_WIDTH // LANES):
        acc = mrun = den = None
        for gi, (window, dilation) in enumerate(DIL_PAIRS):
            q = x_ref[gi, pair] * (HEAD_DIM ** -0.5)
            kn = x_ref[gi, 2 + pair].astype(bf16)
            vn = x_ref[gi, 4 + pair].astype(bf16)
            dist_c = window + q_idx - _iota((nq, window), 1)
            ok_c = jnp.logical_and((dist_c & (dilation - 1)) == 0, dist_c <= window)
            ok_n = jnp.logical_and(jnp.logical_and(same, dn >= 0), (dn & (dilation - 1)) == 0)
            q2 = jnp.concatenate([jnp.where(first_head, q, 0.0), jnp.where(first_head, 0.0, q)], axis=0).astype(bf16)
            slope2 = jnp.where(_iota((2 * nq, 1), 0) < nq, _alibi_slope(gi * N_HEADS + 2 * pair),
                               _alibi_slope(gi * N_HEADS + 2 * pair + 1))
            dist_c2 = jnp.concatenate([dist_c, dist_c], axis=0).astype(f32)
            ok_c2 = jnp.concatenate([ok_c, ok_c], axis=0)
            dn2 = jnp.concatenate([dn, dn], axis=0).astype(f32)
            ok_n2 = jnp.concatenate([ok_n, ok_n], axis=0)
            s_new = jnp.where(ok_n2, _dot_tr(q2, kn) - slope2 * dn2, neg_inf)
            mx_new = jnp.max(s_new, axis=-1, keepdims=True)
            o_pair = m_pair = d_pair = None
            for s in range(per):
                kt = caches[gi][0, s, pair * LANES:(pair + 1) * LANES, :].astype(bf16)
                vt = caches[gi][0, s, GROUP_WIDTH + pair * LANES:GROUP_WIDTH + (pair + 1) * LANES, :].astype(bf16)
                s_c = jnp.where(ok_c2, _dot(q2, kt) - slope2 * dist_c2, neg_inf)
                mx = jnp.maximum(jnp.max(s_c, axis=-1, keepdims=True), mx_new)
                p_c, p_n = jnp.exp(s_c - mx), jnp.exp(s_new - mx)
                o2 = _dot_tr(p_c.astype(bf16), vt) + _dot(p_n.astype(bf16), vn)
                d2 = jnp.sum(p_c, axis=-1, keepdims=True) + jnp.sum(p_n, axis=-1, keepdims=True)
                o_s = jnp.where(first_head, o2[:nq], o2[nq:])
                m_s = jnp.where(first_head, jnp.broadcast_to(mx[:nq], (nq, LANES)), jnp.broadcast_to(mx[nq:], (nq, LANES)))
                d_s = jnp.where(first_head, jnp.broadcast_to(d2[:nq], (nq, LANES)), jnp.broadcast_to(d2[nq:], (nq, LANES)))
                if s == 0:
                    o_pair, m_pair, d_pair = o_s, m_s, d_s
                else:
                    mine = q_bat == s
                    o_pair = jnp.where(mine, o_s, o_pair)
                    m_pair = jnp.where(mine, m_s, m_pair)
                    d_pair = jnp.where(mine, d_s, d_pair)
            if gi == 0:
                acc, mrun, den = o_pair, m_pair, d_pair
            else:
                acc, mrun, den = _merge_rows(acc, mrun, den, o_pair, m_pair, d_pair)
        o_ref[pair] = acc / den


def _dil_sample(x, caches, layer, bsz, t_new):
    n_pairs = GROUP_WIDTH // LANES
    per = SUBLANES // t_new
    return pl.pallas_call(
        functools.partial(_dil_sample_body, t_new=t_new),
        out_shape=jax.ShapeDtypeStruct((n_pairs, bsz * t_new, LANES), f32),
        grid=(bsz // per,),
        in_specs=[pl.BlockSpec((len(DIL_PAIRS), DIL_SLABS, SUBLANES, LANES), lambda i: (0, 0, i, 0))]
        + [pl.BlockSpec((1, per, 2 * GROUP_WIDTH, w), lambda i: (layer, i, 0, 0)) for w, _ in DIL_PAIRS],
        out_specs=pl.BlockSpec((n_pairs, SUBLANES, LANES), lambda i: (0, i, 0)),
        compiler_params=_params("parallel"),
        name="dil_sample",
    )(x, *caches)


def _row(v):
    return v.reshape(1, -1)


def _prep_layer(l, norm_gains, w_ff_gate, w_ff_up, w_ff_down, w_in, w_out, gla_w_alpha2, gla_b_alpha, gla_norm,
                gmlp_ln_g, gmlp_ln_b, gmlp_ws, gmlp_bs, rwkv_mu, rwkv_w0, rwkv_w2, rwkv_a0, rwkv_a2, rwkv_g2,
                rwkv_kk, rwkv_ka, rwkv_rk, rwkv_lnx_g, rwkv_lnx_b):
    win = w_in[l]
    o1, o2, o3 = GLA_COLS, GLA_COLS + GMLP_COLS, GLA_COLS + GMLP_COLS + RWKV_COLS
    hk = N_HEADS * GLA_DK
    a_lo, a_hi = 2 * hk + GROUP_WIDTH, 2 * hk + GROUP_WIDTH + GLA_RANK
    w_gla = win[:, :o1]
    w_gla = jnp.concatenate([w_gla[:, :a_lo], w_gla[:, a_hi:], w_gla[:, a_lo:a_hi],
                             jnp.zeros((D_MODEL, GLA_COLS_PAD - GLA_COLS), f32)], axis=1)
    zeros_r = jnp.zeros((RWKV_W_RANK, GROUP_WIDTH), f32)
    return dict(
        norms=[_row(norm_gains[l, i]) for i in range(norm_gains.shape[1])],
        w_gla=w_gla.astype(bf16), w_gmlp=win[:, o1:o2].astype(bf16), w_rwkv=win[:, o2:o3].astype(bf16),
        w_dil=win[:, o3:].astype(bf16),
        gla_wa=_split_param(jnp.concatenate([gla_w_alpha2[l], jnp.zeros((LANES - GLA_RANK, hk), f32)], axis=0)),
        gla_ba=_row(gla_b_alpha[l]), gla_norm=_row(gla_norm[l]),
        gmlp_ln_g=_row(gmlp_ln_g[l]), gmlp_ln_b=_row(gmlp_ln_b[l]), gmlp_ws=gmlp_ws[l], gmlp_bs=gmlp_bs[l],
        rwkv=dict(mu=_row(rwkv_mu[l]), w0=_row(rwkv_w0[l]),
                  w2=_split_param(jnp.concatenate([rwkv_w2[l], zeros_r], axis=0)), a0=_row(rwkv_a0[l]),
                  a2=_split_param(jnp.concatenate([zeros_r, rwkv_a2[l]], axis=0)), g2=_split_param(rwkv_g2[l]),
                  kk=_row(rwkv_kk[l]), ka=_row(rwkv_ka[l]), rk=_row(rwkv_rk[l]),
                  lnx_g=_row(rwkv_lnx_g[l]), lnx_b=_row(rwkv_lnx_b[l])),
    )


def _kv_rows(c_dil, bsz, t, rows):
    out = []
    for gi in range(len(DIL_PAIRS)):
        kv = c_dil[gi, 2:].reshape(2, GROUP_WIDTH // LANES, bsz, t, LANES)[:, :, :, t - rows[gi]:]
        kv = jnp.transpose(kv, (2, 3, 0, 1, 4))
        out.append(kv.reshape(bsz, rows[gi], 2, N_HEADS, HEAD_DIM))
    return out


def _channel_major_kv(kv):
    bsz, _, rows = kv.shape
    return jnp.transpose(kv.reshape(bsz, 2, N_HEADS, HEAD_DIM, rows), (0, 4, 1, 2, 3))


def _trunk_layer(x, p, ffw, seg_b, bsz, t, gla_h0, rwkv_s0, shift0, caches, layer):
    ng = p["norms"]
    n = bsz * t
    h = _ffn(x, ng[0], ng[1], ffw["gate"], ffw["up"], ffw["down"], layer, 0)
    c_gla, c_gmlp, c_rwkv, c_dil = _proj(h, ng[2], p["w_gla"], p["w_gmlp"], p["w_rwkv"], p["w_dil"])
    is_prompt = caches is None

    t_gla = max(t, GLA_CHUNK)
    cg = c_gla.reshape(bsz, t, GLA_COLS_PAD)
    if t_gla != t:
        cg = jnp.pad(cg, ((0, 0), (0, t_gla - t), (0, 0)))
    o_gla, gla_s = _gla(cg, gla_h0, p["gla_wa"], p["gla_ba"], p["gla_norm"], min(t, GLA_CHUNK))
    o_gla = o_gla[:, :t].reshape(n, GROUP_WIDTH)

    if t >= GMLP_CHUNK:
        ws = p["gmlp_ws"]
        bias_rows = jnp.repeat(p["gmlp_bs"].T, HEAD_DIM, axis=1)
    else:
        per = GMLP_CHUNK // t
        eye = jnp.eye(per, dtype=f32)
        ws = jnp.einsum("ab,gij->gaibj", eye, p["gmlp_ws"][:, :t, :t]).reshape(N_HEADS, GMLP_CHUNK, GMLP_CHUNK)
        bias_rows = jnp.tile(jnp.repeat(p["gmlp_bs"][:, :t].T, HEAD_DIM, axis=1), (per, 1))
    gm = _gmlp(c_gmlp, p["gmlp_ln_g"], p["gmlp_ln_b"], ws, bias_rows, emit_v=not is_prompt)
    o_gmlp = gm[0]
    gmlp_v = None if is_prompt else gm[1]

    t_rw = max(t, SUBLANES)
    cr = c_rwkv.reshape(bsz, t, RWKV_COLS)
    shift = cr[:, -1]
    if t_rw != t:
        cr = jnp.pad(cr, ((0, 0), (0, t_rw - t), (0, 0)))
    o_rwkv, rwkv_s = _rwkv(cr, shift0.reshape(bsz, 1, RWKV_COLS), rwkv_s0, p["rwkv"], seg_b, t)
    o_rwkv = o_rwkv[:, :t].reshape(n, GROUP_WIDTH)

    if is_prompt:
        o_dil, *kv_cm = _dil_prompt(c_dil, bsz, t)
        kv_new = [_channel_major_kv(kv) for kv in kv_cm]
    else:
        o_dil = _dil_sample(c_dil, caches, layer, bsz, t)
        kv_new = _kv_rows(c_dil, bsz, t, [min(w, t) for w, _ in DIL_PAIRS])

    y = _ffn(h, ng[4], ng[5], ffw["gate"], ffw["up"], ffw["down"], layer, 1,
             mix=(o_gla, o_gmlp, o_rwkv, o_dil, ffw["out"], ng[3]))
    return y, gla_s, rwkv_s, shift, gmlp_v, kv_new


def kernel(x_prompt, x_sample, cache_win128, cache_win512, cache_win2048, state_gla, state_rwkv, state_shift,
           norm_gains, w_ff_gate, w_ff_up, w_ff_down, w_in, w_out, gla_w_alpha2, gla_b_alpha, gla_norm,
           gmlp_ln_g, gmlp_ln_b, gmlp_ws, gmlp_bs, rwkv_mu, rwkv_w0, rwkv_w2, rwkv_a0, rwkv_a2, rwkv_g2,
           rwkv_kk, rwkv_ka, rwkv_rk, rwkv_lnx_g, rwkv_lnx_b):
    bp, tp, _ = x_prompt.shape
    bs, ts, _ = x_sample.shape
    depth = norm_gains.shape[0]
    assert tp % GMLP_CHUNK == 0 and tp % (DIL_PAIRS[-1][1] * LANES) == 0 and bp % RWKV_BATCH == 0
    assert ts == 4 and bs % RWKV_BATCH == 0 and (bs * ts) % GMLP_CHUNK == 0
    caches = [jnp.transpose(c, (0, 1, 3, 4, 5, 2)).reshape(depth, bs, 2 * GROUP_WIDTH, c.shape[2])
              for c in (cache_win128, cache_win512, cache_win2048)]
    lanes = np.arange(GROUP_WIDTH)
    seg_b = jnp.asarray((lanes[:, None] // HEAD_DIM) == (lanes[None, :] // HEAD_DIM), dtype=bf16)
    ffw = dict(gate=w_ff_gate.astype(bf16), up=w_ff_up.astype(bf16), down=w_ff_down.astype(bf16),
               out=w_out.astype(bf16))
    gla0 = jnp.zeros((bp, N_HEADS, GLA_DK, HEAD_DIM), f32)
    rwkv0 =jnp.zeros((bp, N_HEADS, HEAD_DIM, HEAD_DIM), f32)
    shift0 = jnp.zeros((bp, RWKV_COLS), f32)
    yp = x_prompt.reshape(bp * tp, D_MODEL)
    ys = x_sample.reshape(bs * ts, D_MODEL)
    pg, pr, psh, pw = [], [], [], [[], [], []]
    sg, sr, ssh, sw, sv = [], [], [], [[], [], []], []
    for l in range(depth):
        p = _prep_layer(l, norm_gains, w_ff_gate, w_ff_up, w_ff_down, w_in, w_out, gla_w_alpha2, gla_b_alpha,
                        gla_norm, gmlp_ln_g, gmlp_ln_b, gmlp_ws, gmlp_bs, rwkv_mu, rwkv_w0, rwkv_w2, rwkv_a0,
                        rwkv_a2, rwkv_g2, rwkv_kk, rwkv_ka, rwkv_rk, rwkv_lnx_g, rwkv_lnx_b)
        yp, g_p, r_p, sh_p, _, kv_p = _trunk_layer(yp, p, ffw, seg_b, bp, tp, gla0, rwkv0, shift0, None, l)
        ys, g_s, r_s, sh_s, v_s, kv_s = _trunk_layer(ys, p, ffw, seg_b, bs, ts, state_gla[l], state_rwkv[l],
                                                     state_shift[l], caches, l)
        pg.append(g_p)
        pr.append(r_p)
        psh.append(sh_p)
        sg.append(g_s)
        sr.append(r_s)
        ssh.append(sh_s)
        sv.append(v_s.reshape(bs, ts, GROUP_WIDTH))
        for gi in range(len(DIL_PAIRS)):
            pw[gi].append(kv_p[gi])
            sw[gi].append(kv_s[gi])
    return (yp.reshape(bp, tp, D_MODEL), ys.reshape(bs, ts, D_MODEL),
            jnp.stack(pg), jnp.stack(pr), jnp.stack(psh),
            jnp.stack(pw[0]), jnp.stack(pw[1]), jnp.stack(pw[2]),
            jnp.stack(sg), jnp.stack(sr), jnp.stack(ssh),
            jnp.stack(sw[0]), jnp.stack(sw[1]), jnp.stack(sw[2]),
            jnp.stack(sv))
```
